```python
import math
import jax, jax.numpy as jnp
from jax import lax
import numpy as np

D_MODEL = 4096
BATCH = 16
SEQ = 256
DEPTH = 2
DEC_BATCH = 2
DEC_SEQ = 4096
PAST_LEN = 512

GRID_W = 64
N_HEADS = 16
N_KV_HEADS = 4
HEAD_DIM = 128
ATT_W = N_HEADS * HEAD_DIM
KV_W = N_KV_HEADS * HEAD_DIM
ROT_FREQS = HEAD_DIM // 4
ROPE_THETA = 10000.0
Q_BLOCK = 128
HY_W = 2048
HY_ORDER = 2
HY_SHORT = 3
HY_BANDS = 16
HY_EMB = 1 + 2 * HY_BANDS
HY_FFN = 64
HY_DECAY_TARGET = 1e-2
HY_FAST_DECAY = 0.3
HY_SLOW_DECAY = 1.5
SSM_W = 2048
SSM_HEADDIM = 64
SSM_HEADS = SSM_W // SSM_HEADDIM
SSM_GROUPS = 8
SSM_STATE = 128
SSM_CONV = 3
SSM_CHUNK = 128
SSM_CONV_DIM = SSM_W + 2 * SSM_GROUPS * SSM_STATE
N_EXPERTS = 16
EC_CAPACITY = 2
MOE_FF = 2048
N_BRANCH = 3
Q_OFF = 0
K_OFF = Q_OFF + ATT_W
V_OFF = K_OFF + KV_W
HY_OFF = V_OFF + KV_W
Z_OFF = HY_OFF + 3 * HY_W
XBC_OFF = Z_OFF + SSM_W
DT_OFF = XBC_OFF + SSM_CONV_DIM
GATE_OFF = DT_OFF + 2 * SSM_HEADS
N_IN = GATE_OFF + N_BRANCH * D_MODEL
ALPHA = (2 * DEPTH) ** 0.25
BETA = (8 * DEPTH) ** -0.25
LN_EPS = 1e-5
RMS_EPS = 1e-6

kernel_name = "hybrid_dit_hyena_gqa_ssd_ecmoe_step"

F32 = jnp.float32


def layer_norm(x, g, b):
    xf = x.astype(F32)
    mu = jnp.mean(xf, axis=-1, keepdims=True)
    var = jnp.mean(jnp.square(xf - mu), axis=-1, keepdims=True)
    return ((xf - mu) * lax.rsqrt(var + LN_EPS)).astype(x.dtype) * g + b


def rms_norm(x, g):
    xf = x.astype(F32)
    return (xf * lax.rsqrt(jnp.mean(jnp.square(xf), axis=-1, keepdims=True) + RMS_EPS)).astype(x.dtype) * g


def modulation(cond, w_mod, b_mod):
    m = jax.nn.silu(cond) @ w_mod + b_mod
    return jnp.split(m[:, None, :], 6, axis=-1)


def conv_centered(x, w, b):
    k = w.shape[0]
    pad = k // 2
    length = x.shape[1]
    xp = jnp.pad(x, ((0, 0), (pad, pad), (0, 0)))
    return sum(xp[:, i:i + length] * w[i] for i in range(k)) + b


def axial_rope(length):
    rows = length // GRID_W
    t = jnp.arange(rows * GRID_W)
    row = (t // GRID_W).astype(F32)
    col = (t % GRID_W).astype(F32)
    inv = ROPE_THETA ** (-jnp.arange(ROT_FREQS, dtype=F32) / ROT_FREQS)
    ang = jnp.stack([row[:, None] * inv, col[:, None] * inv], axis=1)
    return jnp.cos(ang), jnp.sin(ang)


def apply_rope(x, cos, sin):
    xr = x.reshape(*x.shape[:-1], 2, 2, ROT_FREQS)
    x1, x2 = xr[..., 0, :], xr[..., 1, :]
    c = cos[None, :, None].astype(x.dtype)
    s = sin[None, :, None].astype(x.dtype)
    return jnp.stack([x1 * c - x2 * s, x2 * c + x1 * s], axis=-2).reshape(x.shape)


def blocked_attention(q, k, v):
    bsz, lq = q.shape[0], q.shape[1]
    nb = lq // Q_BLOCK
    rep = N_HEADS // N_KV_HEADS
    qb = q.reshape(bsz, nb, Q_BLOCK, N_KV_HEADS, rep, HEAD_DIM).swapaxes(0, 1)
    scale = HEAD_DIM ** -0.5

    def block(qi):
        s = jnp.einsum('bqgrd,bkgd->bgrqk', qi, k).astype(F32) * scale
        p = jax.nn.softmax(s, axis=-1).astype(v.dtype)
        return jnp.einsum('bgrqk,bkgd->bqgrd', p, v)

    o = lax.map(block, qb)
    return o.swapaxes(0, 1).reshape(bsz, lq, ATT_W)


def hyena_filters(length, w1, b1, freq, w2, b2, w3, b3):
    t = jnp.arange(length, dtype=F32) / length
    bands = jnp.linspace(1e-4, HY_BANDS - 1, HY_BANDS, dtype=F32)
    ang = 2.0 * math.pi * t[:, None] * bands
    feats = jnp.concatenate([t[:, None], jnp.cos(ang), jnp.sin(ang)], axis=-1)
    fr = freq.astype(F32)
    hid = jnp.sin(fr * (feats @ w1.astype(F32) + b1.astype(F32)))
    hid = jnp.sin(fr * (hid @ w2.astype(F32) + b2.astype(F32)))
    h = (hid @ w3.astype(F32) + b3.astype(F32)).reshape(length, HY_ORDER, 2, HY_W)
    deltas = jnp.abs(jnp.linspace(math.log(HY_DECAY_TARGET) / HY_SLOW_DECAY,
                                  math.log(HY_DECAY_TARGET) / HY_FAST_DECAY, HY_W, dtype=F32))
    h = h * jnp.exp(-t[:, None] * deltas)[:, None, None]
    g = jnp.concatenate([h[:, :, 0], jnp.zeros((1, HY_ORDER, HY_W), F32), h[:0:-1, :, 1]], axis=0)
    g = g / jnp.sum(jnp.abs(g), axis=0, keepdims=True)
    return jnp.fft.rfft(g, axis=0)


def long_conv(z, gf):
    length = z.shape[1]
    zf = jnp.fft.rfft(z.astype(F32), n=2 * length, axis=1)
    return jnp.fft.irfft(zf * gf, n=2 * length, axis=1)[:, :length].astype(z.dtype)


def hyena_branch(u, p):
    length = u.shape[1]
    proj = conv_centered(u, p['hy_conv_w'], p['hy_conv_b'])
    v, x1, x2 = jnp.split(proj, 3, axis=-1)
    gf = hyena_filters(length, p['hy_w1'], p['hy_b1'], p['hy_freq'], p['hy_w2'], p['hy_b2'], p['hy_w3'], p['hy_b3'])
    z = v
    for n, gate in enumerate((x1, x2)):
        z = gate * (long_conv(z, gf[:, n]) + p['hy_bias'][n] * z)
    return z


def ssd_scan(x, dt, a, b, c, init):
    bsz, length, nh, hp = x.shape
    ng, ns = b.shape[2], b.shape[3]
    rep = nh // ng
    nc = length // SSM_CHUNK
    tt = SSM_CHUNK
    xdt = (x.astype(F32) * dt[..., None]).reshape(bsz, nc, tt, ng, rep, hp)
    bc = b.astype(F32).reshape(bsz, nc, tt, ng, ns)
    cc = c.astype(F32).reshape(bsz, nc, tt, ng, ns)
    acs = jnp.cumsum((dt * a).reshape(bsz, nc, tt, ng, rep), axis=2)
    seg = acs[:, :, :, None] - acs[:, :, None, :]
    causal = jnp.tril(jnp.ones((tt, tt), bool))[None, None, :, :, None, None]
    decay = jnp.exp(jnp.where(causal, seg, -jnp.inf))
    cb = jnp.einsum('bclgn,bcsgn->bclsg', cc, bc)
    y_diag = jnp.einsum('bclsgr,bcsgrp->bclgrp', cb[..., None] * decay, xdt)
    to_end = jnp.exp(acs[:, :, -1:] - acs)
    chunk_states = jnp.einsum('bcsgn,bcsgrp->bcgrpn', bc, xdt * to_end[..., None])
    chunk_decay = jnp.exp(acs[:, :, -1])

    def carry(s, inp):
        st, dec = inp
        return s * dec[..., None, None] + st, s

    final, entering = lax.scan(carry, init.astype(F32).reshape(bsz, ng, rep, hp, ns),
                               (jnp.moveaxis(chunk_states, 1, 0), jnp.moveaxis(chunk_decay, 1, 0)))
    entering = jnp.moveaxis(entering, 0, 1)
    y_off = jnp.einsum('bclgn,bcgrpn->bclgrp', cc, entering) * jnp.exp(acs)[..., None]
    y = (y_diag + y_off).reshape(bsz, length, nh, hp).astype(x.dtype)
    return y, final.reshape(bsz, nh, hp, ns).astype(x.dtype)


def ssd_branch(zs, xbc, dt_raw, p, init):
    bsz, length = xbc.shape[0], xbc.shape[1]
    gn = SSM_GROUPS * SSM_STATE
    xbc = jax.nn.silu(conv_centered(xbc, p['ssm_conv_w'], p['ssm_conv_b']))
    xs = xbc[..., :SSM_W].reshape(bsz, length, SSM_HEADS, SSM_HEADDIM)
    bm = xbc[..., SSM_W:SSM_W + gn].reshape(bsz, length, SSM_GROUPS, SSM_STATE)
    cm = xbc[..., SSM_W + gn:].reshape(bsz, length, SSM_GROUPS, SSM_STATE)
    dt = jax.nn.softplus(dt_raw.astype(F32).reshape(bsz, length, 2, SSM_HEADS) + p['ssm_dt_bias'].astype(F32))
    a = -jnp.exp(p['ssm_a_log'].astype(F32))
    y_f, s_f = ssd_scan(xs, dt[:, :, 0], a[0], bm, cm, init[:, 0])
    y_b, s_b = ssd_scan(xs[:, ::-1], dt[:, ::-1, 1], a[1], bm[:, ::-1], cm[:, ::-1], init[:, 1])
    y = y_f + y_b[:, ::-1] + p['ssm_d'][:, None] * xs
    y = rms_norm(y.reshape(bsz, length, SSM_W) * jax.nn.silu(zs), p['ssm_norm'])
    return y, jnp.stack([s_f, s_b], axis=1)


def mixer(h, p, rope, ctx_k, ctx_v, init):
    u = h @ p['w_in']
    bsz, length = u.shape[0], u.shape[1]
    q = rms_norm(u[..., Q_OFF:K_OFF].reshape(bsz, length, N_HEADS, HEAD_DIM), p['q_norm'])
    k = rms_norm(u[..., K_OFF:V_OFF].reshape(bsz, length, N_KV_HEADS, HEAD_DIM), p['k_norm'])
    v = u[..., V_OFF:HY_OFF].reshape(bsz, length, N_KV_HEADS, HEAD_DIM)
    if rope is None:
        k_all, v_all = k, v
    else:
        q = apply_rope(q, *rope)
        k_all = jnp.concatenate([ctx_k, apply_rope(k, *rope)], axis=1)
        v_all = jnp.concatenate([ctx_v, v], axis=1)
    att = blocked_attention(q, k_all, v_all)
    hy = hyena_branch(u[..., HY_OFF:Z_OFF], p)
    ssm, states = ssd_branch(u[..., Z_OFF:XBC_OFF], u[..., XBC_OFF:DT_OFF], u[..., DT_OFF:GATE_OFF], p, init)
    gates = jax.nn.sigmoid(u[..., GATE_OFF:].reshape(bsz, length, N_BRANCH, D_MODEL))
    merged = (gates[:, :, 0] * (att @ p['w_br_att'])
              + gates[:, :, 1] * (hy @ p['w_br_hy'])
              + gates[:, :, 2] * (ssm @ p['w_br_ssm']))
    return merged @ p['w_out'], (k, v, states)


def ec_route_one(h, w_router, w_gate, w_up, w_down):
    n = h.shape[0]
    cap = EC_CAPACITY * n // N_EXPERTS
    aff = jax.nn.softmax((h @ w_router).astype(F32), axis=-1)
    g, idx = lax.top_k(aff.T, cap)
    xs = h[idx]
    hid = jax.nn.silu(jnp.einsum('ecd,edf->ecf', xs, w_gate)) * jnp.einsum('ecd,edf->ecf', xs, w_up)
    y = jnp.einsum('ecf,efd->ecd', hid, w_down) * g[..., None].astype(h.dtype)
    return jnp.zeros_like(h).at[idx.reshape(-1)].add(y.reshape(-1, D_MODEL))


def ec_moe(h, p):
    return jax.vmap(ec_route_one, in_axes=(0, None, None, None, None))(
        h, p['w_router'], p['w_gate'], p['w_up'], p['w_down'])


def trunk_layer(x, cond, p, rope, ctx_k, ctx_v, init):
    sh1, sc1, g1, sh2, sc2, g2 = modulation(cond, p['w_mod'], p['b_mod'])
    m, ctx = mixer(x * (1 + sc1) + sh1, p, rope, ctx_k, ctx_v, init)
    x = layer_norm(ALPHA * x + g1 * m, p['ln1_g'], p['ln1_b'])
    f = ec_moe(x * (1 + sc2) + sh2, p)
    x = layer_norm(ALPHA * x + g2 * f, p['ln2_g'], p['ln2_b'])
    return x, ctx


def _normal(key, shape, scale):
    return jax.random.normal(key, shape, F32) * scale


def setup_inputs(seed: int = 0) -> dict:
    key = jax.random.key(seed)
    ks = list(jax.random.split(key, 48))
    dt0 = jnp.exp(jax.random.uniform(ks[40], (DEPTH, 2, SSM_HEADS), F32) * (math.log(0.1) - math.log(1e-3)) + math.log(1e-3))
    return {
        'x_prompt': _normal(ks[0], (BATCH, SEQ, D_MODEL), 1.0),
        'x_sample': _normal(ks[1], (DEC_BATCH, DEC_SEQ, D_MODEL), 1.0),
        'cache_k': _normal(ks[2], (DEC_BATCH, DEPTH, PAST_LEN, N_KV_HEADS, HEAD_DIM), 1.0),
        'cache_v': _normal(ks[3], (DEC_BATCH, DEPTH, PAST_LEN, N_KV_HEADS, HEAD_DIM), 1.0),
        'state_ssm': _normal(ks[4], (DEC_BATCH, DEPTH, 2, SSM_HEADS, SSM_HEADDIM, SSM_STATE), 0.5),
        'c': _normal(ks[5], (DEC_BATCH, D_MODEL), 1.0),
        'c_ctx': _normal(ks[6], (D_MODEL,), 1.0),
        'w_mod': _normal(ks[7], (DEPTH, D_MODEL, 6 * D_MODEL), 0.5 * D_MODEL ** -0.5),
        'b_mod': _normal(ks[8], (DEPTH, 6 * D_MODEL), 0.02),
        'w_in': _normal(ks[9], (DEPTH, D_MODEL, N_IN), D_MODEL ** -0.5),
        'q_norm': 1.0 + _normal(ks[10], (DEPTH, HEAD_DIM), 0.02),
        'k_norm': 1.0 + _normal(ks[11], (DEPTH, HEAD_DIM), 0.02),
        'hy_conv_w': _normal(ks[12], (DEPTH, HY_SHORT, 3 * HY_W), HY_SHORT ** -0.5),
        'hy_conv_b': _normal(ks[13], (DEPTH, 3 * HY_W), 0.02),
        'hy_w1': _normal(ks[14], (DEPTH, HY_EMB, HY_FFN), HY_EMB ** -0.5),
        'hy_b1': _normal(ks[15], (DEPTH, HY_FFN), 0.02),
        'hy_freq': 1.0 + _normal(ks[16], (DEPTH, HY_FFN), 0.02),
        'hy_w2': _normal(ks[17], (DEPTH, HY_FFN, HY_FFN), HY_FFN ** -0.5),
        'hy_b2': _normal(ks[18], (DEPTH, HY_FFN), 0.02),
        'hy_w3': _normal(ks[19], (DEPTH, HY_FFN, HY_ORDER * 2 * HY_W), HY_FFN ** -0.5),
        'hy_b3': _normal(ks[20], (DEPTH, HY_ORDER * 2 * HY_W), 0.02),
        'hy_bias': _normal(ks[21], (DEPTH, HY_ORDER, HY_W), 1.0),
        'ssm_conv_w': _normal(ks[22], (DEPTH, SSM_CONV, SSM_CONV_DIM), SSM_CONV ** -0.5),
        'ssm_conv_b': _normal(ks[23], (DEPTH, SSM_CONV_DIM), 0.02),
        'ssm_dt_bias': dt0 + jnp.log(-jnp.expm1(-dt0)),
        'ssm_a_log': jnp.log(jax.random.uniform(ks[24], (DEPTH, 2, SSM_HEADS), F32, 1.0, 16.0)),
        'ssm_d': 1.0 + _normal(ks[25], (DEPTH, SSM_HEADS), 0.02),
        'ssm_norm': 1.0 + _normal(ks[26], (DEPTH, SSM_W), 0.02),
        'w_br_att': _normal(ks[27], (DEPTH, ATT_W, D_MODEL), ATT_W ** -0.5),
        'w_br_hy': _normal(ks[28], (DEPTH, HY_W, D_MODEL), HY_W ** -0.5),
        'w_br_ssm': _normal(ks[29], (DEPTH, SSM_W, D_MODEL), SSM_W ** -0.5),
        'w_out': _normal(ks[30], (DEPTH, D_MODEL, D_MODEL), BETA * D_MODEL ** -0.5),
        'ln1_g': 1.0 + _normal(ks[31], (DEPTH, D_MODEL), 0.02),
        'ln1_b': _normal(ks[32], (DEPTH, D_MODEL), 0.02),
        'w_router': _normal(ks[33], (DEPTH, D_MODEL, N_EXPERTS), D_MODEL ** -0.5),
        'w_gate': _normal(ks[34], (DEPTH, N_EXPERTS, D_MODEL, MOE_FF), D_MODEL ** -0.5),
        'w_up': _normal(ks[35], (DEPTH, N_EXPERTS, D_MODEL, MOE_FF), D_MODEL ** -0.5),
        'w_down': _normal(ks[36], (DEPTH, N_EXPERTS, MOE_FF, D_MODEL), BETA * MOE_FF ** -0.5),
        'ln2_g': 1.0 + _normal(ks[37], (DEPTH, D_MODEL), 0.02),
        'ln2_b': _normal(ks[38], (DEPTH, D_MODEL), 0.02),
    }


def reference(x_prompt, x_sample, cache_k, cache_v, state_ssm, c, c_ctx, w_mod, b_mod, w_in, q_norm, k_norm,
              hy_conv_w, hy_conv_b, hy_w1, hy_b1, hy_freq, hy_w2, hy_b2, hy_w3, hy_b3, hy_bias,
              ssm_conv_w, ssm_conv_b, ssm_dt_bias, ssm_a_log, ssm_d, ssm_norm,
              w_br_att, w_br_hy, w_br_ssm, w_out, ln1_g, ln1_b, w_router, w_gate, w_up, w_down, ln2_g, ln2_b):
    rope = axial_rope(x_sample.shape[1])
    zero_state = jnp.zeros((x_prompt.shape[0], 2, SSM_HEADS, SSM_HEADDIM, SSM_STATE), x_prompt.dtype)
    y_prompt, y_sample = x_prompt, x_sample
    new_k, new_v, new_s = [], [], []
    for l in range(DEPTH):
        p = dict(w_mod=w_mod[l], b_mod=b_mod[l], w_in=w_in[l], q_norm=q_norm[l], k_norm=k_norm[l],
                 hy_conv_w=hy_conv_w[l], hy_conv_b=hy_conv_b[l], hy_w1=hy_w1[l], hy_b1=hy_b1[l],
                 hy_freq=hy_freq[l], hy_w2=hy_w2[l], hy_b2=hy_b2[l], hy_w3=hy_w3[l], hy_b3=hy_b3[l],
                 hy_bias=hy_bias[l], ssm_conv_w=ssm_conv_w[l], ssm_conv_b=ssm_conv_b[l],
                 ssm_dt_bias=ssm_dt_bias[l], ssm_a_log=ssm_a_log[l], ssm_d=ssm_d[l], ssm_norm=ssm_norm[l],
                 w_br_att=w_br_att[l], w_br_hy=w_br_hy[l], w_br_ssm=w_br_ssm[l], w_out=w_out[l],
                 ln1_g=ln1_g[l], ln1_b=ln1_b[l], w_router=w_router[l], w_gate=w_gate[l], w_up=w_up[l],
                 w_down=w_down[l], ln2_g=ln2_g[l], ln2_b=ln2_b[l])
        y_prompt, (k_l, v_l, s_l) = trunk_layer(y_prompt, c_ctx[None], p, None, None, None, zero_state)
        new_k.append(k_l)
        new_v.append(v_l)
        new_s.append(s_l)
        y_sample, _ = trunk_layer(y_sample, c, p, rope, cache_k[:, l], cache_v[:, l], state_ssm[:, l])
    return (y_prompt, y_sample, jnp.stack(new_k, axis=1), jnp.stack(new_v, axis=1), jnp.stack(new_s, axis=1))
```

```python
import functools
import math

import jax
import jax.numpy as jnp
from jax import lax
from jax.experimental import pallas as pl
from jax.experimental.pallas import tpu as pltpu

F32 = jnp.float32
BF16 = jnp.bfloat16

D_MODEL = 4096
DEPTH = 2
GRID_W = 64
N_HEADS = 16
N_KV_HEADS = 4
HEAD_DIM = 128
ATT_W = N_HEADS * HEAD_DIM
KV_W = N_KV_HEADS * HEAD_DIM
ROT_FREQS = HEAD_DIM // 4
ROPE_THETA = 10000.0
Q_BLOCK = 128
HY_W = 2048
HY_ORDER = 2
HY_BANDS = 16
HY_DECAY_TARGET = 1e-2
HY_FAST_DECAY = 0.3
HY_SLOW_DECAY = 1.5
SSM_W = 2048
SSM_HEADDIM = 64
SSM_HEADS = SSM_W // SSM_HEADDIM
SSM_GROUPS = 8
SSM_STATE = 128
SSM_CHUNK = 128
SSM_CONV_DIM = SSM_W + 2 * SSM_GROUPS * SSM_STATE
N_EXPERTS = 16
EC_CAPACITY = 2
MOE_FF = 2048
N_BRANCH = 3
Q_OFF = 0
K_OFF = Q_OFF + ATT_W
V_OFF = K_OFF + KV_W
HY_OFF = V_OFF + KV_W
Z_OFF = HY_OFF + 3 * HY_W
XBC_OFF = Z_OFF + SSM_W
DT_OFF = XBC_OFF + SSM_CONV_DIM
GATE_OFF = DT_OFF + 2 * SSM_HEADS
N_IN = GATE_OFF + N_BRANCH * D_MODEL
ALPHA = (2 * DEPTH) ** 0.25
LN_EPS = 1e-5
RMS_EPS = 1e-6

VMEM_LIMIT_BYTES = 56 * 1024 * 1024


def _mm_kernel(x_ref, w_ref, o_ref, acc_ref):
    k = pl.program_id(3)

    @pl.when(k == 0)
    def _():
        acc_ref[...] = jnp.zeros_like(acc_ref)

    acc_ref[...] += jnp.dot(x_ref[0].astype(BF16), w_ref[0].astype(BF16), preferred_element_type=F32)

    @pl.when(k == pl.num_programs(3) - 1)
    def _():
        o_ref[0] = acc_ref[...].astype(o_ref.dtype)


def _pick(dim, pref):
    t = min(dim, pref)
    while dim % t:
        t //= 2
    return t


def gmm(x, w, out_dtype=F32, tm=1024, tn=1024, tk=1024):
    g, m, kd = x.shape
    _, _, n = w.shape
    tm, tn, tk = _pick(m, tm), _pick(n, tn), _pick(kd, tk)
    return pl.pallas_call(
        _mm_kernel,
        grid=(g, m // tm, n // tn, kd // tk),
        in_specs=[pl.BlockSpec((1, tm, tk), lambda e, i, j, k: (e, i, k)),
                  pl.BlockSpec((1, tk, tn), lambda e, i, j, k: (e, k, j))],
        out_specs=pl.BlockSpec((1, tm, tn), lambda e, i, j, k: (e, i, j)),
        out_shape=jax.ShapeDtypeStruct((g, m, n), out_dtype),
        scratch_shapes=[pltpu.VMEM((tm, tn), F32)],
        compiler_params=pltpu.CompilerParams(
            dimension_semantics=("parallel", "parallel", "parallel", "arbitrary"),
            vmem_limit_bytes=VMEM_LIMIT_BYTES),
        name="gmm",
    )(x, w)


def mm(x, w, out_dtype=F32, **kw):
    lead = x.shape[:-1]
    x2 = x.reshape(-1, x.shape[-1])
    m = x2.shape[0]
    pad = (-m) % 8
    if pad:
        x2 = jnp.pad(x2, ((0, pad), (0, 0)))
    y = gmm(x2[None], w[None], out_dtype, **kw)[0]
    if pad:
        y = y[:m]
    return y.reshape(*lead, w.shape[-1])


def layer_norm(x, g, b):
    mu = jnp.mean(x, axis=-1, keepdims=True)
    var = jnp.mean(jnp.square(x - mu), axis=-1, keepdims=True)
    return ((x - mu) * lax.rsqrt(var + LN_EPS)) * g + b


def rms_norm(x, g):
    return (x * lax.rsqrt(jnp.mean(jnp.square(x), axis=-1, keepdims=True) + RMS_EPS)) * g


def conv_centered(x, w, b):
    k = w.shape[0]
    pad = k // 2
    length = x.shape[1]
    xp = jnp.pad(x, ((0, 0), (pad, pad), (0, 0)))
    return sum(xp[:, i:i + length] * w[i] for i in range(k)) + b


def axial_rope(length):
    rows = length // GRID_W
    t = jnp.arange(rows * GRID_W)
    row = (t // GRID_W).astype(F32)
    col = (t % GRID_W).astype(F32)
    inv = ROPE_THETA ** (-jnp.arange(ROT_FREQS, dtype=F32) / ROT_FREQS)
    ang = jnp.stack([row[:, None] * inv, col[:, None] * inv], axis=1)
    return jnp.cos(ang), jnp.sin(ang)


def apply_rope(x, cos, sin):
    xr = x.reshape(*x.shape[:-1], 2, 2, ROT_FREQS)
    x1, x2 = xr[..., 0, :], xr[..., 1, :]
    c = cos[None, :, None]
    s = sin[None, :, None]
    return jnp.stack([x1 * c - x2 * s, x2 * c + x1 * s], axis=-2).reshape(x.shape)


def blocked_attention(q, k, v):
    bsz, lq = q.shape[0], q.shape[1]
    nb = lq // Q_BLOCK
    rep = N_HEADS // N_KV_HEADS
    qb = q.reshape(bsz, nb, Q_BLOCK, N_KV_HEADS, rep, HEAD_DIM).swapaxes(0, 1)
    scale = HEAD_DIM ** -0.5

    def block(qi):
        s = jnp.einsum('bqgrd,bkgd->bgrqk', qi, k).astype(F32) * scale
        p = jax.nn.softmax(s, axis=-1).astype(v.dtype)
        return jnp.einsum('bgrqk,bkgd->bqgrd', p, v)

    o = lax.map(block, qb)
    return o.swapaxes(0, 1).reshape(bsz, lq, ATT_W)


def hyena_filters(length, w1, b1, freq, w2, b2, w3, b3):
    t = jnp.arange(length, dtype=F32) / length
    bands = jnp.linspace(1e-4, HY_BANDS - 1, HY_BANDS, dtype=F32)
    ang = 2.0 * math.pi * t[:, None] * bands
    feats = jnp.concatenate([t[:, None], jnp.cos(ang), jnp.sin(ang)], axis=-1)
    hid = jnp.sin(freq * (feats @ w1 + b1))
    hid = jnp.sin(freq * (hid @ w2 + b2))
    h = (hid @ w3 + b3).reshape(length, HY_ORDER, 2, HY_W)
    deltas = jnp.abs(jnp.linspace(math.log(HY_DECAY_TARGET) / HY_SLOW_DECAY,
                                  math.log(HY_DECAY_TARGET) / HY_FAST_DECAY, HY_W, dtype=F32))
    h = h * jnp.exp(-t[:, None] * deltas)[:, None, None]
    g = jnp.concatenate([h[:, :, 0], jnp.zeros((1, HY_ORDER, HY_W), F32), h[:0:-1, :, 1]], axis=0)
    g = g / jnp.sum(jnp.abs(g), axis=0, keepdims=True)
    return jnp.fft.rfft(g, axis=0)


def long_conv(z, gf):
    length = z.shape[1]
    zf = jnp.fft.rfft(z, n=2 * length, axis=1)
    return jnp.fft.irfft(zf * gf, n=2 * length, axis=1)[:, :length]


def hyena_branch(u, p):
    length = u.shape[1]
    proj = conv_centered(u, p['hy_conv_w'], p['hy_conv_b'])
    v, x1, x2 = jnp.split(proj, 3, axis=-1)
    gf = hyena_filters(length, p['hy_w1'], p['hy_b1'], p['hy_freq'], p['hy_w2'], p['hy_b2'], p['hy_w3'], p['hy_b3'])
    z = v
    for n, gate in enumerate((x1, x2)):
        z = gate * (long_conv(z, gf[:, n]) + p['hy_bias'][n] * z)
    return z


def ssd_scan(x, dt, a, b, c, init):
    bsz, length, nh, hp = x.shape
    ng, ns = b.shape[2], b.shape[3]
    rep = nh // ng
    nc = length // SSM_CHUNK
    tt = SSM_CHUNK
    xdt = (x * dt[..., None]).reshape(bsz, nc, tt, ng, rep, hp)
    bc = b.reshape(bsz, nc, tt, ng, ns)
    cc = c.reshape(bsz, nc, tt, ng, ns)
    acs = jnp.cumsum((dt * a).reshape(bsz, nc, tt, ng, rep), axis=2)
    seg = acs[:, :, :, None] - acs[:, :, None, :]
    causal = jnp.tril(jnp.ones((tt, tt), bool))[None, None, :, :, None, None]
    decay = jnp.exp(jnp.where(causal, seg, -jnp.inf))
    cb = jnp.einsum('bclgn,bcsgn->bclsg', cc, bc)
    y_diag = jnp.einsum('bclsgr,bcsgrp->bclgrp', cb[..., None] * decay, xdt)
    to_end = jnp.exp(acs[:, :, -1:] - acs)
    chunk_states = jnp.einsum('bcsgn,bcsgrp->bcgrpn', bc, xdt * to_end[..., None])
    chunk_decay = jnp.exp(acs[:, :, -1])

    def carry(s, inp):
        st, dec = inp
        return s * dec[..., None, None] + st, s

    final, entering = lax.scan(carry, init.reshape(bsz, ng, rep, hp, ns),
                               (jnp.moveaxis(chunk_states, 1, 0), jnp.moveaxis(chunk_decay, 1, 0)))
    entering = jnp.moveaxis(entering, 0, 1)
    y_off = jnp.einsum('bclgn,bcgrpn->bclgrp', cc, entering) * jnp.exp(acs)[..., None]
    y = (y_diag + y_off).reshape(bsz, length, nh, hp)
    return y, final.reshape(bsz, nh, hp, ns)


def ssd_branch(zs, xbc, dt_raw, p, init):
    bsz, length = xbc.shape[0], xbc.shape[1]
    gn = SSM_GROUPS * SSM_STATE
    xbc = jax.nn.silu(conv_centered(xbc, p['ssm_conv_w'], p['ssm_conv_b']))
    xs = xbc[..., :SSM_W].reshape(bsz, length, SSM_HEADS, SSM_HEADDIM)
    bm = xbc[..., SSM_W:SSM_W + gn].reshape(bsz, length, SSM_GROUPS, SSM_STATE)
    cm = xbc[..., SSM_W + gn:].reshape(bsz, length, SSM_GROUPS, SSM_STATE)
    dt = jax.nn.softplus(dt_raw.reshape(bsz, length, 2, SSM_HEADS) + p['ssm_dt_bias'])
    a = -jnp.exp(p['ssm_a_log'])
    y_f, s_f = ssd_scan(xs, dt[:, :, 0], a[0], bm, cm, init[:, 0])
    y_b, s_b = ssd_scan(xs[:, ::-1], dt[:, ::-1, 1], a[1], bm[:, ::-1], cm[:, ::-1], init[:, 1])
    y = y_f + y_b[:, ::-1] + p['ssm_d'][:, None] * xs
    y = rms_norm(y.reshape(bsz, length, SSM_W) * jax.nn.silu(zs), p['ssm_norm'])
    return y, jnp.stack([s_f, s_b], axis=1)


def mixer(h, p, rope, ctx_k, ctx_v, init):
    hb = h.astype(BF16)
    u = mm(hb, p['w_in_main'])
    dt_raw = mm(hb, p['w_in_dt'], tn=128)[..., :2 * SSM_HEADS]
    gate_logits = mm(hb, p['w_in_gate'])
    bsz, length = u.shape[0], u.shape[1]
    q = rms_norm(u[..., Q_OFF:K_OFF].reshape(bsz, length, N_HEADS, HEAD_DIM), p['q_norm'])
    k = rms_norm(u[..., K_OFF:V_OFF].reshape(bsz, length, N_KV_HEADS, HEAD_DIM), p['k_norm'])
    v = u[..., V_OFF:HY_OFF].reshape(bsz, length, N_KV_HEADS, HEAD_DIM)
    if rope is None:
        k_all, v_all = k, v
    else:
        q = apply_rope(q, *rope)
        k_all = jnp.concatenate([ctx_k, apply_rope(k, *rope)], axis=1)
        v_all = jnp.concatenate([ctx_v, v], axis=1)
    att = blocked_attention(q, k_all, v_all)
    hy = hyena_branch(u[..., HY_OFF:Z_OFF], p)
    ssm, states = ssd_branch(u[..., Z_OFF:XBC_OFF], u[..., XBC_OFF:DT_OFF], dt_raw, p, init)
    gates = jax.nn.sigmoid(gate_logits.reshape(bsz, length, N_BRANCH, D_MODEL))
    merged = (gates[:, :, 0] * mm(att.astype(BF16), p['w_br_att'])
              + gates[:, :, 1] * mm(hy.astype(BF16), p['w_br_hy'])
              + gates[:, :, 2] * mm(ssm.astype(BF16), p['w_br_ssm']))
    return mm(merged.astype(BF16), p['w_out']), (k, v, states)


def ec_route(h, w_router):
    n = h.shape[1]
    cap = EC_CAPACITY * n // N_EXPERTS
    aff = jax.nn.softmax(jnp.einsum('snd,de->sne', h, w_router), axis=-1)
    return lax.top_k(jnp.swapaxes(aff, 1, 2), cap)


def ec_moe(h, p):
    s, n, _ = h.shape
    g, idx = ec_route(h, p['w_router'])
    cap = idx.shape[-1]
    xs = jax.vmap(lambda hh, ii: hh[ii])(h, idx)
    xs = jnp.swapaxes(xs, 0, 1).reshape(N_EXPERTS, s * cap, D_MODEL).astype(BF16)
    hid = jax.nn.silu(gmm(xs, p['w_gate'])) * gmm(xs, p['w_up'])
    y = gmm(hid.astype(BF16), p['w_down'])
    y = jnp.swapaxes(y.reshape(N_EXPERTS, s, cap, D_MODEL), 0, 1) * g[..., None]
    return jax.vmap(lambda yy, ii: jnp.zeros((n, D_MODEL), F32).at[ii.reshape(-1)].add(yy.reshape(-1, D_MODEL)))(y, idx)


def modulation(cond, p):
    m = mm(jax.nn.silu(cond).astype(BF16), p['w_mod'], tm=8, tn=2048, tk=2048) + p['b_mod']
    return jnp.split(m[:, None, :], 6, axis=-1)


def trunk_layer(x, cond, p, rope, ctx_k, ctx_v, init):
    sh1, sc1, g1, sh2, sc2, g2 = modulation(cond, p)
    m, ctx = mixer(x * (1 + sc1) + sh1, p, rope, ctx_k, ctx_v, init)
    x = layer_norm(ALPHA * x + g1 * m, p['ln1_g'], p['ln1_b'])
    f = ec_moe(x * (1 + sc2) + sh2, p)
    x = layer_norm(ALPHA * x + g2 * f, p['ln2_g'], p['ln2_b'])
    return x, ctx


def kernel(x_prompt, x_sample, cache_k, cache_v, state_ssm, c, c_ctx, w_mod, b_mod, w_in, q_norm, k_norm, hy_conv_w, hy_conv_b, hy_w1, hy_b1, hy_freq, hy_w2, hy_b2, hy_w3, hy_b3, hy_bias, ssm_conv_w, ssm_conv_b, ssm_dt_bias, ssm_a_log, ssm_d, ssm_norm, w_br_att, w_br_hy, w_br_ssm, w_out, ln1_g, ln1_b, w_router, w_gate, w_up, w_down, ln2_g, ln2_b):
    rope = axial_rope(x_sample.shape[1])
    zero_state = jnp.zeros((x_prompt.shape[0], 2, SSM_HEADS, SSM_HEADDIM, SSM_STATE), x_prompt.dtype)
    y_prompt, y_sample = x_prompt, x_sample
    new_k, new_v, new_s = [], [], []
    for l in range(DEPTH):
        p = dict(w_mod=w_mod[l].astype(BF16), b_mod=b_mod[l],
                 w_in_main=w_in[l, :, :DT_OFF].astype(BF16),
                 w_in_dt=w_in[l, :, DT_OFF:DT_OFF + 128].astype(BF16),
                 w_in_gate=w_in[l, :, GATE_OFF:].astype(BF16),
                 q_norm=q_norm[l], k_norm=k_norm[l],
                 hy_conv_w=hy_conv_w[l], hy_conv_b=hy_conv_b[l], hy_w1=hy_w1[l], hy_b1=hy_b1[l],
                 hy_freq=hy_freq[l], hy_w2=hy_w2[l], hy_b2=hy_b2[l], hy_w3=hy_w3[l], hy_b3=hy_b3[l],
                 hy_bias=hy_bias[l], ssm_conv_w=ssm_conv_w[l], ssm_conv_b=ssm_conv_b[l],
                 ssm_dt_bias=ssm_dt_bias[l], ssm_a_log=ssm_a_log[l], ssm_d=ssm_d[l], ssm_norm=ssm_norm[l],
                 w_br_att=w_br_att[l].astype(BF16), w_br_hy=w_br_hy[l].astype(BF16),
                 w_br_ssm=w_br_ssm[l].astype(BF16), w_out=w_out[l].astype(BF16),
                 ln1_g=ln1_g[l], ln1_b=ln1_b[l], w_router=w_router[l],
                 w_gate=w_gate[l].astype(BF16), w_up=w_up[l].astype(BF16), w_down=w_down[l].astype(BF16),
                 ln2_g=ln2_g[l], ln2_b=ln2_b[l])
        y_prompt, (k_l, v_l, s_l) = trunk_layer(y_prompt, c_ctx[None], p, None, None, None, zero_state)
        new_k.append(k_l)
        new_v.append(v_l)
        new_s.append(s_l)
        y_sample, _ = trunk_layer(y_sample, c, p, rope, cache_k[:, l], cache_v[:, l], state_ssm[:, l])
    return (y_prompt, y_sample, jnp.stack(new_k, axis=1), jnp.stack(new_v, axis=1), jnp.stack(new_s, axis=1))
```

```python
import functools
import math

import jax
import jax.numpy as jnp
from jax import lax
from jax.experimental import pallas as pl
from jax.experimental.pallas import tpu as pltpu

F32 = jnp.float32
BF16 = jnp.bfloat16

D_MODEL = 4096
DEPTH = 2
GRID_W = 64
N_HEADS = 16
N_KV_HEADS = 4
KV_REP = N_HEADS // N_KV_HEADS
HEAD_DIM = 128
ATT_W = N_HEADS * HEAD_DIM
KV_W = N_KV_HEADS * HEAD_DIM
ROT_FREQS = HEAD_DIM // 4
ROPE_THETA = 10000.0
HY_W = 2048
HY_ORDER = 2
HY_BANDS = 16
HY_DECAY_TARGET = 1e-2
HY_FAST_DECAY = 0.3
HY_SLOW_DECAY = 1.5
SSM_W = 2048
SSM_HEADDIM = 64
SSM_HEADS = SSM_W // SSM_HEADDIM
SSM_GROUPS = 8
SSM_REP = SSM_HEADS // SSM_GROUPS
SSM_STATE = 128
SSM_CHUNK = 128
SSM_CONV_DIM = SSM_W + 2 * SSM_GROUPS * SSM_STATE
N_EXPERTS = 16
EC_CAPACITY = 2
MOE_FF = 2048
N_BRANCH = 3
Q_OFF = 0
K_OFF = Q_OFF + ATT_W
V_OFF = K_OFF + KV_W
HY_OFF = V_OFF + KV_W
Z_OFF = HY_OFF + 3 * HY_W
XBC_OFF = Z_OFF + SSM_W
DT_OFF = XBC_OFF + SSM_CONV_DIM
GATE_OFF = DT_OFF + 2 * SSM_HEADS
ALPHA = (2 * DEPTH) ** 0.25
LN_EPS = 1e-5
RMS_EPS = 1e-6
N_COND_PAD = 8
LANES = 128
SUBLANES = 8

VMEM_LIMIT_BYTES = 56 * 1024 * 1024

NT_DIMS = (((1,), (1,)), ((), ()))
TN_DIMS = (((0,), (0,)), ((), ()))


def _cparams(*sem):
    return pltpu.CompilerParams(dimension_semantics=sem, vmem_limit_bytes=VMEM_LIMIT_BYTES)


def _pick(dim, pref):
    t = min(dim, pref)
    while dim % t:
        t //= 2
    return t


def _mm_kernel(x_ref, w_ref, o_ref, acc_ref):
    k = pl.program_id(3)

    @pl.when(k == 0)
    def _():
        acc_ref[...] = jnp.zeros_like(acc_ref)

    acc_ref[...] += jnp.dot(x_ref[0].astype(BF16), w_ref[0].astype(BF16), preferred_element_type=F32)

    @pl.when(k == pl.num_programs(3) - 1)
    def _():
        o_ref[0] = acc_ref[...].astype(o_ref.dtype)


def gmm(x, w, out_dtype=F32, tm=1024, tn=1024, tk=1024, share_x=False):
    g, kd, n = w.shape
    m = x.shape[1]
    tm, tn, tk = _pick(m, tm), _pick(n, tn), _pick(kd, tk)
    xmap = (lambda e, i, j, k: (0, i, k)) if share_x else (lambda e, i, j, k: (e, i, k))
    return pl.pallas_call(
        _mm_kernel,
        grid=(g, m // tm, n // tn, kd // tk),
        in_specs=[pl.BlockSpec((1, tm, tk), xmap),
                  pl.BlockSpec((1, tk, tn), lambda e, i, j, k: (e, k, j))],
        out_specs=pl.BlockSpec((1, tm, tn), lambda e, i, j, k: (e, i, j)),
        out_shape=jax.ShapeDtypeStruct((g, m, n), out_dtype),
        scratch_shapes=[pltpu.VMEM((tm, tn), F32)],
        compiler_params=_cparams("parallel", "parallel", "parallel", "arbitrary"),
        name="gmm",
    )(x, w)


def mm(x, w, out_dtype=F32, **kw):
    return gmm(x[None], w[None], out_dtype, **kw)[0]


class Layout:
    def __init__(self, n_p, l_p, n_s, l_s):
        self.n_p, self.l_p, self.n_s, self.l_s = n_p, l_p, n_s, l_s
        self.tp = n_p * l_p
        self.ts = n_s * l_s
        self.t = self.tp + self.ts

    def group_of_block(self, i, rows):
        bp = self.tp // rows
        return jnp.where(i < bp, 0, 1 + (i - bp) // (self.l_s // rows))

    def seq_edges(self, i, rows):
        bp = self.tp // rows
        per_p, per_s = self.l_p // rows, self.l_s // rows
        first = jnp.where(i < bp, i % per_p == 0, (i - bp) % per_s == 0)
        last = jnp.where(i < bp, i % per_p == per_p - 1, (i - bp) % per_s == per_s - 1)
        return first, last


ROWS = 256


def _modulate_kernel(x_ref, sc_ref, sh_ref, o_ref):
    o_ref[...] = (x_ref[...] * (1.0 + sc_ref[0]) + sh_ref[0]).astype(o_ref.dtype)


def modulate(x, mod, sec_sc, sec_sh, lay):
    t, d = x.shape
    grp = lambda i: lay.group_of_block(i, ROWS)
    return pl.pallas_call(
        _modulate_kernel,
        grid=(t // ROWS,),
        in_specs=[pl.BlockSpec((ROWS, d), lambda i: (i, 0)),
                  pl.BlockSpec((1, 1, d), lambda i: (grp(i), 0, sec_sc)),
                  pl.BlockSpec((1, 1, d), lambda i: (grp(i), 0, sec_sh))],
        out_specs=pl.BlockSpec((ROWS, d), lambda i: (i, 0)),
        out_shape=jax.ShapeDtypeStruct((t, d), BF16),
        compiler_params=_cparams("parallel"),
        name="modulate",
    )(x, mod, mod)


def _ln_mod_kernel(*refs, with_mod):
    if with_mod:
        x_ref, m_ref, gt_ref, lg_ref, lb_ref, sc_ref, sh_ref, o_ref, om_ref = refs
    else:
        x_ref, m_ref, gt_ref, lg_ref, lb_ref, o_ref = refs
    r = ALPHA * x_ref[...] + gt_ref[0] * m_ref[...]
    mu = jnp.mean(r, axis=-1, keepdims=True)
    dlt = r - mu
    var = jnp.mean(dlt * dlt, axis=-1, keepdims=True)
    y = dlt * lax.rsqrt(var + LN_EPS) * lg_ref[...] + lb_ref[...]
    o_ref[...] = y
    if with_mod:
        om_ref[...] = (y * (1.0 + sc_ref[0]) + sh_ref[0]).astype(om_ref.dtype)


def ln_mod(x, m, mod, sec_gate, ln_g, ln_b, lay, mod_next=None, sec_sc=0, sec_sh=0):
    t, d = x.shape
    rows = ROWS // 2
    grp = lambda i: lay.group_of_block(i, rows)
    with_mod = mod_next is not None
    row_spec = pl.BlockSpec((rows, d), lambda i: (i, 0))
    vec_spec = pl.BlockSpec((1, d), lambda i: (0, 0))
    in_specs = [row_spec, row_spec,
                pl.BlockSpec((1, 1, d), lambda i: (grp(i), 0, sec_gate)), vec_spec, vec_spec]
    args = [x, m, mod, ln_g.reshape(1, d), ln_b.reshape(1, d)]
    out_specs = [row_spec]
    out_shape = [jax.ShapeDtypeStruct((t, d), F32)]
    if with_mod:
        in_specs += [pl.BlockSpec((1, 1, d), lambda i: (grp(i), 0, sec_sc)),
                     pl.BlockSpec((1, 1, d), lambda i: (grp(i), 0, sec_sh))]
        args += [mod_next, mod_next]
        out_specs.append(row_spec)
        out_shape.append(jax.ShapeDtypeStruct((t, d), BF16))
    res = pl.pallas_call(
        functools.partial(_ln_mod_kernel, with_mod=with_mod),
        grid=(t // rows,),
        in_specs=in_specs, out_specs=out_specs, out_shape=out_shape,
        compiler_params=_cparams("parallel"),
        name="ln_mod",
    )(*args)
    return (res[0], res[1]) if with_mod else (res[0], None)


def _qkv_prep_kernel(u_ref, cos_ref, sin_ref, qn_ref, kn_ref, q_ref, kr_ref, vb_ref, kf_ref, vf_ref):
    cos = cos_ref[...]
    sin = sin_ref[...]
    lane = lax.broadcasted_iota(jnp.int32, cos.shape, 1)
    lane_lo = (lane % (2 * ROT_FREQS)) < ROT_FREQS

    def norm(x, g):
        return x * lax.rsqrt(jnp.mean(x * x, axis=-1, keepdims=True) + RMS_EPS) * g

    def rope(x):
        sw = jnp.where(lane_lo, pltpu.roll(x, LANES - ROT_FREQS, 1), pltpu.roll(x, ROT_FREQS, 1))
        return x * cos + sw * sin

    for h in range(N_HEADS):
        sl = slice(Q_OFF + h * HEAD_DIM, Q_OFF + (h + 1) * HEAD_DIM)
        q_ref[:, h * HEAD_DIM:(h + 1) * HEAD_DIM] = rope(norm(u_ref[:, sl], qn_ref[...])).astype(q_ref.dtype)
    for h in range(N_KV_HEADS):
        o = slice(h * HEAD_DIM, (h + 1) * HEAD_DIM)
        kk = norm(u_ref[:, K_OFF + h * HEAD_DIM:K_OFF + (h + 1) * HEAD_DIM], kn_ref[...])
        kf_ref[:, o] = kk
        kr_ref[:, o] = rope(kk).astype(kr_ref.dtype)
        vv = u_ref[:, V_OFF + h * HEAD_DIM:V_OFF + (h + 1) * HEAD_DIM]
        vf_ref[:, o] = vv
        vb_ref[:, o] = vv.astype(vb_ref.dtype)


def qkv_prep(u, cos_t, sin_t, q_norm, k_norm):
    t = u.shape[0]
    row = lambda w: pl.BlockSpec((ROWS, w), lambda i: (i, 0))
    vec = pl.BlockSpec((1, HEAD_DIM), lambda i: (0, 0))
    return pl.pallas_call(
        _qkv_prep_kernel,
        grid=(t // ROWS,),
        in_specs=[row(HY_OFF), row(HEAD_DIM), row(HEAD_DIM), vec, vec],
        out_specs=[row(ATT_W), row(KV_W), row(KV_W), row(KV_W), row(KV_W)],
        out_shape=[jax.ShapeDtypeStruct((t, ATT_W), BF16), jax.ShapeDtypeStruct((t, KV_W), BF16),
                   jax.ShapeDtypeStruct((t, KV_W), BF16), jax.ShapeDtypeStruct((t, KV_W), F32),
                   jax.ShapeDtypeStruct((t, KV_W), F32)],
        compiler_params=_cparams("parallel"),
        name="qkv_prep",
    )(u, cos_t, sin_t, q_norm.reshape(1, HEAD_DIM), k_norm.reshape(1, HEAD_DIM))


def _attn_kernel(*refs, aliased):
    q_ref, k_ref, v_ref = refs[:3]
    o_ref = refs[-1]
    scale = HEAD_DIM ** -0.5
    k = k_ref[0]
    v = v_ref[0]
    for r in range(KV_REP):
        sl = slice(r * HEAD_DIM, (r + 1) * HEAD_DIM)
        s = lax.dot_general(q_ref[:, sl], k, NT_DIMS, preferred_element_type=F32)
        m = jnp.max(s, axis=1, keepdims=True)
        p = jnp.exp((s - m) * scale)
        l = jnp.sum(p, axis=1, keepdims=True)
        o = jnp.dot(p.astype(BF16), v, preferred_element_type=F32)
        o_ref[:, sl] = (o / l).astype(o_ref.dtype)


def attention(q, k, v, row0, n_seq, l_q, tq, prev=None):
    t = q.shape[0]
    l_k = k.shape[1]
    gw = KV_REP * HEAD_DIM
    nq = l_q // tq
    b0 = row0 // tq
    qmap = lambda b, g, i: (b0 + b * nq + i, g)
    in_specs = [pl.BlockSpec((tq, gw), qmap),
                pl.BlockSpec((1, l_k, HEAD_DIM), lambda b, g, i: (b, 0, g)),
                pl.BlockSpec((1, l_k, HEAD_DIM), lambda b, g, i: (b, 0, g))]
    args = [q, k, v]
    aliases = {}
    if prev is not None:
        in_specs.append(pl.BlockSpec(memory_space=pl.ANY))
        args.append(prev)
        aliases = {3: 0}
    return pl.pallas_call(
        functools.partial(_attn_kernel, aliased=prev is not None),
        grid=(n_seq, N_KV_HEADS, nq),
        in_specs=in_specs,
        out_specs=pl.BlockSpec((tq, gw), qmap),
        out_shape=jax.ShapeDtypeStruct((t, ATT_W), BF16),
        input_output_aliases=aliases,
        compiler_params=_cparams("parallel", "parallel", "parallel"),
        name="attention",
    )(*args)


def _conv3_kernel(x_ref, prev_ref, next_ref, w_ref, b_ref, o_ref, *, lay, silu):
    i = pl.program_id(0)
    first, last = lay.seq_edges(i, ROWS)
    x = x_ref[...]
    rows = lax.broadcasted_iota(jnp.int32, x.shape, 0)
    before = jnp.where(first, 0.0, 1.0) * prev_ref[SUBLANES - 1:SUBLANES, :]
    after = jnp.where(last, 0.0, 1.0) * next_ref[0:1, :]
    xm1 = jnp.where(rows == 0, before, pltpu.roll(x, 1, 0))
    xp1 = jnp.where(rows == ROWS - 1, after, pltpu.roll(x, ROWS - 1, 0))
    y = xm1 * w_ref[0:1, :] + x * w_ref[1:2, :] + xp1 * w_ref[2:3, :] + b_ref[...]
    if silu:
        y = y * jax.nn.sigmoid(y)
    o_ref[...] = y


def conv3(u, col0, width, w, b, lay, silu, tc=512):
    t = u.shape[0]
    c0 = col0 // tc
    sub = ROWS // SUBLANES
    n_sub = t // SUBLANES
    return pl.pallas_call(
        functools.partial(_conv3_kernel, lay=lay, silu=silu),
        grid=(t // ROWS, width // tc),
        in_specs=[pl.BlockSpec((ROWS, tc), lambda i, j: (i, c0 + j)),
                  pl.BlockSpec((SUBLANES, tc), lambda i, j: (jnp.maximum(i * sub - 1, 0), c0 + j)),
                  pl.BlockSpec((SUBLANES, tc), lambda i, j: (jnp.minimum((i + 1) * sub, n_sub - 1), c0 + j)),
                  pl.BlockSpec((3, tc), lambda i, j: (0, j)),
                  pl.BlockSpec((1, tc), lambda i, j: (0, j))],
        out_specs=pl.BlockSpec((ROWS, tc), lambda i, j: (i, j)),
        out_shape=jax.ShapeDtypeStruct((t, width), F32),
        compiler_params=_cparams("parallel", "parallel"),
        name="conv3",
    )(u, u, u, w, b.reshape(1, width))


def _dt_prep_kernel(raw_ref, bias_ref, a_ref, o_ref):
    x = raw_ref[...] + bias_ref[...]
    dt = jnp.maximum(x, 0.0) + jnp.log1p(jnp.exp(-jnp.abs(x)))
    o_ref[...] = jnp.where(lax.broadcasted_iota(jnp.int32, x.shape, 1) < 2 * SSM_HEADS,
                           dt, pltpu.roll(dt, 2 * SSM_HEADS, 1) * a_ref[...])


def dt_prep(raw, dt_bias, a_log):
    t = raw.shape[0]
    nh2 = 2 * SSM_HEADS
    bias = jnp.zeros((1, LANES), F32).at[0, :nh2].set(dt_bias.reshape(-1))
    a = jnp.zeros((1, LANES), F32).at[0, nh2:2 * nh2].set(-jnp.exp(a_log.reshape(-1)))
    return pl.pallas_call(
        _dt_prep_kernel,
        grid=(t // ROWS,),
        in_specs=[pl.BlockSpec((ROWS, LANES), lambda i: (i, 0)),
                  pl.BlockSpec((1, LANES), lambda i: (0, 0)),
                  pl.BlockSpec((1, LANES), lambda i: (0, 0))],
        out_specs=pl.BlockSpec((ROWS, LANES), lambda i: (i, 0)),
        out_shape=jax.ShapeDtypeStruct((t, LANES), F32),
        compiler_params=_cparams("parallel"),
        name="dt_prep",
    )(raw, bias, a)


def _prefix_sum(x, axis):
    idx = lax.broadcasted_iota(jnp.int32, x.shape, axis)
    d = 1
    while d < SSM_CHUNK:
        x = x + jnp.where(idx >= d, pltpu.roll(x, d, axis), 0.0)
        d *= 2
    return x


def _ssd_kernel(*refs, n_chunks, aliased):
    xs_ref, b_ref, c_ref, dtc_ref, dtr_ref, init_ref = refs[:6]
    y_ref, fin_ref, st_ref = refs[-3:]
    d = pl.program_id(1)
    c = pl.program_id(2)
    tt = SSM_CHUNK
    nh = SSM_HEADS

    @pl.when(c == 0)
    def _():
        st_ref[...] = init_ref[0, 0]

    fwd = d == 0
    dtc = dtc_ref[...]
    dtr = dtr_ref[...]
    dtv_c = jnp.where(fwd, dtc[:, 0:nh], dtc[:, nh:2 * nh])
    da_c = jnp.where(fwd, dtc[:, 2 * nh:3 * nh], dtc[:, 3 * nh:4 * nh])
    da_r = jnp.where(fwd, dtr[2 * nh:3 * nh, :], dtr[3 * nh:4 * nh, :])
    pc = _prefix_sum(da_c, 0)
    pr = _prefix_sum(da_r, 1)
    tot_c = pc[tt - 1:tt, :]
    tot_r = pr[:, tt - 1:tt]
    acs_c = jnp.where(fwd, pc, tot_c - pc + da_c)
    acs_r = jnp.where(fwd, pr, tot_r - pr + da_r)
    li = lax.broadcasted_iota(jnp.int32, (tt, tt), 0)
    si = lax.broadcasted_iota(jnp.int32, (tt, tt), 1)
    mask = jnp.where(fwd, li - si, si - li) >= 0
    eacs_c = jnp.exp(acs_c)
    toend_c = jnp.exp(tot_c - acs_c)
    cdec_r = jnp.exp(tot_r)
    ns = SSM_STATE
    hp = SSM_HEADDIM
    for g in range(SSM_GROUPS):
        bg = b_ref[:, g * ns:(g + 1) * ns].astype(BF16)
        cg = c_ref[:, g * ns:(g + 1) * ns].astype(BF16)
        cb = lax.dot_general(cg, bg, NT_DIMS, preferred_element_type=F32)
        for r in range(SSM_REP):
            h = g * SSM_REP + r
            seg = acs_c[:, h:h + 1] - acs_r[h:h + 1, :]
            dec = jnp.exp(jnp.where(mask, seg, -jnp.inf))
            xdt = xs_ref[:, h * hp:(h + 1) * hp] * dtv_c[:, h:h + 1]
            y_diag = jnp.dot((cb * dec).astype(BF16), xdt.astype(BF16), preferred_element_type=F32)
            state = st_ref[h]
            y_off = lax.dot_general(cg, state.astype(BF16), NT_DIMS, preferred_element_type=F32)
            y_ref[0, :, h * hp:(h + 1) * hp] = y_diag + y_off * eacs_c[:, h:h + 1]
            xw = (xdt * toend_c[:, h:h + 1]).astype(BF16)
            chunk_state = lax.dot_general(xw, bg, TN_DIMS, preferred_element_type=F32)
            st_ref[h] = state * cdec_r[h:h + 1, :] + chunk_state

    @pl.when(c == n_chunks - 1)
    def _():
        fin_ref[0, 0] = st_ref[...]


def ssd(xbc, dtc, dtr, init, row0, n_seq, length, prev=None):
    t = xbc.shape[0]
    tt = SSM_CHUNK
    nc = length // tt
    b0 = row0 // tt
    gn = SSM_GROUPS * SSM_STATE

    def blk(b, d, c):
        return b0 + b * nc + jnp.where(d == 0, c, nc - 1 - c)

    st_shape = (SSM_HEADS, SSM_HEADDIM, SSM_STATE)
    st_spec = pl.BlockSpec((1, 1) + st_shape, lambda b, d, c: (b, d, 0, 0, 0))
    in_specs = [pl.BlockSpec((tt, SSM_W), lambda b, d, c: (blk(b, d, c), 0)),
                pl.BlockSpec((tt, gn), lambda b, d, c: (blk(b, d, c), SSM_W // gn)),
                pl.BlockSpec((tt, gn), lambda b, d, c: (blk(b, d, c), SSM_W // gn + 1)),
                pl.BlockSpec((tt, LANES), lambda b, d, c: (blk(b, d, c), 0)),
                pl.BlockSpec((LANES, tt), lambda b, d, c: (0, blk(b, d, c))),
                st_spec]
    args = [xbc, xbc, xbc, dtc, dtr, init]
    aliases = {}
    if prev is not None:
        in_specs.append(pl.BlockSpec(memory_space=pl.ANY))
        args.append(prev)
        aliases = {6: 0}
    return pl.pallas_call(
        functools.partial(_ssd_kernel, n_chunks=nc, aliased=prev is not None),
        grid=(n_seq, 2, nc),
        in_specs=in_specs,
        out_specs=[pl.BlockSpec((1, tt, SSM_W), lambda b, d, c: (d, blk(b, d, c), 0)), st_spec],
        out_shape=[jax.ShapeDtypeStruct((2, t, SSM_W), F32),
                   jax.ShapeDtypeStruct((n_seq, 2) + st_shape, F32)],
        scratch_shapes=[pltpu.VMEM(st_shape, F32)],
        input_output_aliases=aliases,
        compiler_params=_cparams("parallel", "parallel", "arbitrary"),
        name="ssd",
    )(*args)


def _ssd_gate_kernel(y_ref, xs_ref, zlo_ref, zhi_ref, d_ref, g_ref, o_ref):
    z = jnp.concatenate([zlo_ref[...], zhi_ref[...]], axis=1)
    y = (y_ref[0] + y_ref[1] + d_ref[...] * xs_ref[...]) * (z * jax.nn.sigmoid(z))
    y = y * lax.rsqrt(jnp.mean(y * y, axis=-1, keepdims=True) + RMS_EPS) * g_ref[...]
    o_ref[...] = y.astype(o_ref.dtype)


def ssd_gate(y2, xbc, u, ssm_d, ssm_norm):
    t = xbc.shape[0]
    half = SSM_W // 2
    zb = Z_OFF // half
    vec = pl.BlockSpec((1, SSM_W), lambda i: (0, 0))
    return pl.pallas_call(
        _ssd_gate_kernel,
        grid=(t // ROWS,),
        in_specs=[pl.BlockSpec((2, ROWS, SSM_W), lambda i: (0, i, 0)),
                  pl.BlockSpec((ROWS, SSM_W), lambda i: (i, 0)),
                  pl.BlockSpec((ROWS, half), lambda i: (i, zb)),
                  pl.BlockSpec((ROWS, half), lambda i: (i, zb + 1)),
                  vec, vec],
        out_specs=pl.BlockSpec((ROWS, SSM_W), lambda i: (i, 0)),
        out_shape=jax.ShapeDtypeStruct((t, SSM_W), BF16),
        compiler_params=_cparams("parallel"),
        name="ssd_gate",
    )(y2, xbc, u, u, jnp.repeat(ssm_d, SSM_HEADDIM).reshape(1, SSM_W), ssm_norm.reshape(1, SSM_W))


def dft_matrices(length):
    k = jnp.arange(length, dtype=jnp.int32)[:, None]
    s = jnp.arange(length, dtype=jnp.int32)[None, :]
    ang = ((k * s) % (2 * length)).astype(F32) * (math.pi / length)
    alt = jnp.where(s % 2 == 0, 1.0, -1.0).astype(F32)
    sin = jnp.where(k == 0, alt, jnp.sin(ang))
    fwd = jnp.concatenate([jnp.cos(ang), sin], axis=0).astype(BF16)
    return fwd, fwd.T


def _hyfilt_kernel(hid_ref, wf_ref, wb_ref, bf_ref, bb_ref, dl_ref, o_ref, nrm_ref, nyq_ref, *, length, tr):
    i = pl.program_id(2)
    hid = hid_ref[...].astype(BF16)
    t_idx = lax.broadcasted_iota(jnp.int32, (tr, 1), 0) + i * tr
    win = jnp.exp(-(t_idx.astype(F32) / length) * dl_ref[...])
    hf = (jnp.dot(hid, wf_ref[...].astype(BF16), preferred_element_type=F32) + bf_ref[...]) * win
    hb = (jnp.dot(hid, wb_ref[...].astype(BF16), preferred_element_type=F32) + bb_ref[...]) * win
    hb = jnp.where(t_idx == 0, 0.0, hb)
    o_ref[0] = (hf + hb).astype(o_ref.dtype)
    o_ref[1] = (hf - hb).astype(o_ref.dtype)
    sign = jnp.where(t_idx % 2 == 0, 1.0, -1.0)

    @pl.when(i == 0)
    def _():
        nrm_ref[...] = jnp.zeros_like(nrm_ref)
        nyq_ref[...] = jnp.zeros_like(nyq_ref)

    nrm_ref[...] += jnp.sum(jnp.abs(hf) + jnp.abs(hb), axis=0, keepdims=True)
    nyq_ref[...] += jnp.sum(sign * (hf + hb), axis=0, keepdims=True)


def hyena_filters(length, p, fwd):
    t = jnp.arange(length, dtype=F32) / length
    bands = jnp.linspace(1e-4, HY_BANDS - 1, HY_BANDS, dtype=F32)
    ang = 2.0 * math.pi * t[:, None] * bands
    feats = jnp.concatenate([t[:, None], jnp.cos(ang), jnp.sin(ang)], axis=-1)
    hid = jnp.sin(p['hy_freq'] * (feats @ p['hy_w1'] + p['hy_b1']))
    hid = jnp.sin(p['hy_freq'] * (hid @ p['hy_w2'] + p['hy_b2']))
    deltas = jnp.abs(jnp.linspace(math.log(HY_DECAY_TARGET) / HY_SLOW_DECAY,
                                  math.log(HY_DECAY_TARGET) / HY_FAST_DECAY, HY_W, dtype=F32)).reshape(1, HY_W)
    ffn = hid.shape[1]
    tr, tc = _pick(length, 256), 512
    nj = HY_W // tc
    w3, b3 = p['hy_w3'], p['hy_b3'].reshape(1, -1)
    ow = HY_ORDER * HY_W
    hsd, nrm, nyq = pl.pallas_call(
        functools.partial(_hyfilt_kernel, length=length, tr=tr),
        grid=(HY_ORDER, nj, length // tr),
        in_specs=[pl.BlockSpec((tr, ffn), lambda n, j, i: (i, 0)),
                  pl.BlockSpec((ffn, tc), lambda n, j, i: (0, (2 * n) * nj + j)),
                  pl.BlockSpec((ffn, tc), lambda n, j, i: (0, (2 * n + 1) * nj + j)),
                  pl.BlockSpec((1, tc), lambda n, j, i: (0, (2 * n) * nj + j)),
                  pl.BlockSpec((1, tc), lambda n, j, i: (0, (2 * n + 1) * nj + j)),
                  pl.BlockSpec((1, tc), lambda n, j, i: (0, j))],
        out_specs=[pl.BlockSpec((2, tr, tc), lambda n, j, i: (0, i, n * nj + j)),
                   pl.BlockSpec((1, tc), lambda n, j, i: (0, n * nj + j)),
                   pl.BlockSpec((1, tc), lambda n, j, i: (0, n * nj + j))],
        out_shape=[jax.ShapeDtypeStruct((2, length, ow), BF16),
                   jax.ShapeDtypeStruct((1, ow), F32), jax.ShapeDtypeStruct((1, ow), F32)],
        compiler_params=_cparams("parallel", "parallel", "arbitrary"),
        name="hyena_filter",
    )(hid, w3, w3, b3, b3, deltas)
    pq = gmm(fwd.reshape(2, length, length), hsd)
    return pq, nrm, nyq


def _dft_fwd_kernel(f_ref, z_ref, o_ref, acc_ref):
    k = pl.program_id(3)

    @pl.when(k == 0)
    def _():
        acc_ref[...] = jnp.zeros_like(acc_ref)

    acc_ref[...] += jnp.dot(f_ref[...], z_ref[...].astype(BF16), preferred_element_type=F32)

    @pl.when(k == pl.num_programs(3) - 1)
    def _():
        o_ref[0] = acc_ref[...]


def dft_fwd(fwd, z, zcol0, row0, n_seq, length):
    tm, tn, tk = _pick(2 * length, 1024), 1024, _pick(length, 1024)
    r0, c0 = row0 // tk, zcol0 // tn
    nk = length // tk
    return pl.pallas_call(
        _dft_fwd_kernel,
        grid=(n_seq, 2 * length // tm, HY_W // tn, nk),
        in_specs=[pl.BlockSpec((tm, tk), lambda b, i, j, k: (i, k)),
                  pl.BlockSpec((tk, tn), lambda b, i, j, k: (r0 + b * nk + k, c0 + j))],
        out_specs=pl.BlockSpec((1, tm, tn), lambda b, i, j, k: (b, i, j)),
        out_shape=jax.ShapeDtypeStruct((n_seq, 2 * length, HY_W), F32),
        scratch_shapes=[pltpu.VMEM((tm, tn), F32)],
        compiler_params=_cparams("parallel", "parallel", "parallel", "arbitrary"),
        name="hyena_dft_fwd",
    )(fwd, z)


def _hy_point_kernel(ab_ref, pq_ref, nrm_ref, nyq_ref, uv_ref, *, length, tr):
    i = pl.program_id(1)
    k_idx = lax.broadcasted_iota(jnp.int32, (tr, 1), 0) + i * tr
    is0 = k_idx == 0
    wk = jnp.where(is0, 1.0, 2.0) * (0.5 / length) / nrm_ref[...]
    a, b, pp, qq = ab_ref[0, 0], ab_ref[0, 1], pq_ref[0], pq_ref[1]
    uv_ref[0, 0] = (wk * (a * pp - jnp.where(is0, 0.0, b * qq))).astype(uv_ref.dtype)
    uv_ref[0, 1] = (wk * jnp.where(is0, b * nyq_ref[...], a * qq + b * pp)).astype(uv_ref.dtype)


def hy_pointwise(ab, pq, nrm, nyq, order, length):
    n_seq = ab.shape[0]
    tr, tc = _pick(length, 256), 512
    nj = HY_W // tc
    vec = pl.BlockSpec((1, tc), lambda b, i, j: (0, order * nj + j))
    uv = pl.pallas_call(
        functools.partial(_hy_point_kernel, length=length, tr=tr),
        grid=(n_seq, length // tr, nj),
        in_specs=[pl.BlockSpec((1, 2, tr, tc), lambda b, i, j: (b, 0, i, j)),
                  pl.BlockSpec((2, tr, tc), lambda b, i, j: (0, i, order * nj + j)),
                  vec, vec],
        out_specs=pl.BlockSpec((1, 2, tr, tc), lambda b, i, j: (b, 0, i, j)),
        out_shape=jax.ShapeDtypeStruct((n_seq, 2, length, HY_W), BF16),
        compiler_params=_cparams("parallel", "parallel", "parallel"),
        name="hyena_pointwise",
    )(ab.reshape(n_seq, 2, length, HY_W), pq, nrm, nyq)
    return uv.reshape(n_seq, 2 * length, HY_W)


def _hy_inv_kernel(*refs, aliased):
    f_ref, uv_ref, z_ref, gate_ref, bias_ref = refs[:5]
    o_ref, acc_ref = refs[-2:]
    k = pl.program_id(3)

    @pl.when(k == 0)
    def _():
        acc_ref[...] = jnp.zeros_like(acc_ref)

    acc_ref[...] += jnp.dot(f_ref[...], uv_ref[0], preferred_element_type=F32)

    @pl.when(k == pl.num_programs(3) - 1)
    def _():
        o_ref[...] = (gate_ref[...] * (acc_ref[...] + bias_ref[...] * z_ref[...])).astype(o_ref.dtype)


def hy_inverse(inv, uv, z, zcol0, gate, gcol0, bias, row0, length, out_dtype, prev=None):
    t = z.shape[0]
    n_seq = uv.shape[0]
    tm, tn, tk = _pick(length, 512), 1024, _pick(2 * length, 1024)
    r0 = row0 // tm
    ni = length // tm
    rmap = lambda c0: (lambda b, i, j, k: (r0 + b * ni + i, c0 // tn + j))
    in_specs = [pl.BlockSpec((tm, tk), lambda b, i, j, k: (i, k)),
                pl.BlockSpec((1, tk, tn), lambda b, i, j, k: (b, k, j)),
                pl.BlockSpec((tm, tn), rmap(zcol0)),
                pl.BlockSpec((tm, tn), rmap(gcol0)),
                pl.BlockSpec((1, tn), lambda b, i, j, k: (0, j))]
    args = [inv, uv, z, gate, bias.reshape(1, HY_W)]
    aliases = {}
    if prev is not None:
        in_specs.append(pl.BlockSpec(memory_space=pl.ANY))
        args.append(prev)
        aliases = {5: 0}
    return pl.pallas_call(
        functools.partial(_hy_inv_kernel, aliased=prev is not None),
        grid=(n_seq, ni, HY_W // tn, 2 * length // tk),
        in_specs=in_specs,
        out_specs=pl.BlockSpec((tm, tn), rmap(0)),
        out_shape=jax.ShapeDtypeStruct((t, HY_W), out_dtype),
        scratch_shapes=[pltpu.VMEM((tm, tn), F32)],
        input_output_aliases=aliases,
        compiler_params=_cparams("parallel", "parallel", "parallel", "arbitrary"),
        name="hyena_dft_inv",
    )(*args)


def hyena_group(hyc, filt, mats, p, row0, n_seq, length, prev):
    fwd, inv = mats
    pq, nrm, nyq = filt
    prev1, prev2 = prev
    ab = dft_fwd(fwd, hyc, 0, row0, n_seq, length)
    uv = hy_pointwise(ab, pq, nrm, nyq, 0, length)
    z1 = hy_inverse(inv, uv, hyc, 0, hyc, HY_W, p['hy_bias'][0], row0, length, F32, prev1)
    ab = dft_fwd(fwd, z1, 0, row0, n_seq, length)
    uv = hy_pointwise(ab, pq, nrm, nyq, 1, length)
    z2 = hy_inverse(inv, uv, z1, 0, hyc, 2 * HY_W, p['hy_bias'][1], row0, length, BF16, prev2)
    return z1, z2


def _merge_kernel(a_ref, h_ref, s_ref, wa_ref, wh_ref, ws_ref, ga_ref, gh_ref, gs_ref, o_ref):
    acc = jax.nn.sigmoid(ga_ref[...]) * jnp.dot(a_ref[...], wa_ref[...], preferred_element_type=F32)
    acc += jax.nn.sigmoid(gh_ref[...]) * jnp.dot(h_ref[...], wh_ref[...], preferred_element_type=F32)
    acc += jax.nn.sigmoid(gs_ref[...]) * jnp.dot(s_ref[...], ws_ref[...], preferred_element_type=F32)
    o_ref[...] = acc.astype(o_ref.dtype)


def branch_merge(att, hy, ssm, wa, wh, ws, gate_logits, tm=512, tn=512):
    t, kd = att.shape
    d = wa.shape[1]
    nj = d // tn
    xs = pl.BlockSpec((tm, kd), lambda j, i: (i, 0))
    ws_ = pl.BlockSpec((kd, tn), lambda j, i: (0, j))
    gs = lambda b: pl.BlockSpec((tm, tn), lambda j, i: (i, b * nj + j))
    return pl.pallas_call(
        _merge_kernel,
        grid=(nj, t // tm),
        in_specs=[xs, xs, xs, ws_, ws_, ws_, gs(0), gs(1), gs(2)],
        out_specs=pl.BlockSpec((tm, tn), lambda j, i: (i, j)),
        out_shape=jax.ShapeDtypeStruct((t, d), BF16),
        compiler_params=_cparams("parallel", "parallel"),
        name="branch_merge",
    )(att, hy, ssm, wa, wh, ws, gate_logits, gate_logits, gate_logits)


def _swiglu_kernel(x_ref, wg_ref, wu_ref, o_ref, ag_ref, au_ref):
    k = pl.program_id(2)

    @pl.when(k == 0)
    def _():
        ag_ref[...] = jnp.zeros_like(ag_ref)
        au_ref[...] = jnp.zeros_like(au_ref)

    x = x_ref[0]
    ag_ref[...] += jnp.dot(x, wg_ref[0].astype(BF16), preferred_element_type=F32)
    au_ref[...] += jnp.dot(x, wu_ref[0].astype(BF16), preferred_element_type=F32)

    @pl.when(k == pl.num_programs(2) - 1)
    def _():
        g = ag_ref[...]
        o_ref[0] = (g * jax.nn.sigmoid(g) * au_ref[...]).astype(o_ref.dtype)


def expert_swiglu(xs, w_gate, w_up, layer, tn=512, tk=512):
    e, m, d = xs.shape
    f = w_gate.shape[3]
    return pl.pallas_call(
        _swiglu_kernel,
        grid=(e, f // tn, d // tk),
        in_specs=[pl.BlockSpec((1, m, tk), lambda e, j, k: (e, 0, k)),
                  pl.BlockSpec((None, 1, tk, tn), lambda e, j, k: (layer, e, k, j)),
                  pl.BlockSpec((None, 1, tk, tn), lambda e, j, k: (layer, e, k, j))],
        out_specs=pl.BlockSpec((1, m, tn), lambda e, j, k: (e, 0, j)),
        out_shape=jax.ShapeDtypeStruct((e, m, f), BF16),
        scratch_shapes=[pltpu.VMEM((m, tn), F32), pltpu.VMEM((m, tn), F32)],
        compiler_params=_cparams("parallel", "parallel", "arbitrary"),
        name="expert_swiglu",
    )(xs, w_gate, w_up)


def _down_kernel(h_ref, w_ref, g_ref, o_ref, acc_ref):
    k = pl.program_id(2)

    @pl.when(k == 0)
    def _():
        acc_ref[...] = jnp.zeros_like(acc_ref)

    acc_ref[...] += jnp.dot(h_ref[0], w_ref[0].astype(BF16), preferred_element_type=F32)

    @pl.when(k == pl.num_programs(2) - 1)
    def _():
        o_ref[0] = acc_ref[...] * g_ref[0]


def expert_down(hid, w_down, gates, layer, tn=512, tk=512):
    e, m, f = hid.shape
    d = w_down.shape[3]
    return pl.pallas_call(
        _down_kernel,
        grid=(e, d // tn, f // tk),
        in_specs=[pl.BlockSpec((1, m, tk), lambda e, j, k: (e, 0, k)),
                  pl.BlockSpec((None, 1, tk, tn), lambda e, j, k: (layer, e, k, j)),
                  pl.BlockSpec((1, m, 1), lambda e, j, k: (e, 0, 0))],
        out_specs=pl.BlockSpec((1, m, tn), lambda e, j, k: (e, 0, j)),
        out_shape=jax.ShapeDtypeStruct((e, m, d), F32),
        scratch_shapes=[pltpu.VMEM((m, tn), F32)],
        compiler_params=_cparams("parallel", "parallel", "arbitrary"),
        name="expert_down",
    )(hid, w_down, gates)


def ec_moe(xm, p, lay):
    t, d = xm.shape
    w_r = jnp.zeros((d, LANES), BF16).at[:, :N_EXPERTS].set(p['w_router'].astype(BF16))
    logits = mm(xm, w_r, tn=LANES)[:, :N_EXPERTS]
    aff = jax.nn.softmax(logits, axis=-1)
    gates, rows = [], []
    for row0, n_seq, length in ((0, lay.n_p, lay.l_p), (lay.tp, lay.n_s, lay.l_s)):
        cap = EC_CAPACITY * length // N_EXPERTS
        a = aff[row0:row0 + n_seq * length].reshape(n_seq, length, N_EXPERTS)
        g, idx = lax.top_k(jnp.swapaxes(a, 1, 2), cap)
        idx = idx + (row0 + jnp.arange(n_seq, dtype=idx.dtype) * length)[:, None, None]
        gates.append(jnp.swapaxes(g, 0, 1).reshape(N_EXPERTS, n_seq * cap))
        rows.append(jnp.swapaxes(idx, 0, 1).reshape(N_EXPERTS, n_seq * cap))
    gates = jnp.concatenate(gates, axis=1)
    rows = jnp.concatenate(rows, axis=1)
    xs = jnp.take(xm, rows.reshape(-1), axis=0).reshape(N_EXPERTS, -1, d)
    hid = expert_swiglu(xs, p['w_gate'], p['w_up'], p['layer'])
    y = expert_down(hid, p['w_down'], gates[..., None], p['layer'])
    return jnp.zeros((t, d), F32).at[rows.reshape(-1)].add(y.reshape(-1, d))


def rope_tables(lay):
    pos = jnp.arange(lay.l_s)
    row = (pos // GRID_W).astype(F32)
    col = (pos % GRID_W).astype(F32)
    inv = ROPE_THETA ** (-jnp.arange(ROT_FREQS, dtype=F32) / ROT_FREQS)
    ang = jnp.concatenate([row[:, None] * inv] * 2 + [col[:, None] * inv] * 2, axis=1)
    sign = jnp.where((jnp.arange(HEAD_DIM) % (2 * ROT_FREQS)) < ROT_FREQS, -1.0, 1.0).astype(F32)
    cos_s, sin_s = jnp.cos(ang), jnp.sin(ang) * sign
    cos_t = jnp.concatenate([jnp.ones((lay.tp, HEAD_DIM), F32)] + [cos_s] * lay.n_s, axis=0)
    sin_t = jnp.concatenate([jnp.zeros((lay.tp, HEAD_DIM), F32)] + [sin_s] * lay.n_s, axis=0)
    return cos_t, sin_t


def trunk_layer(x, xm, p, mod, mod_next, consts, lay, cache_k, cache_v, state_ssm):
    cos_t, sin_t, mats_p, mats_s = consts
    tp = lay.tp
    u = mm(xm, p['w_in_main'])
    dt_raw = mm(xm, p['w_in_dt'], tn=LANES)
    gate_logits = mm(xm, p['w_in_gate'])

    q, kr, vb, kf, vf = qkv_prep(u, cos_t, sin_t, p['q_norm'], p['k_norm'])
    k_p = kr[:tp].reshape(lay.n_p, lay.l_p, KV_W)
    v_p = vb[:tp].reshape(lay.n_p, lay.l_p, KV_W)
    k_s = jnp.concatenate([cache_k.reshape(lay.n_s, -1, KV_W).astype(BF16),
                           kr[tp:].reshape(lay.n_s, lay.l_s, KV_W)], axis=1)
    v_s = jnp.concatenate([cache_v.reshape(lay.n_s, -1, KV_W).astype(BF16),
                           vb[tp:].reshape(lay.n_s, lay.l_s, KV_W)], axis=1)
    att = attention(q, k_p, v_p, 0, lay.n_p, lay.l_p, lay.l_p)
    att = attention(q, k_s, v_s, tp, lay.n_s, lay.l_s, ROWS, prev=att)

    hyc = conv3(u, HY_OFF, 3 * HY_W, p['hy_conv_w'], p['hy_conv_b'], lay, silu=False)
    filt_p = hyena_filters(lay.l_p, p, mats_p[0])
    filt_s = hyena_filters(lay.l_s, p, mats_s[0])
    z1, hy = hyena_group(hyc, filt_p, mats_p, p, 0, lay.n_p, lay.l_p, (None, None))
    _, hy = hyena_group(hyc, filt_s, mats_s, p, tp, lay.n_s, lay.l_s, (z1, hy))

    xbc = conv3(u, XBC_OFF, SSM_CONV_DIM, p['ssm_conv_w'], p['ssm_conv_b'], lay, silu=True)
    dtc = dt_prep(dt_raw, p['ssm_dt_bias'], p['ssm_a_log'])
    dtr = dtc.T
    zero_state = jnp.zeros((lay.n_p, 2, SSM_HEADS, SSM_HEADDIM, SSM_STATE), F32)
    y2, states = ssd(xbc, dtc, dtr, zero_state, 0, lay.n_p, lay.l_p)
    y2, _ = ssd(xbc, dtc, dtr, state_ssm, tp, lay.n_s, lay.l_s, prev=y2)
    ssm = ssd_gate(y2, xbc, u, p['ssm_d'], p['ssm_norm'])

    merged = branch_merge(att, hy, ssm, p['w_br_att'], p['w_br_hy'], p['w_br_ssm'], gate_logits)
    m = mm(merged, p['w_out'])
    x1, xm2 = ln_mod(x, m, mod, 2, p['ln1_g'], p['ln1_b'], lay, mod_next=mod, sec_sc=4, sec_sh=3)
    f = ec_moe(xm2, p, lay)
    x2, xm_next = ln_mod(x1, f, mod, 5, p['ln2_g'], p['ln2_b'], lay, mod_next=mod_next, sec_sc=1, sec_sh=0)
    new_k = kf[:tp].reshape(lay.n_p, lay.l_p, N_KV_HEADS, HEAD_DIM)
    new_v = vf[:tp].reshape(lay.n_p, lay.l_p, N_KV_HEADS, HEAD_DIM)
    return x2, xm_next, (new_k, new_v, states)


def kernel(x_prompt, x_sample, cache_k, cache_v, state_ssm, c, c_ctx, w_mod, b_mod, w_in, q_norm, k_norm, hy_conv_w, hy_conv_b, hy_w1, hy_b1, hy_freq, hy_w2, hy_b2, hy_w3, hy_b3, hy_bias, ssm_conv_w, ssm_conv_b, ssm_dt_bias, ssm_a_log, ssm_d, ssm_norm, w_br_att, w_br_hy, w_br_ssm, w_out, ln1_g, ln1_b, w_router, w_gate, w_up, w_down, ln2_g, ln2_b):
    n_p, l_p, d = x_prompt.shape
    n_s, l_s, _ = x_sample.shape
    depth = w_in.shape[0]
    lay = Layout(n_p, l_p, n_s, l_s)
    x = jnp.concatenate([x_prompt.reshape(lay.tp, d), x_sample.reshape(lay.ts, d)], axis=0)

    cond = jnp.zeros((N_COND_PAD, d), F32).at[0].set(c_ctx).at[1:1 + n_s].set(c)
    act = (cond * jax.nn.sigmoid(cond)).astype(BF16)
    mod_all = gmm(act[None], w_mod, tm=N_COND_PAD, tn=2048, tk=1024, share_x=True) + b_mod[:, None, :]
    mods = [mod_all[l].reshape(N_COND_PAD, 1, 6 * d) for l in range(depth)]

    consts = rope_tables(lay) + (dft_matrices(l_p), dft_matrices(l_s))
    xm = modulate(x, mods[0], 1, 0, lay)
    new_k, new_v, new_s = [], [], []
    for l in range(depth):
        p = dict(w_in_main=w_in[l, :, :DT_OFF].astype(BF16),
                 w_in_dt=w_in[l, :, DT_OFF:DT_OFF + LANES].astype(BF16),
                 w_in_gate=w_in[l, :, GATE_OFF:].astype(BF16),
                 q_norm=q_norm[l], k_norm=k_norm[l],
                 hy_conv_w=hy_conv_w[l], hy_conv_b=hy_conv_b[l], hy_w1=hy_w1[l], hy_b1=hy_b1[l],
                 hy_freq=hy_freq[l], hy_w2=hy_w2[l], hy_b2=hy_b2[l], hy_w3=hy_w3[l], hy_b3=hy_b3[l],
                 hy_bias=hy_bias[l], ssm_conv_w=ssm_conv_w[l], ssm_conv_b=ssm_conv_b[l],
                 ssm_dt_bias=ssm_dt_bias[l], ssm_a_log=ssm_a_log[l], ssm_d=ssm_d[l], ssm_norm=ssm_norm[l],
                 w_br_att=w_br_att[l].astype(BF16), w_br_hy=w_br_hy[l].astype(BF16),
                 w_br_ssm=w_br_ssm[l].astype(BF16), w_out=w_out[l].astype(BF16),
                 ln1_g=ln1_g[l], ln1_b=ln1_b[l], w_router=w_router[l],
                 w_gate=w_gate, w_up=w_up, w_down=w_down, layer=l,
                 ln2_g=ln2_g[l], ln2_b=ln2_b[l])
        mod_next = mods[l + 1] if l + 1 < depth else None
        x, xm, (k_l, v_l, s_l) = trunk_layer(x, xm, p, mods[l], mod_next, consts, lay,
                                             cache_k[:, l], cache_v[:, l], state_ssm[:, l])
        new_k.append(k_l)
        new_v.append(v_l)
        new_s.append(s_l)
    y_prompt = x[:lay.tp].reshape(n_p, l_p, d)
    y_sample = x[lay.tp:].reshape(n_s, l_s, d)
    return (y_prompt, y_sample, jnp.stack(new_k, axis=1), jnp.stack(new_v, axis=1), jnp.stack(new_s, axis=1))
```

```python
import functools
import math

import jax
import jax.numpy as jnp
from jax import lax
from jax.experimental import pallas as pl
from jax.experimental.pallas import tpu as pltpu

F32 = jnp.float32
BF16 = jnp.bfloat16

D_MODEL = 4096
DEPTH = 2
GRID_W = 64
N_HEADS = 16
N_KV_HEADS = 4
KV_REP = N_HEADS // N_KV_HEADS
HEAD_DIM = 128
ATT_W = N_HEADS * HEAD_DIM
KV_W = N_KV_HEADS * HEAD_DIM
ROT_FREQS = HEAD_DIM // 4
ROPE_THETA = 10000.0
HY_W = 2048
HY_ORDER = 2
HY_BANDS = 16
HY_DECAY_TARGET = 1e-2
HY_FAST_DECAY = 0.3
HY_SLOW_DECAY = 1.5
SSM_W = 2048
SSM_HEADDIM = 64
SSM_HEADS = SSM_W // SSM_HEADDIM
SSM_GROUPS = 8
SSM_REP = SSM_HEADS // SSM_GROUPS
SSM_STATE = 128
SSM_CHUNK = 128
SSM_CONV_DIM = SSM_W + 2 * SSM_GROUPS * SSM_STATE
N_EXPERTS = 16
EC_CAPACITY = 2
MOE_FF = 2048
N_BRANCH = 3
Q_OFF = 0
K_OFF = Q_OFF + ATT_W
V_OFF = K_OFF + KV_W
HY_OFF = V_OFF + KV_W
Z_OFF = HY_OFF + 3 * HY_W
XBC_OFF = Z_OFF + SSM_W
DT_OFF = XBC_OFF + SSM_CONV_DIM
GATE_OFF = DT_OFF + 2 * SSM_HEADS
ALPHA = (2 * DEPTH) ** 0.25
LN_EPS = 1e-5
RMS_EPS = 1e-6
N_COND_PAD = 8
LANES = 128
SUBLANES = 8

VMEM_LIMIT_BYTES = 56 * 1024 * 1024

NT_DIMS = (((1,), (1,)), ((), ()))
TN_DIMS = (((0,), (0,)), ((), ()))


def _cparams(*sem):
    return pltpu.CompilerParams(dimension_semantics=sem, vmem_limit_bytes=VMEM_LIMIT_BYTES)


def _pick(dim, pref):
    t = min(dim, pref)
    while dim % t:
        t //= 2
    return t


def _mm_kernel(x_ref, w_ref, o_ref, acc_ref):
    k = pl.program_id(3)

    @pl.when(k == 0)
    def _():
        acc_ref[...] = jnp.zeros_like(acc_ref)

    acc_ref[...] += jnp.dot(x_ref[0].astype(BF16), w_ref[0].astype(BF16), preferred_element_type=F32)

    @pl.when(k == pl.num_programs(3) - 1)
    def _():
        o_ref[0] = acc_ref[...].astype(o_ref.dtype)


def _mm_fullk_kernel(x_ref, w_ref, o_ref):
    o_ref[0] = jnp.dot(x_ref[0].astype(BF16), w_ref[0].astype(BF16),
                       preferred_element_type=F32).astype(o_ref.dtype)


def gmm(x, w, out_dtype=F32, tm=1024, tn=512, tk=4096, share_x=False):
    g, kd, n = w.shape
    m = x.shape[1]
    tm, tn, tk = _pick(m, tm), _pick(n, tn), _pick(kd, tk)
    if tk == kd:
        xmap = (lambda e, i, j: (0, i, 0)) if share_x else (lambda e, i, j: (e, i, 0))
        return pl.pallas_call(
            _mm_fullk_kernel,
            grid=(g, m // tm, n // tn),
            in_specs=[pl.BlockSpec((1, tm, kd), xmap),
                      pl.BlockSpec((1, kd, tn), lambda e, i, j: (e, 0, j))],
            out_specs=pl.BlockSpec((1, tm, tn), lambda e, i, j: (e, i, j)),
            out_shape=jax.ShapeDtypeStruct((g, m, n), out_dtype),
            compiler_params=_cparams("parallel", "parallel", "parallel"),
            name="gmm",
        )(x, w)
    xmap = (lambda e, i, j, k: (0, i, k)) if share_x else (lambda e, i, j, k: (e, i, k))
    return pl.pallas_call(
        _mm_kernel,
        grid=(g, m // tm, n // tn, kd // tk),
        in_specs=[pl.BlockSpec((1, tm, tk), xmap),
                  pl.BlockSpec((1, tk, tn), lambda e, i, j, k: (e, k, j))],
        out_specs=pl.BlockSpec((1, tm, tn), lambda e, i, j, k: (e, i, j)),
        out_shape=jax.ShapeDtypeStruct((g, m, n), out_dtype),
        scratch_shapes=[pltpu.VMEM((tm, tn), F32)],
        compiler_params=_cparams("parallel", "parallel", "parallel", "arbitrary"),
        name="gmm",
    )(x, w)


def mm(x, w, out_dtype=F32, **kw):
    return gmm(x[None], w[None], out_dtype, **kw)[0]


class Layout:
    def __init__(self, n_p, l_p, n_s, l_s):
        self.n_p, self.l_p, self.n_s, self.l_s = n_p, l_p, n_s, l_s
        self.tp = n_p * l_p
        self.ts = n_s * l_s
        self.t = self.tp + self.ts

    def group_of_block(self, i, rows):
        bp = self.tp // rows
        return jnp.where(i < bp, 0, 1 + (i - bp) // (self.l_s // rows))

    def seq_edges(self, i, rows):
        bp = self.tp // rows
        per_p, per_s = self.l_p // rows, self.l_s // rows
        first = jnp.where(i < bp, i % per_p == 0, (i - bp) % per_s == 0)
        last = jnp.where(i < bp, i % per_p == per_p - 1, (i - bp) % per_s == per_s - 1)
        return first, last


ROWS = 256


def _modulate_kernel(x_ref, sc_ref, sh_ref, o_ref):
    o_ref[...] = (x_ref[...] * (1.0 + sc_ref[0]) + sh_ref[0]).astype(o_ref.dtype)


def modulate(x, mod, sec_sc, sec_sh, lay):
    t, d = x.shape
    grp = lambda i: lay.group_of_block(i, ROWS)
    return pl.pallas_call(
        _modulate_kernel,
        grid=(t // ROWS,),
        in_specs=[pl.BlockSpec((ROWS, d), lambda i: (i, 0)),
                  pl.BlockSpec((1, 1, d), lambda i: (grp(i), 0, sec_sc)),
                  pl.BlockSpec((1, 1, d), lambda i: (grp(i), 0, sec_sh))],
        out_specs=pl.BlockSpec((ROWS, d), lambda i: (i, 0)),
        out_shape=jax.ShapeDtypeStruct((t, d), BF16),
        compiler_params=_cparams("parallel"),
        name="modulate",
    )(x, mod, mod)


def _ln_mod_kernel(*refs, with_mod):
    if with_mod:
        x_ref, m_ref, gt_ref, lg_ref, lb_ref, sc_ref, sh_ref, o_ref, om_ref = refs
    else:
        x_ref, m_ref, gt_ref, lg_ref, lb_ref, o_ref = refs
    r = ALPHA * x_ref[...] + gt_ref[0] * m_ref[...]
    mu = jnp.mean(r, axis=-1, keepdims=True)
    dlt = r - mu
    var = jnp.mean(dlt * dlt, axis=-1, keepdims=True)
    y = dlt * lax.rsqrt(var + LN_EPS) * lg_ref[...] + lb_ref[...]
    o_ref[...] = y
    if with_mod:
        om_ref[...] = (y * (1.0 + sc_ref[0]) + sh_ref[0]).astype(om_ref.dtype)


def ln_mod(x, m, mod, sec_gate, ln_g, ln_b, lay, mod_next=None, sec_sc=0, sec_sh=0):
    t, d = x.shape
    rows = ROWS // 2
    grp = lambda i: lay.group_of_block(i, rows)
    with_mod = mod_next is not None
    row_spec = pl.BlockSpec((rows, d), lambda i: (i, 0))
    vec_spec = pl.BlockSpec((1, d), lambda i: (0, 0))
    in_specs = [row_spec, row_spec,
                pl.BlockSpec((1, 1, d), lambda i: (grp(i), 0, sec_gate)), vec_spec, vec_spec]
    args = [x, m, mod, ln_g.reshape(1, d), ln_b.reshape(1, d)]
    out_specs = [row_spec]
    out_shape = [jax.ShapeDtypeStruct((t, d), F32)]
    if with_mod:
        in_specs += [pl.BlockSpec((1, 1, d), lambda i: (grp(i), 0, sec_sc)),
                     pl.BlockSpec((1, 1, d), lambda i: (grp(i), 0, sec_sh))]
        args += [mod_next, mod_next]
        out_specs.append(row_spec)
        out_shape.append(jax.ShapeDtypeStruct((t, d), BF16))
    res = pl.pallas_call(
        functools.partial(_ln_mod_kernel, with_mod=with_mod),
        grid=(t // rows,),
        in_specs=in_specs, out_specs=out_specs, out_shape=out_shape,
        compiler_params=_cparams("parallel"),
        name="ln_mod",
    )(*args)
    return (res[0], res[1]) if with_mod else (res[0], None)


def _qkv_prep_kernel(u_ref, cos_ref, sin_ref, qn_ref, kn_ref, q_ref, kr_ref, vb_ref, kf_ref, vf_ref):
    cos = cos_ref[...]
    sin = sin_ref[...]
    lane = lax.broadcasted_iota(jnp.int32, cos.shape, 1)
    lane_lo = (lane % (2 * ROT_FREQS)) < ROT_FREQS

    def norm(x, g):
        return x * lax.rsqrt(jnp.mean(x * x, axis=-1, keepdims=True) + RMS_EPS) * g

    def rope(x):
        sw = jnp.where(lane_lo, pltpu.roll(x, LANES - ROT_FREQS, 1), pltpu.roll(x, ROT_FREQS, 1))
        return x * cos + sw * sin

    for h in range(N_HEADS):
        sl = slice(Q_OFF + h * HEAD_DIM, Q_OFF + (h + 1) * HEAD_DIM)
        q_ref[:, h * HEAD_DIM:(h + 1) * HEAD_DIM] = rope(norm(u_ref[:, sl], qn_ref[...])).astype(q_ref.dtype)
    for h in range(N_KV_HEADS):
        o = slice(h * HEAD_DIM, (h + 1) * HEAD_DIM)
        kk = norm(u_ref[:, K_OFF + h * HEAD_DIM:K_OFF + (h + 1) * HEAD_DIM], kn_ref[...])
        kf_ref[:, o] = kk
        kr_ref[:, o] = rope(kk).astype(kr_ref.dtype)
        vv = u_ref[:, V_OFF + h * HEAD_DIM:V_OFF + (h + 1) * HEAD_DIM]
        vf_ref[:, o] = vv
        vb_ref[:, o] = vv.astype(vb_ref.dtype)


def qkv_prep(u, cos_t, sin_t, q_norm, k_norm):
    t = u.shape[0]
    row = lambda w: pl.BlockSpec((ROWS, w), lambda i: (i, 0))
    vec = pl.BlockSpec((1, HEAD_DIM), lambda i: (0, 0))
    return pl.pallas_call(
        _qkv_prep_kernel,
        grid=(t // ROWS,),
        in_specs=[row(HY_OFF), row(HEAD_DIM), row(HEAD_DIM), vec, vec],
        out_specs=[row(ATT_W), row(KV_W), row(KV_W), row(KV_W), row(KV_W)],
        out_shape=[jax.ShapeDtypeStruct((t, ATT_W), BF16), jax.ShapeDtypeStruct((t, KV_W), BF16),
                   jax.ShapeDtypeStruct((t, KV_W), BF16), jax.ShapeDtypeStruct((t, KV_W), F32),
                   jax.ShapeDtypeStruct((t, KV_W), F32)],
        compiler_params=_cparams("parallel"),
        name="qkv_prep",
    )(u, cos_t, sin_t, q_norm.reshape(1, HEAD_DIM), k_norm.reshape(1, HEAD_DIM))


def _attn_kernel(*refs, aliased):
    q_ref, k_ref, v_ref = refs[:3]
    o_ref = refs[-1]
    scale = HEAD_DIM ** -0.5
    k = k_ref[0]
    v = v_ref[0]
    for r in range(KV_REP):
        sl = slice(r * HEAD_DIM, (r + 1) * HEAD_DIM)
        s = lax.dot_general(q_ref[:, sl], k, NT_DIMS, preferred_element_type=F32)
        m = jnp.max(s, axis=1, keepdims=True)
        p = jnp.exp((s - m) * scale)
        l = jnp.sum(p, axis=1, keepdims=True)
        o = jnp.dot(p.astype(BF16), v, preferred_element_type=F32)
        o_ref[:, sl] = (o / l).astype(o_ref.dtype)


def attention(q, k, v, row0, n_seq, l_q, tq, prev=None):
    t = q.shape[0]
    l_k = k.shape[1]
    gw = KV_REP * HEAD_DIM
    nq = l_q // tq
    b0 = row0 // tq
    qmap = lambda b, g, i: (b0 + b * nq + i, g)
    in_specs = [pl.BlockSpec((tq, gw), qmap),
                pl.BlockSpec((1, l_k, HEAD_DIM), lambda b, g, i: (b, 0, g)),
                pl.BlockSpec((1, l_k, HEAD_DIM), lambda b, g, i: (b, 0, g))]
    args = [q, k, v]
    aliases = {}
    if prev is not None:
        in_specs.append(pl.BlockSpec(memory_space=pl.ANY))
        args.append(prev)
        aliases = {3: 0}
    return pl.pallas_call(
        functools.partial(_attn_kernel, aliased=prev is not None),
        grid=(n_seq, N_KV_HEADS, nq),
        in_specs=in_specs,
        out_specs=pl.BlockSpec((tq, gw), qmap),
        out_shape=jax.ShapeDtypeStruct((t, ATT_W), BF16),
        input_output_aliases=aliases,
        compiler_params=_cparams("parallel", "parallel", "parallel"),
        name="attention",
    )(*args)


CONV_ROWS = 1024


def _conv3_kernel(x_ref, prev_ref, next_ref, w_ref, b_ref, *o_refs, lay, silu):
    i = pl.program_id(0)
    x = x_ref[...]
    rows = lax.broadcasted_iota(jnp.int32, (CONV_ROWS, 1), 0)
    grow = rows + i * CONV_ROWS
    in_p = grow < lay.tp
    pos = jnp.where(in_p, grow & (lay.l_p - 1), (grow - lay.tp) & (lay.l_s - 1))
    last = jnp.where(in_p, lay.l_p - 1, lay.l_s - 1)
    xm1 = jnp.where(rows == 0, prev_ref[SUBLANES - 1:SUBLANES, :], pltpu.roll(x, 1, 0))
    xp1 = jnp.where(rows == CONV_ROWS - 1, next_ref[0:1, :], pltpu.roll(x, CONV_ROWS - 1, 0))
    xm1 = jnp.where(pos == 0, 0.0, xm1)
    xp1 = jnp.where(pos == last, 0.0, xp1)
    y = xm1 * w_ref[0:1, :] + x * w_ref[1:2, :] + xp1 * w_ref[2:3, :] + b_ref[...]
    if silu:
        y = y * jax.nn.sigmoid(y)
    for o_ref in o_refs:
        o_ref[...] = y.astype(o_ref.dtype)


def conv3(u, col0, width, w, b, lay, silu, out_dtypes=(F32,), tc=1024):
    t = u.shape[0]
    assert lay.l_p & (lay.l_p - 1) == 0 and lay.l_s & (lay.l_s - 1) == 0
    c0 = col0 // tc
    sub = CONV_ROWS // SUBLANES
    n_sub = t // SUBLANES
    res = pl.pallas_call(
        functools.partial(_conv3_kernel, lay=lay, silu=silu),
        grid=(t // CONV_ROWS, width // tc),
        in_specs=[pl.BlockSpec((CONV_ROWS, tc), lambda i, j: (i, c0 + j)),
                  pl.BlockSpec((SUBLANES, tc), lambda i, j: (jnp.maximum(i * sub - 1, 0), c0 + j)),
                  pl.BlockSpec((SUBLANES, tc), lambda i, j: (jnp.minimum((i + 1) * sub, n_sub - 1), c0 + j)),
                  pl.BlockSpec((3, tc), lambda i, j: (0, j)),
                  pl.BlockSpec((1, tc), lambda i, j: (0, j))],
        out_specs=[pl.BlockSpec((CONV_ROWS, tc), lambda i, j: (i, j)) for _ in out_dtypes],
        out_shape=[jax.ShapeDtypeStruct((t, width), dt) for dt in out_dtypes],
        compiler_params=_cparams("parallel", "parallel"),
        name="conv3",
    )(u, u, u, w, b.reshape(1, width))
    return res if len(out_dtypes) > 1 else res[0]


def _dt_prep_kernel(raw_ref, bias_ref, a_ref, o_ref):
    x = raw_ref[...] + bias_ref[...]
    dt = jnp.maximum(x, 0.0) + jnp.log1p(jnp.exp(-jnp.abs(x)))
    o_ref[...] = jnp.where(lax.broadcasted_iota(jnp.int32, x.shape, 1) < 2 * SSM_HEADS,
                           dt, pltpu.roll(dt, 2 * SSM_HEADS, 1) * a_ref[...])


def dt_prep(raw, dt_bias, a_log):
    t = raw.shape[0]
    nh2 = 2 * SSM_HEADS
    bias = jnp.zeros((1, LANES), F32).at[0, :nh2].set(dt_bias.reshape(-1))
    a = jnp.zeros((1, LANES), F32).at[0, nh2:2 * nh2].set(-jnp.exp(a_log.reshape(-1)))
    return pl.pallas_call(
        _dt_prep_kernel,
        grid=(t // ROWS,),
        in_specs=[pl.BlockSpec((ROWS, LANES), lambda i: (i, 0)),
                  pl.BlockSpec((1, LANES), lambda i: (0, 0)),
                  pl.BlockSpec((1, LANES), lambda i: (0, 0))],
        out_specs=pl.BlockSpec((ROWS, LANES), lambda i: (i, 0)),
        out_shape=jax.ShapeDtypeStruct((t, LANES), F32),
        compiler_params=_cparams("parallel"),
        name="dt_prep",
    )(raw, bias, a)


def _prefix_sum(x, axis):
    idx = lax.broadcasted_iota(jnp.int32, x.shape, axis)
    d = 1
    while d < SSM_CHUNK:
        x = x + jnp.where(idx >= d, pltpu.roll(x, d, axis), 0.0)
        d *= 2
    return x


def _ssd_kernel(*refs, n_chunks, aliased):
    xs_ref, xst_ref, b_ref, c_ref, dtc_ref, dtr_ref, init_ref = refs[:7]
    y_ref, fin_ref, st_ref = refs[-3:]
    d = pl.program_id(1)
    c = pl.program_id(2)
    tt = SSM_CHUNK
    nh = SSM_HEADS

    @pl.when(c == 0)
    def _():
        st_ref[...] = init_ref[0, 0]

    fwd = d == 0
    dtc = dtc_ref[...]
    dtr = dtr_ref[...]
    da_c = jnp.where(fwd, dtc[:, 2 * nh:3 * nh], dtc[:, 3 * nh:4 * nh])
    dtv_r = jnp.where(fwd, dtr[0:nh, :], dtr[nh:2 * nh, :])
    da_r = jnp.where(fwd, dtr[2 * nh:3 * nh, :], dtr[3 * nh:4 * nh, :])
    pc = _prefix_sum(da_c, 0)
    pr = _prefix_sum(da_r, 1)
    tot_c = pc[tt - 1:tt, :]
    tot_r = pr[:, tt - 1:tt]
    acs_c = jnp.where(fwd, pc, tot_c - pc + da_c)
    acs_r = jnp.where(fwd, pr, tot_r - pr + da_r)
    li = lax.broadcasted_iota(jnp.int32, (tt, tt), 0)
    si = lax.broadcasted_iota(jnp.int32, (tt, tt), 1)
    mask = jnp.where(fwd, li - si, si - li) >= 0
    w_r = dtv_r * jnp.exp(tot_r - acs_r)
    cdec_r = jnp.exp(tot_r)
    ns = SSM_STATE
    hp = SSM_HEADDIM
    for g in range(SSM_GROUPS):
        bg = b_ref[:, g * ns:(g + 1) * ns].astype(BF16)
        cg32 = c_ref[:, g * ns:(g + 1) * ns]
        cb = lax.dot_general(cg32.astype(BF16), bg, NT_DIMS, preferred_element_type=F32)
        for r in range(SSM_REP):
            h = g * SSM_REP + r
            acs_l = jnp.broadcast_to(acs_c[:, h:h + 1], (tt, tt))
            dec = jnp.exp(jnp.where(mask, acs_l - acs_r[h:h + 1, :], -jnp.inf))
            m = (cb * dec * dtv_r[h:h + 1, :]).astype(BF16)
            c_in = (cg32 * jnp.exp(acs_l)).astype(BF16)
            state = st_ref[h]
            y = jnp.dot(m, xs_ref[:, h * hp:(h + 1) * hp].astype(BF16), preferred_element_type=F32)
            y += lax.dot_general(c_in, state.astype(BF16), NT_DIMS, preferred_element_type=F32)
            y_ref[0, :, h * hp:(h + 1) * hp] = y
            xw = (xst_ref[h * hp:(h + 1) * hp, :] * w_r[h:h + 1, :]).astype(BF16)
            st_ref[h] = state * cdec_r[h:h + 1, :] + jnp.dot(xw, bg, preferred_element_type=F32)

    @pl.when(c == n_chunks - 1)
    def _():
        fin_ref[0, 0] = st_ref[...]


def ssd(xbc, xst, dtc, dtr, init, row0, n_seq, length, prev=None):
    t = xbc.shape[0]
    tt = SSM_CHUNK
    nc = length // tt
    b0 = row0 // tt
    gn = SSM_GROUPS * SSM_STATE

    def blk(b, d, c):
        return b0 + b * nc + jnp.where(d == 0, c, nc - 1 - c)

    st_shape = (SSM_HEADS, SSM_HEADDIM, SSM_STATE)
    st_spec = pl.BlockSpec((1, 1) + st_shape, lambda b, d, c: (b, d, 0, 0, 0))
    in_specs = [pl.BlockSpec((tt, SSM_W), lambda b, d, c: (blk(b, d, c), 0)),
                pl.BlockSpec((SSM_W, tt), lambda b, d, c: (0, blk(b, d, c))),
                pl.BlockSpec((tt, gn), lambda b, d, c: (blk(b, d, c), SSM_W // gn)),
                pl.BlockSpec((tt, gn), lambda b, d, c: (blk(b, d, c), SSM_W // gn + 1)),
                pl.BlockSpec((tt, LANES), lambda b, d, c: (blk(b, d, c), 0)),
                pl.BlockSpec((LANES, tt), lambda b, d, c: (0, blk(b, d, c))),
                st_spec]
    args = [xbc, xst, xbc, xbc, dtc, dtr, init]
    aliases = {}
    if prev is not None:
        in_specs.append(pl.BlockSpec(memory_space=pl.ANY))
        args.append(prev)
        aliases = {7: 0}
    return pl.pallas_call(
        functools.partial(_ssd_kernel, n_chunks=nc, aliased=prev is not None),
        grid=(n_seq, 2, nc),
        in_specs=in_specs,
        out_specs=[pl.BlockSpec((1, tt, SSM_W), lambda b, d, c: (d, blk(b, d, c), 0)), st_spec],
        out_shape=[jax.ShapeDtypeStruct((2, t, SSM_W), F32),
                   jax.ShapeDtypeStruct((n_seq, 2) + st_shape, F32)],
        scratch_shapes=[pltpu.VMEM(st_shape, F32)],
        input_output_aliases=aliases,
        compiler_params=_cparams("parallel", "parallel", "arbitrary"),
        name="ssd",
    )(*args)


def _ssd_gate_kernel(y_ref, xs_ref, zlo_ref, zhi_ref, d_ref, g_ref, o_ref):
    z = jnp.concatenate([zlo_ref[...], zhi_ref[...]], axis=1)
    y = (y_ref[0] + y_ref[1] + d_ref[...] * xs_ref[...]) * (z * jax.nn.sigmoid(z))
    y = y * lax.rsqrt(jnp.mean(y * y, axis=-1, keepdims=True) + RMS_EPS) * g_ref[...]
    o_ref[...] = y.astype(o_ref.dtype)


def ssd_gate(y2, xbc, u, ssm_d, ssm_norm):
    t = xbc.shape[0]
    half = SSM_W // 2
    zb = Z_OFF // half
    vec = pl.BlockSpec((1, SSM_W), lambda i: (0, 0))
    return pl.pallas_call(
        _ssd_gate_kernel,
        grid=(t // ROWS,),
        in_specs=[pl.BlockSpec((2, ROWS, SSM_W), lambda i: (0, i, 0)),
                  pl.BlockSpec((ROWS, SSM_W), lambda i: (i, 0)),
                  pl.BlockSpec((ROWS, half), lambda i: (i, zb)),
                  pl.BlockSpec((ROWS, half), lambda i: (i, zb + 1)),
                  vec, vec],
        out_specs=pl.BlockSpec((ROWS, SSM_W), lambda i: (i, 0)),
        out_shape=jax.ShapeDtypeStruct((t, SSM_W), BF16),
        compiler_params=_cparams("parallel"),
        name="ssd_gate",
    )(y2, xbc, u, u, jnp.repeat(ssm_d, SSM_HEADDIM).reshape(1, SSM_W), ssm_norm.reshape(1, SSM_W))


def dft_matrices(length):
    blk = 64
    two_l = 2 * length
    k = jnp.arange(length, dtype=jnp.int32)[:, None]
    a = jnp.arange(length // blk, dtype=jnp.int32)[None, :]
    b = jnp.arange(blk, dtype=jnp.int32)[None, :]
    xa = ((k * (a * blk)) % two_l).astype(F32) * (math.pi / length)
    xb = ((k * b) % two_l).astype(F32) * (math.pi / length)
    ca, sa, cb, sb = jnp.cos(xa)[:, :, None], jnp.sin(xa)[:, :, None], jnp.cos(xb)[:, None, :], jnp.sin(xb)[:, None, :]
    cos = (ca * cb - sa * sb).reshape(length, length)
    sin = (sa * cb + ca * sb).reshape(length, length)
    idx = jnp.arange(length, dtype=jnp.int32)
    alt = jnp.where(idx % 2 == 0, 1.0, -1.0).astype(F32)
    sin_fwd = jnp.where(idx[:, None] == 0, alt[None, :], sin)
    sin_inv = jnp.where(idx[None, :] == 0, alt[:, None], sin)
    return jnp.stack([cos, sin_fwd]).astype(BF16), jnp.stack([cos, sin_inv]).astype(BF16)


def _hyfilt_kernel(hid_ref, wf_ref, wb_ref, bf_ref, bb_ref, dl_ref, o_ref, nrm_ref, nyq_ref, *, length, tr):
    i = pl.program_id(2)
    hid = hid_ref[...].astype(BF16)
    t_idx = lax.broadcasted_iota(jnp.int32, (tr, 1), 0) + i * tr
    win = jnp.exp(-(t_idx.astype(F32) / length) * dl_ref[...])
    hf = (jnp.dot(hid, wf_ref[...].astype(BF16), preferred_element_type=F32) + bf_ref[...]) * win
    hb = (jnp.dot(hid, wb_ref[...].astype(BF16), preferred_element_type=F32) + bb_ref[...]) * win
    hb = jnp.where(t_idx == 0, 0.0, hb)
    o_ref[0] = (hf + hb).astype(o_ref.dtype)
    o_ref[1] = (hf - hb).astype(o_ref.dtype)
    sign = jnp.where(t_idx % 2 == 0, 1.0, -1.0)

    @pl.when(i == 0)
    def _():
        nrm_ref[...] = jnp.zeros_like(nrm_ref)
        nyq_ref[...] = jnp.zeros_like(nyq_ref)

    nrm_ref[...] += jnp.sum(jnp.abs(hf) + jnp.abs(hb), axis=0, keepdims=True)
    nyq_ref[...] += jnp.sum(sign * (hf + hb), axis=0, keepdims=True)


def hyena_filters(length, p, fwd):
    t = jnp.arange(length, dtype=F32) / length
    bands = jnp.linspace(1e-4, HY_BANDS - 1, HY_BANDS, dtype=F32)
    ang = 2.0 * math.pi * t[:, None] * bands
    feats = jnp.concatenate([t[:, None], jnp.cos(ang), jnp.sin(ang)], axis=-1)
    hid = jnp.sin(p['hy_freq'] * (feats @ p['hy_w1'] + p['hy_b1']))
    hid = jnp.sin(p['hy_freq'] * (hid @ p['hy_w2'] + p['hy_b2']))
    deltas = jnp.abs(jnp.linspace(math.log(HY_DECAY_TARGET) / HY_SLOW_DECAY,
                                  math.log(HY_DECAY_TARGET) / HY_FAST_DECAY, HY_W, dtype=F32)).reshape(1, HY_W)
    ffn = hid.shape[1]
    tr, tc = _pick(length, 256), 512
    nj = HY_W // tc
    w3, b3 = p['hy_w3'], p['hy_b3'].reshape(1, -1)
    ow = HY_ORDER * HY_W
    hsd, nrm, nyq = pl.pallas_call(
        functools.partial(_hyfilt_kernel, length=length, tr=tr),
        grid=(HY_ORDER, nj, length // tr),
        in_specs=[pl.BlockSpec((tr, ffn), lambda n, j, i: (i, 0)),
                  pl.BlockSpec((ffn, tc), lambda n, j, i: (0, (2 * n) * nj + j)),
                  pl.BlockSpec((ffn, tc), lambda n, j, i: (0, (2 * n + 1) * nj + j)),
                  pl.BlockSpec((1, tc), lambda n, j, i: (0, (2 * n) * nj + j)),
                  pl.BlockSpec((1, tc), lambda n, j, i: (0, (2 * n + 1) * nj + j)),
                  pl.BlockSpec((1, tc), lambda n, j, i: (0, j))],
        out_specs=[pl.BlockSpec((2, tr, tc), lambda n, j, i: (0, i, n * nj + j)),
                   pl.BlockSpec((1, tc), lambda n, j, i: (0, n * nj + j)),
                   pl.BlockSpec((1, tc), lambda n, j, i: (0, n * nj + j))],
        out_shape=[jax.ShapeDtypeStruct((2, length, ow), BF16),
                   jax.ShapeDtypeStruct((1, ow), F32), jax.ShapeDtypeStruct((1, ow), F32)],
        compiler_params=_cparams("parallel", "parallel", "arbitrary"),
        name="hyena_filter",
    )(hid, w3, w3, b3, b3, deltas)
    pq = gmm(fwd, hsd)
    return pq, nrm, nyq


def _dft_fwd_kernel(f_ref, z_ref, pq_ref, nrm_ref, nyq_ref, uv_ref, acc_ref, *, length, tm):
    i = pl.program_id(1)
    k = pl.program_id(3)

    @pl.when(k == 0)
    def _():
        acc_ref[...] = jnp.zeros_like(acc_ref)

    z = z_ref[...]
    acc_ref[0] += jnp.dot(f_ref[0], z, preferred_element_type=F32)
    acc_ref[1] += jnp.dot(f_ref[1], z, preferred_element_type=F32)

    @pl.when(k == pl.num_programs(3) - 1)
    def _():
        k_idx = lax.broadcasted_iota(jnp.int32, (tm, 1), 0) + i * tm
        is0 = k_idx == 0
        wk = jnp.where(is0, 1.0, 2.0) * (0.5 / length) / nrm_ref[...]
        a, b, pp, qq = acc_ref[0], acc_ref[1], pq_ref[0], pq_ref[1]
        uv_ref[0, 0] = (wk * (a * pp - jnp.where(is0, 0.0, b * qq))).astype(uv_ref.dtype)
        uv_ref[0, 1] = (wk * jnp.where(is0, b * nyq_ref[...], a * qq + b * pp)).astype(uv_ref.dtype)


def dft_fwd(fwd, z, pq, nrm, nyq, order, row0, n_seq, length):
    tm, tn, tk = _pick(length, 1024), 512, _pick(length, 1024)
    nk = length // tk
    nj = HY_W // tn
    r0 = row0 // tk
    vec = pl.BlockSpec((1, tn), lambda b, i, j, k: (0, order * nj + j))
    uv = pl.pallas_call(
        functools.partial(_dft_fwd_kernel, length=length, tm=tm),
        grid=(n_seq, length // tm, nj, nk),
        in_specs=[pl.BlockSpec((2, tm, tk), lambda b, i, j, k: (0, i, k)),
                  pl.BlockSpec((tk, tn), lambda b, i, j, k: (r0 + b * nk + k, j)),
                  pl.BlockSpec((2, tm, tn), lambda b, i, j, k: (0, i, order * nj + j)),
                  vec, vec],
        out_specs=pl.BlockSpec((1, 2, tm, tn), lambda b, i, j, k: (b, 0, i, j)),
        out_shape=jax.ShapeDtypeStruct((n_seq, 2, length, HY_W), BF16),
        scratch_shapes=[pltpu.VMEM((2, tm, tn), F32)],
        compiler_params=_cparams("parallel", "parallel", "parallel", "arbitrary"),
        name="hyena_dft_fwd",
    )(fwd, z, pq, nrm, nyq)
    return uv.reshape(n_seq, 2 * length, HY_W)


def _hy_inv_kernel(*refs, n_out):
    f_ref, uv_ref, z_ref, gate_ref, bias_ref = refs[:5]
    acc_ref = refs[-1]
    o_refs = refs[-1 - n_out:-1]
    k = pl.program_id(3)

    @pl.when(k == 0)
    def _():
        acc_ref[...] = jnp.zeros_like(acc_ref)

    acc_ref[...] += jnp.dot(f_ref[0], uv_ref[0], preferred_element_type=F32)

    @pl.when(k == pl.num_programs(3) - 1)
    def _():
        y = gate_ref[...] * (acc_ref[...] + bias_ref[...] * z_ref[...])
        for o_ref in o_refs:
            o_ref[...] = y.astype(o_ref.dtype)


def hy_inverse(inv, uv, z, zcol0, gate, gcol0, bias, row0, length, out_dtypes, prev=None):
    t = z.shape[0]
    n_seq = uv.shape[0]
    tm, tn, tk = _pick(length, 1024), 1024, _pick(length, 1024)
    r0 = row0 // tm
    ni = length // tm
    kp = length // tk
    rmap = lambda c0: (lambda b, i, j, k: (r0 + b * ni + i, c0 // tn + j))
    in_specs = [pl.BlockSpec((1, tm, tk), lambda b, i, j, k: (k // kp, i, k % kp)),
                pl.BlockSpec((1, tk, tn), lambda b, i, j, k: (b, k, j)),
                pl.BlockSpec((tm, tn), rmap(zcol0)),
                pl.BlockSpec((tm, tn), rmap(gcol0)),
                pl.BlockSpec((1, tn), lambda b, i, j, k: (0, j))]
    args = [inv, uv, z, gate, bias.reshape(1, HY_W)]
    aliases = {}
    if prev is not None:
        for n, pv in enumerate(prev):
            in_specs.append(pl.BlockSpec(memory_space=pl.ANY))
            args.append(pv)
            aliases[5 + n] = n
    return pl.pallas_call(
        functools.partial(_hy_inv_kernel, n_out=len(out_dtypes)),
        grid=(n_seq, ni, HY_W // tn, 2 * kp),
        in_specs=in_specs,
        out_specs=[pl.BlockSpec((tm, tn), rmap(0)) for _ in out_dtypes],
        out_shape=[jax.ShapeDtypeStruct((t, HY_W), dt) for dt in out_dtypes],
        scratch_shapes=[pltpu.VMEM((tm, tn), F32)],
        input_output_aliases=aliases,
        compiler_params=_cparams("parallel", "parallel", "parallel", "arbitrary"),
        name="hyena_dft_inv",
    )(*args)


def hyena_group(v32, v16, x12, filt, mats, p, row0, n_seq, length, prev):
    fwd, inv = mats
    pq, nrm, nyq = filt
    prev1, prev2 = prev
    uv = dft_fwd(fwd, v16, pq, nrm, nyq, 0, row0, n_seq, length)
    z1 = hy_inverse(inv, uv, v32, 0, x12, 0, p['hy_bias'][0], row0, length, (F32, BF16), prev1)
    uv = dft_fwd(fwd, z1[1], pq, nrm, nyq, 1, row0, n_seq, length)
    z2 = hy_inverse(inv, uv, z1[0], 0, x12, HY_W, p['hy_bias'][1], row0, length, (BF16,), prev2)
    return z1, z2


def _merge_kernel(a_ref, h_ref, s_ref, wa_ref, wh_ref, ws_ref, ga_ref, gh_ref, gs_ref, o_ref):
    acc = jax.nn.sigmoid(ga_ref[...]) * jnp.dot(a_ref[...], wa_ref[...], preferred_element_type=F32)
    acc += jax.nn.sigmoid(gh_ref[...]) * jnp.dot(h_ref[...], wh_ref[...], preferred_element_type=F32)
    acc += jax.nn.sigmoid(gs_ref[...]) * jnp.dot(s_ref[...], ws_ref[...], preferred_element_type=F32)
    o_ref[...] = acc.astype(o_ref.dtype)


def branch_merge(att, hy, ssm, wa, wh, ws, gate_logits, tm=512, tn=512):
    t, kd = att.shape
    d = wa.shape[1]
    nj = d // tn
    xs = pl.BlockSpec((tm, kd), lambda j, i: (i, 0))
    ws_ = pl.BlockSpec((kd, tn), lambda j, i: (0, j))
    gs = lambda b: pl.BlockSpec((tm, tn), lambda j, i: (i, b * nj + j))
    return pl.pallas_call(
        _merge_kernel,
        grid=(nj, t // tm),
        in_specs=[xs, xs, xs, ws_, ws_, ws_, gs(0), gs(1), gs(2)],
        out_specs=pl.BlockSpec((tm, tn), lambda j, i: (i, j)),
        out_shape=jax.ShapeDtypeStruct((t, d), BF16),
        compiler_params=_cparams("parallel", "parallel"),
        name="branch_merge",
    )(att, hy, ssm, wa, wh, ws, gate_logits, gate_logits, gate_logits)


def _swiglu_kernel(x_ref, wg_ref, wu_ref, o_ref):
    x = x_ref[0]
    g = jnp.dot(x, wg_ref[0].astype(BF16), preferred_element_type=F32)
    u = jnp.dot(x, wu_ref[0].astype(BF16), preferred_element_type=F32)
    o_ref[0] = (g * jax.nn.sigmoid(g) * u).astype(o_ref.dtype)


def expert_swiglu(xs, w_gate, w_up, layer, tn=256):
    e, m, d = xs.shape
    f = w_gate.shape[3]
    wspec = pl.BlockSpec((None, 1, d, tn), lambda e, j: (layer, e, 0, j))
    return pl.pallas_call(
        _swiglu_kernel,
        grid=(e, f // tn),
        in_specs=[pl.BlockSpec((1, m, d), lambda e, j: (e, 0, 0)), wspec, wspec],
        out_specs=pl.BlockSpec((1, m, tn), lambda e, j: (e, 0, j)),
        out_shape=jax.ShapeDtypeStruct((e, m, f), BF16),
        compiler_params=_cparams("parallel", "parallel"),
        name="expert_swiglu",
    )(xs, w_gate, w_up)


def _down_kernel(h_ref, w_ref, g_ref, o_ref):
    o_ref[0] = jnp.dot(h_ref[0], w_ref[0].astype(BF16), preferred_element_type=F32) * g_ref[0]


def expert_down(hid, w_down, gates, layer, tn=512):
    e, m, f = hid.shape
    d = w_down.shape[3]
    return pl.pallas_call(
        _down_kernel,
        grid=(e, d // tn),
        in_specs=[pl.BlockSpec((1, m, f), lambda e, j: (e, 0, 0)),
                  pl.BlockSpec((None, 1, f, tn), lambda e, j: (layer, e, 0, j)),
                  pl.BlockSpec((1, m, 1), lambda e, j: (e, 0, 0))],
        out_specs=pl.BlockSpec((1, m, tn), lambda e, j: (e, 0, j)),
        out_shape=jax.ShapeDtypeStruct((e, m, d), F32),
        compiler_params=_cparams("parallel", "parallel"),
        name="expert_down",
    )(hid, w_down, gates)


def ec_moe(xm, p, lay):
    t, d = xm.shape
    w_r = jnp.zeros((d, LANES), BF16).at[:, :N_EXPERTS].set(p['w_router'].astype(BF16))
    logits = mm(xm, w_r, tn=LANES)[:, :N_EXPERTS]
    aff = jax.nn.softmax(logits, axis=-1)
    gates, rows = [], []
    for row0, n_seq, length in ((0, lay.n_p, lay.l_p), (lay.tp, lay.n_s, lay.l_s)):
        cap = EC_CAPACITY * length // N_EXPERTS
        a = aff[row0:row0 + n_seq * length].reshape(n_seq, length, N_EXPERTS)
        g, idx = lax.top_k(jnp.swapaxes(a, 1, 2), cap)
        idx = idx + (row0 + jnp.arange(n_seq, dtype=idx.dtype) * length)[:, None, None]
        gates.append(jnp.swapaxes(g, 0, 1).reshape(N_EXPERTS, n_seq * cap))
        rows.append(jnp.swapaxes(idx, 0, 1).reshape(N_EXPERTS, n_seq * cap))
    gates = jnp.concatenate(gates, axis=1)
    rows = jnp.concatenate(rows, axis=1)
    xs = jnp.take(xm, rows.reshape(-1), axis=0).reshape(N_EXPERTS, -1, d)
    hid = expert_swiglu(xs, p['w_gate'], p['w_up'], p['layer'])
    y = expert_down(hid, p['w_down'], gates[..., None], p['layer'])
    return jnp.zeros((t, d), F32).at[rows.reshape(-1)].add(y.reshape(-1, d))


def rope_tables(lay):
    pos = jnp.arange(lay.l_s)
    row = (pos // GRID_W).astype(F32)
    col = (pos % GRID_W).astype(F32)
    inv = ROPE_THETA ** (-jnp.arange(ROT_FREQS, dtype=F32) / ROT_FREQS)
    ang = jnp.concatenate([row[:, None] * inv] * 2 + [col[:, None] * inv] * 2, axis=1)
    sign = jnp.where((jnp.arange(HEAD_DIM) % (2 * ROT_FREQS)) < ROT_FREQS, -1.0, 1.0).astype(F32)
    cos_s, sin_s = jnp.cos(ang), jnp.sin(ang) * sign
    cos_t = jnp.concatenate([jnp.ones((lay.tp, HEAD_DIM), F32)] + [cos_s] * lay.n_s, axis=0)
    sin_t = jnp.concatenate([jnp.zeros((lay.tp, HEAD_DIM), F32)] + [sin_s] * lay.n_s, axis=0)
    return cos_t, sin_t


def trunk_layer(x, xm, p, mod, mod_next, consts, lay, cache_k, cache_v, state_ssm):
    cos_t, sin_t, mats_p, mats_s = consts
    tp = lay.tp
    u = mm(xm, p['w_in_main'])
    dt_raw = mm(xm, p['w_in_dt'], tn=LANES)
    gate_logits = mm(xm, p['w_in_gate'])

    q, kr, vb, kf, vf = qkv_prep(u, cos_t, sin_t, p['q_norm'], p['k_norm'])
    k_p = kr[:tp].reshape(lay.n_p, lay.l_p, KV_W)
    v_p = vb[:tp].reshape(lay.n_p, lay.l_p, KV_W)
    k_s = jnp.concatenate([cache_k.reshape(lay.n_s, -1, KV_W).astype(BF16),
                           kr[tp:].reshape(lay.n_s, lay.l_s, KV_W)], axis=1)
    v_s = jnp.concatenate([cache_v.reshape(lay.n_s, -1, KV_W).astype(BF16),
                           vb[tp:].reshape(lay.n_s, lay.l_s, KV_W)], axis=1)
    att = attention(q, k_p, v_p, 0, lay.n_p, lay.l_p, lay.l_p)
    att = attention(q, k_s, v_s, tp, lay.n_s, lay.l_s, ROWS, prev=att)

    v32, v16 = conv3(u, HY_OFF, HY_W, p['hy_conv_w'][:, :HY_W], p['hy_conv_b'][:HY_W], lay, silu=False,
                     out_dtypes=(F32, BF16))
    x12 = conv3(u, HY_OFF + HY_W, 2 * HY_W, p['hy_conv_w'][:, HY_W:], p['hy_conv_b'][HY_W:], lay, silu=False)
    filt_p = hyena_filters(lay.l_p, p, mats_p[0])
    filt_s = hyena_filters(lay.l_s, p, mats_s[0])
    z1, hy = hyena_group(v32, v16, x12, filt_p, mats_p, p, 0, lay.n_p, lay.l_p, (None, None))
    _, hy = hyena_group(v32, v16, x12, filt_s, mats_s, p, tp, lay.n_s, lay.l_s, (z1, hy))
    hy = hy[0]

    xbc = conv3(u, XBC_OFF, SSM_CONV_DIM, p['ssm_conv_w'], p['ssm_conv_b'], lay, silu=True)
    dtc = dt_prep(dt_raw, p['ssm_dt_bias'], p['ssm_a_log'])
    dtr = dtc.T
    zero_state = jnp.zeros((lay.n_p, 2, SSM_HEADS, SSM_HEADDIM, SSM_STATE), F32)
    xst = xbc[:, :SSM_W].T
    y2, states = ssd(xbc, xst, dtc, dtr, zero_state, 0, lay.n_p, lay.l_p)
    y2, _ = ssd(xbc, xst, dtc, dtr, state_ssm, tp, lay.n_s, lay.l_s, prev=y2)
    ssm = ssd_gate(y2, xbc, u, p['ssm_d'], p['ssm_norm'])

    merged = branch_merge(att, hy, ssm, p['w_br_att'], p['w_br_hy'], p['w_br_ssm'], gate_logits)
    m = mm(merged, p['w_out'])
    x1, xm2 = ln_mod(x, m, mod, 2, p['ln1_g'], p['ln1_b'], lay, mod_next=mod, sec_sc=4, sec_sh=3)
    f = ec_moe(xm2, p, lay)
    x2, xm_next = ln_mod(x1, f, mod, 5, p['ln2_g'], p['ln2_b'], lay, mod_next=mod_next, sec_sc=1, sec_sh=0)
    new_k = kf[:tp].reshape(lay.n_p, lay.l_p, N_KV_HEADS, HEAD_DIM)
    new_v = vf[:tp].reshape(lay.n_p, lay.l_p, N_KV_HEADS, HEAD_DIM)
    return x2, xm_next, (new_k, new_v, states)


def kernel(x_prompt, x_sample, cache_k, cache_v, state_ssm, c, c_ctx, w_mod, b_mod, w_in, q_norm, k_norm, hy_conv_w, hy_conv_b, hy_w1, hy_b1, hy_freq, hy_w2, hy_b2, hy_w3, hy_b3, hy_bias, ssm_conv_w, ssm_conv_b, ssm_dt_bias, ssm_a_log, ssm_d, ssm_norm, w_br_att, w_br_hy, w_br_ssm, w_out, ln1_g, ln1_b, w_router, w_gate, w_up, w_down, ln2_g, ln2_b):
    n_p, l_p, d = x_prompt.shape
    n_s, l_s, _ = x_sample.shape
    depth = w_in.shape[0]
    lay = Layout(n_p, l_p, n_s, l_s)
    x = jnp.concatenate([x_prompt.reshape(lay.tp, d), x_sample.reshape(lay.ts, d)], axis=0)

    cond = jnp.zeros((N_COND_PAD, d), F32).at[0].set(c_ctx).at[1:1 + n_s].set(c)
    act = (cond * jax.nn.sigmoid(cond)).astype(BF16)
    mod_all = gmm(act[None], w_mod, tm=N_COND_PAD, tn=2048, tk=1024, share_x=True) + b_mod[:, None, :]
    mods = [mod_all[l].reshape(N_COND_PAD, 1, 6 * d) for l in range(depth)]

    consts = rope_tables(lay) + (dft_matrices(l_p), dft_matrices(l_s))
    xm = modulate(x, mods[0], 1, 0, lay)
    new_k, new_v, new_s = [], [], []
    for l in range(depth):
        p = dict(w_in_main=w_in[l, :, :DT_OFF].astype(BF16),
                 w_in_dt=w_in[l, :, DT_OFF:DT_OFF + LANES].astype(BF16),
                 w_in_gate=w_in[l, :, GATE_OFF:].astype(BF16),
                 q_norm=q_norm[l], k_norm=k_norm[l],
                 hy_conv_w=hy_conv_w[l], hy_conv_b=hy_conv_b[l], hy_w1=hy_w1[l], hy_b1=hy_b1[l],
                 hy_freq=hy_freq[l], hy_w2=hy_w2[l], hy_b2=hy_b2[l], hy_w3=hy_w3[l], hy_b3=hy_b3[l],
                 hy_bias=hy_bias[l], ssm_conv_w=ssm_conv_w[l], ssm_conv_b=ssm_conv_b[l],
                 ssm_dt_bias=ssm_dt_bias[l], ssm_a_log=ssm_a_log[l], ssm_d=ssm_d[l], ssm_norm=ssm_norm[l],
                 w_br_att=w_br_att[l].astype(BF16), w_br_hy=w_br_hy[l].astype(BF16),
                 w_br_ssm=w_br_ssm[l].astype(BF16), w_out=w_out[l].astype(BF16),
                 ln1_g=ln1_g[l], ln1_b=ln1_b[l], w_router=w_router[l],
                 w_gate=w_gate, w_up=w_up, w_down=w_down, layer=l,
                 ln2_g=ln2_g[l], ln2_b=ln2_b[l])
        mod_next = mods[l + 1] if l + 1 < depth else None
        x, xm, (k_l, v_l, s_l) = trunk_layer(x, xm, p, mods[l], mod_next, consts, lay,
                                             cache_k[:, l], cache_v[:, l], state_ssm[:, l])
        new_k.append(k_l)
        new_v.append(v_l)
        new_s.append(s_l)
    y_prompt = x[:lay.tp].reshape(n_p, l_p, d)
    y_sample = x[lay.tp:].reshape(n_s, l_s, d)
    return (y_prompt, y_sample, jnp.stack(new_k, axis=1), jnp.stack(new_v, axis=1), jnp.stack(new_s, axis=1))
```

```python
import functools
import math

import jax
import jax.numpy as jnp
from jax import lax
from jax.experimental import pallas as pl
from jax.experimental.pallas import tpu as pltpu

F32 = jnp.float32
BF16 = jnp.bfloat16

D_MODEL = 4096
DEPTH = 2
GRID_W = 64
N_HEADS = 16
N_KV_HEADS = 4
KV_REP = N_HEADS // N_KV_HEADS
HEAD_DIM = 128
ATT_W = N_HEADS * HEAD_DIM
KV_W = N_KV_HEADS * HEAD_DIM
ROT_FREQS = HEAD_DIM // 4
ROPE_THETA = 10000.0
HY_W = 2048
HY_ORDER = 2
HY_BANDS = 16
HY_DECAY_TARGET = 1e-2
HY_FAST_DECAY = 0.3
HY_SLOW_DECAY = 1.5
SSM_W = 2048
SSM_HEADDIM = 64
SSM_HEADS = SSM_W // SSM_HEADDIM
SSM_GROUPS = 8
SSM_REP = SSM_HEADS // SSM_GROUPS
SSM_STATE = 128
SSM_CHUNK = 128
SSM_CONV_DIM = SSM_W + 2 * SSM_GROUPS * SSM_STATE
N_EXPERTS = 16
EC_CAPACITY = 2
MOE_FF = 2048
N_BRANCH = 3
Q_OFF = 0
K_OFF = Q_OFF + ATT_W
V_OFF = K_OFF + KV_W
HY_OFF = V_OFF + KV_W
Z_OFF = HY_OFF + 3 * HY_W
XBC_OFF = Z_OFF + SSM_W
DT_OFF = XBC_OFF + SSM_CONV_DIM
GATE_OFF = DT_OFF + 2 * SSM_HEADS
ALPHA = (2 * DEPTH) ** 0.25
LN_EPS = 1e-5
RMS_EPS = 1e-6
N_COND_PAD = 8
LANES = 128
SUBLANES = 8

VMEM_LIMIT_BYTES = 56 * 1024 * 1024

NT_DIMS = (((1,), (1,)), ((), ()))
TN_DIMS = (((0,), (0,)), ((), ()))


def _cparams(*sem):
    return pltpu.CompilerParams(dimension_semantics=sem, vmem_limit_bytes=VMEM_LIMIT_BYTES)


def _pick(dim, pref):
    t = min(dim, pref)
    while dim % t:
        t //= 2
    return t


def _mm_kernel(x_ref, w_ref, o_ref, acc_ref):
    k = pl.program_id(3)

    @pl.when(k == 0)
    def _():
        acc_ref[...] = jnp.zeros_like(acc_ref)

    acc_ref[...] += jnp.dot(x_ref[0].astype(BF16), w_ref[0].astype(BF16), preferred_element_type=F32)

    @pl.when(k == pl.num_programs(3) - 1)
    def _():
        o_ref[0] = acc_ref[...].astype(o_ref.dtype)


def _mm_fullk_kernel(x_ref, w_ref, o_ref):
    o_ref[0] = jnp.dot(x_ref[0].astype(BF16), w_ref[0].astype(BF16),
                       preferred_element_type=F32).astype(o_ref.dtype)


def gmm(x, w, out_dtype=F32, tm=1024, tn=512, tk=4096, share_x=False):
    g, kd, n = w.shape
    m = x.shape[1]
    tm, tn, tk = _pick(m, tm), _pick(n, tn), _pick(kd, tk)
    if tk == kd:
        xmap = (lambda e, i, j: (0, i, 0)) if share_x else (lambda e, i, j: (e, i, 0))
        return pl.pallas_call(
            _mm_fullk_kernel,
            grid=(g, m // tm, n // tn),
            in_specs=[pl.BlockSpec((1, tm, kd), xmap),
                      pl.BlockSpec((1, kd, tn), lambda e, i, j: (e, 0, j))],
            out_specs=pl.BlockSpec((1, tm, tn), lambda e, i, j: (e, i, j)),
            out_shape=jax.ShapeDtypeStruct((g, m, n), out_dtype),
            compiler_params=_cparams("parallel", "parallel", "parallel"),
            name="gmm",
        )(x, w)
    xmap = (lambda e, i, j, k: (0, i, k)) if share_x else (lambda e, i, j, k: (e, i, k))
    return pl.pallas_call(
        _mm_kernel,
        grid=(g, m // tm, n // tn, kd // tk),
        in_specs=[pl.BlockSpec((1, tm, tk), xmap),
                  pl.BlockSpec((1, tk, tn), lambda e, i, j, k: (e, k, j))],
        out_specs=pl.BlockSpec((1, tm, tn), lambda e, i, j, k: (e, i, j)),
        out_shape=jax.ShapeDtypeStruct((g, m, n), out_dtype),
        scratch_shapes=[pltpu.VMEM((tm, tn), F32)],
        compiler_params=_cparams("parallel", "parallel", "parallel", "arbitrary"),
        name="gmm",
    )(x, w)


def mm(x, w, out_dtype=F32, **kw):
    return gmm(x[None], w[None], out_dtype, **kw)[0]


def _mm_act_kernel(x_ref, w_ref, o_ref, *, sigmoid):
    y = jnp.dot(x_ref[...], w_ref[...].astype(BF16), preferred_element_type=F32)
    if sigmoid:
        y = jax.nn.sigmoid(y)
    o_ref[...] = y.astype(o_ref.dtype)


def mm_cols(x, w, layer, col0, n_cols, out_dtype=F32, sigmoid=False, tm=2048, tn=256):
    m, kd = x.shape
    tm, tn = _pick(m, tm), _pick(n_cols, tn)
    c0 = col0 // tn
    if layer is None:
        wspec = pl.BlockSpec((kd, tn), lambda i, j: (0, c0 + j))
    else:
        wspec = pl.BlockSpec((None, kd, tn), lambda i, j: (layer, 0, c0 + j))
    return pl.pallas_call(
        functools.partial(_mm_act_kernel, sigmoid=sigmoid),
        grid=(m // tm, n_cols // tn),
        in_specs=[pl.BlockSpec((tm, kd), lambda i, j: (i, 0)), wspec],
        out_specs=pl.BlockSpec((tm, tn), lambda i, j: (i, j)),
        out_shape=jax.ShapeDtypeStruct((m, n_cols), out_dtype),
        compiler_params=_cparams("parallel", "parallel"),
        name="mm_cols",
    )(x, w)


class Layout:
    def __init__(self, n_p, l_p, n_s, l_s):
        self.n_p, self.l_p, self.n_s, self.l_s = n_p, l_p, n_s, l_s
        self.tp = n_p * l_p
        self.ts = n_s * l_s
        self.t = self.tp + self.ts

    def group_of_block(self, i, rows):
        bp = self.tp // rows
        return jnp.where(i < bp, 0, 1 + (i - bp) // (self.l_s // rows))

    def seq_edges(self, i, rows):
        bp = self.tp // rows
        per_p, per_s = self.l_p // rows, self.l_s // rows
        first = jnp.where(i < bp, i % per_p == 0, (i - bp) % per_s == 0)
        last = jnp.where(i < bp, i % per_p == per_p - 1, (i - bp) % per_s == per_s - 1)
        return first, last


ROWS = 256


def _modulate_kernel(x_ref, sc_ref, sh_ref, o_ref):
    o_ref[...] = (x_ref[...] * (1.0 + sc_ref[0]) + sh_ref[0]).astype(o_ref.dtype)


def modulate(x, mod, sec_sc, sec_sh, lay):
    t, d = x.shape
    grp = lambda i: lay.group_of_block(i, ROWS)
    return pl.pallas_call(
        _modulate_kernel,
        grid=(t // ROWS,),
        in_specs=[pl.BlockSpec((ROWS, d), lambda i: (i, 0)),
                  pl.BlockSpec((1, 1, d), lambda i: (grp(i), 0, sec_sc)),
                  pl.BlockSpec((1, 1, d), lambda i: (grp(i), 0, sec_sh))],
        out_specs=pl.BlockSpec((ROWS, d), lambda i: (i, 0)),
        out_shape=jax.ShapeDtypeStruct((t, d), BF16),
        compiler_params=_cparams("parallel"),
        name="modulate",
    )(x, mod, mod)


def _ln_mod_kernel(*refs, with_mod):
    if with_mod:
        x_ref, m_ref, gt_ref, lg_ref, lb_ref, sc_ref, sh_ref, o_ref, om_ref = refs
    else:
        x_ref, m_ref, gt_ref, lg_ref, lb_ref, o_ref = refs
    r = ALPHA * x_ref[...] + gt_ref[0] * m_ref[...]
    mu = jnp.mean(r, axis=-1, keepdims=True)
    dlt = r - mu
    var = jnp.mean(dlt * dlt, axis=-1, keepdims=True)
    y = dlt * lax.rsqrt(var + LN_EPS) * lg_ref[...] + lb_ref[...]
    o_ref[...] = y
    if with_mod:
        om_ref[...] = (y * (1.0 + sc_ref[0]) + sh_ref[0]).astype(om_ref.dtype)


def ln_mod(x, m, mod, sec_gate, ln_g, ln_b, lay, mod_next=None, sec_sc=0, sec_sh=0):
    t, d = x.shape
    rows = ROWS // 2
    grp = lambda i: lay.group_of_block(i, rows)
    with_mod = mod_next is not None
    row_spec = pl.BlockSpec((rows, d), lambda i: (i, 0))
    vec_spec = pl.BlockSpec((1, d), lambda i: (0, 0))
    in_specs = [row_spec, row_spec,
                pl.BlockSpec((1, 1, d), lambda i: (grp(i), 0, sec_gate)), vec_spec, vec_spec]
    args = [x, m, mod, ln_g.reshape(1, d), ln_b.reshape(1, d)]
    out_specs = [row_spec]
    out_shape = [jax.ShapeDtypeStruct((t, d), F32)]
    if with_mod:
        in_specs += [pl.BlockSpec((1, 1, d), lambda i: (grp(i), 0, sec_sc)),
                     pl.BlockSpec((1, 1, d), lambda i: (grp(i), 0, sec_sh))]
        args += [mod_next, mod_next]
        out_specs.append(row_spec)
        out_shape.append(jax.ShapeDtypeStruct((t, d), BF16))
    res = pl.pallas_call(
        functools.partial(_ln_mod_kernel, with_mod=with_mod),
        grid=(t // rows,),
        in_specs=in_specs, out_specs=out_specs, out_shape=out_shape,
        compiler_params=_cparams("parallel"),
        name="ln_mod",
    )(*args)
    return (res[0], res[1]) if with_mod else (res[0], None)


def _qkv_prep_kernel(u_ref, cos_ref, sin_ref, qn_ref, kn_ref, q_ref, kr_ref, vb_ref, kf_ref, vf_ref):
    cos = cos_ref[...]
    sin = sin_ref[...]
    lane = lax.broadcasted_iota(jnp.int32, cos.shape, 1)
    lane_lo = (lane % (2 * ROT_FREQS)) < ROT_FREQS

    def norm(x, g):
        return x * lax.rsqrt(jnp.mean(x * x, axis=-1, keepdims=True) + RMS_EPS) * g

    def rope(x):
        sw = jnp.where(lane_lo, pltpu.roll(x, LANES - ROT_FREQS, 1), pltpu.roll(x, ROT_FREQS, 1))
        return x * cos + sw * sin

    for h in range(N_HEADS):
        sl = slice(Q_OFF + h * HEAD_DIM, Q_OFF + (h + 1) * HEAD_DIM)
        q_ref[:, h * HEAD_DIM:(h + 1) * HEAD_DIM] = rope(norm(u_ref[:, sl], qn_ref[...])).astype(q_ref.dtype)
    for h in range(N_KV_HEADS):
        o = slice(h * HEAD_DIM, (h + 1) * HEAD_DIM)
        kk = norm(u_ref[:, K_OFF + h * HEAD_DIM:K_OFF + (h + 1) * HEAD_DIM], kn_ref[...])
        kf_ref[:, o] = kk
        kr_ref[:, o] = rope(kk).astype(kr_ref.dtype)
        vv = u_ref[:, V_OFF + h * HEAD_DIM:V_OFF + (h + 1) * HEAD_DIM]
        vf_ref[:, o] = vv
        vb_ref[:, o] = vv.astype(vb_ref.dtype)


def qkv_prep(u, cos_t, sin_t, q_norm, k_norm):
    t = u.shape[0]
    row = lambda w: pl.BlockSpec((ROWS, w), lambda i: (i, 0))
    vec = pl.BlockSpec((1, HEAD_DIM), lambda i: (0, 0))
    return pl.pallas_call(
        _qkv_prep_kernel,
        grid=(t // ROWS,),
        in_specs=[row(HY_OFF), row(HEAD_DIM), row(HEAD_DIM), vec, vec],
        out_specs=[row(ATT_W), row(KV_W), row(KV_W), row(KV_W), row(KV_W)],
        out_shape=[jax.ShapeDtypeStruct((t, ATT_W), BF16), jax.ShapeDtypeStruct((t, KV_W), BF16),
                   jax.ShapeDtypeStruct((t, KV_W), BF16), jax.ShapeDtypeStruct((t, KV_W), F32),
                   jax.ShapeDtypeStruct((t, KV_W), F32)],
        compiler_params=_cparams("parallel"),
        name="qkv_prep",
    )(u, cos_t, sin_t, q_norm.reshape(1, HEAD_DIM), k_norm.reshape(1, HEAD_DIM))


def _attn_kernel(*refs, aliased):
    q_ref, k_ref, v_ref = refs[:3]
    o_ref = refs[-1]
    scale = HEAD_DIM ** -0.5
    k = k_ref[0]
    v = v_ref[0]
    for r in range(KV_REP):
        sl = slice(r * HEAD_DIM, (r + 1) * HEAD_DIM)
        s = lax.dot_general(q_ref[:, sl], k, NT_DIMS, preferred_element_type=F32)
        m = jnp.max(s, axis=1, keepdims=True)
        p = jnp.exp((s - m) * scale)
        l = jnp.sum(p, axis=1, keepdims=True)
        o = jnp.dot(p.astype(BF16), v, preferred_element_type=F32)
        o_ref[:, sl] = (o / l).astype(o_ref.dtype)


def attention(q, k, v, row0, n_seq, l_q, tq, prev=None):
    t = q.shape[0]
    l_k = k.shape[1]
    gw = KV_REP * HEAD_DIM
    nq = l_q // tq
    b0 = row0 // tq
    qmap = lambda b, g, i: (b0 + b * nq + i, g)
    in_specs = [pl.BlockSpec((tq, gw), qmap),
                pl.BlockSpec((1, l_k, HEAD_DIM), lambda b, g, i: (b, 0, g)),
                pl.BlockSpec((1, l_k, HEAD_DIM), lambda b, g, i: (b, 0, g))]
    args = [q, k, v]
    aliases = {}
    if prev is not None:
        in_specs.append(pl.BlockSpec(memory_space=pl.ANY))
        args.append(prev)
        aliases = {3: 0}
    return pl.pallas_call(
        functools.partial(_attn_kernel, aliased=prev is not None),
        grid=(n_seq, N_KV_HEADS, nq),
        in_specs=in_specs,
        out_specs=pl.BlockSpec((tq, gw), qmap),
        out_shape=jax.ShapeDtypeStruct((t, ATT_W), BF16),
        input_output_aliases=aliases,
        compiler_params=_cparams("parallel", "parallel", "parallel"),
        name="attention",
    )(*args)


CONV_ROWS = 1024


def _conv3_kernel(x_ref, prev_ref, next_ref, w_ref, b_ref, *o_refs, lay, silu):
    i = pl.program_id(0)
    x = x_ref[...]
    rows = lax.broadcasted_iota(jnp.int32, (CONV_ROWS, 1), 0)
    grow = rows + i * CONV_ROWS
    in_p = grow < lay.tp
    pos = jnp.where(in_p, grow & (lay.l_p - 1), (grow - lay.tp) & (lay.l_s - 1))
    last = jnp.where(in_p, lay.l_p - 1, lay.l_s - 1)
    xm1 = jnp.where(rows == 0, prev_ref[SUBLANES - 1:SUBLANES, :], pltpu.roll(x, 1, 0))
    xp1 = jnp.where(rows == CONV_ROWS - 1, next_ref[0:1, :], pltpu.roll(x, CONV_ROWS - 1, 0))
    xm1 = jnp.where(pos == 0, 0.0, xm1)
    xp1 = jnp.where(pos == last, 0.0, xp1)
    y = xm1 * w_ref[0:1, :] + x * w_ref[1:2, :] + xp1 * w_ref[2:3, :] + b_ref[...]
    if silu:
        y = y * jax.nn.sigmoid(y)
    for o_ref in o_refs:
        o_ref[...] = y.astype(o_ref.dtype)


def conv3(u, col0, width, w, b, lay, silu, out_dtypes=(F32,), tc=1024):
    t = u.shape[0]
    assert lay.l_p & (lay.l_p - 1) == 0 and lay.l_s & (lay.l_s - 1) == 0
    c0 = col0 // tc
    sub = CONV_ROWS // SUBLANES
    n_sub = t // SUBLANES
    res = pl.pallas_call(
        functools.partial(_conv3_kernel, lay=lay, silu=silu),
        grid=(t // CONV_ROWS, width // tc),
        in_specs=[pl.BlockSpec((CONV_ROWS, tc), lambda i, j: (i, c0 + j)),
                  pl.BlockSpec((SUBLANES, tc), lambda i, j: (jnp.maximum(i * sub - 1, 0), c0 + j)),
                  pl.BlockSpec((SUBLANES, tc), lambda i, j: (jnp.minimum((i + 1) * sub, n_sub - 1), c0 + j)),
                  pl.BlockSpec((3, tc), lambda i, j: (0, j)),
                  pl.BlockSpec((1, tc), lambda i, j: (0, j))],
        out_specs=[pl.BlockSpec((CONV_ROWS, tc), lambda i, j: (i, j)) for _ in out_dtypes],
        out_shape=[jax.ShapeDtypeStruct((t, width), dt) for dt in out_dtypes],
        compiler_params=_cparams("parallel", "parallel"),
        name="conv3",
    )(u, u, u, w, b.reshape(1, width))
    return res if len(out_dtypes) > 1 else res[0]


def _dt_prep_kernel(raw_ref, bias_ref, a_ref, o_ref):
    x = raw_ref[...] + bias_ref[...]
    dt = jnp.maximum(x, 0.0) + jnp.log1p(jnp.exp(-jnp.abs(x)))
    o_ref[...] = jnp.where(lax.broadcasted_iota(jnp.int32, x.shape, 1) < 2 * SSM_HEADS,
                           dt, pltpu.roll(dt, 2 * SSM_HEADS, 1) * a_ref[...])


def dt_prep(raw, dt_bias, a_log):
    t = raw.shape[0]
    nh2 = 2 * SSM_HEADS
    bias = jnp.zeros((1, LANES), F32).at[0, :nh2].set(dt_bias.reshape(-1))
    a = jnp.zeros((1, LANES), F32).at[0, nh2:2 * nh2].set(-jnp.exp(a_log.reshape(-1)))
    return pl.pallas_call(
        _dt_prep_kernel,
        grid=(t // ROWS,),
        in_specs=[pl.BlockSpec((ROWS, LANES), lambda i: (i, 0)),
                  pl.BlockSpec((1, LANES), lambda i: (0, 0)),
                  pl.BlockSpec((1, LANES), lambda i: (0, 0))],
        out_specs=pl.BlockSpec((ROWS, LANES), lambda i: (i, 0)),
        out_shape=jax.ShapeDtypeStruct((t, LANES), F32),
        compiler_params=_cparams("parallel"),
        name="dt_prep",
    )(raw, bias, a)


def _prefix_sum(x, axis):
    idx = lax.broadcasted_iota(jnp.int32, x.shape, axis)
    d = 1
    while d < SSM_CHUNK:
        x = x + jnp.where(idx >= d, pltpu.roll(x, d, axis), 0.0)
        d *= 2
    return x


def _ssd_kernel(*refs, n_chunks, aliased):
    xs_ref, xst_ref, b_ref, c_ref, dtc_ref, dtr_ref, init_ref = refs[:7]
    y_ref, fin_ref, st_ref = refs[-3:]
    d = pl.program_id(1)
    c = pl.program_id(2)
    tt = SSM_CHUNK
    nh = SSM_HEADS

    @pl.when(c == 0)
    def _():
        st_ref[...] = init_ref[0, 0]

    fwd = d == 0
    dtc = dtc_ref[...]
    dtr = dtr_ref[...]
    da_c = jnp.where(fwd, dtc[:, 2 * nh:3 * nh], dtc[:, 3 * nh:4 * nh])
    dtv_r = jnp.where(fwd, dtr[0:nh, :], dtr[nh:2 * nh, :])
    da_r = jnp.where(fwd, dtr[2 * nh:3 * nh, :], dtr[3 * nh:4 * nh, :])
    pc = _prefix_sum(da_c, 0)
    pr = _prefix_sum(da_r, 1)
    tot_c = pc[tt - 1:tt, :]
    tot_r = pr[:, tt - 1:tt]
    acs_c = jnp.where(fwd, pc, tot_c - pc + da_c)
    acs_r = jnp.where(fwd, pr, tot_r - pr + da_r)
    li = lax.broadcasted_iota(jnp.int32, (tt, tt), 0)
    si = lax.broadcasted_iota(jnp.int32, (tt, tt), 1)
    mask = jnp.where(fwd, li - si, si - li) >= 0
    w_r = dtv_r * jnp.exp(tot_r - acs_r)
    cdec_r = jnp.exp(tot_r)
    ns = SSM_STATE
    hp = SSM_HEADDIM
    for g in range(SSM_GROUPS):
        bg = b_ref[:, g * ns:(g + 1) * ns].astype(BF16)
        cg32 = c_ref[:, g * ns:(g + 1) * ns]
        cb = lax.dot_general(cg32.astype(BF16), bg, NT_DIMS, preferred_element_type=F32)
        for r in range(SSM_REP):
            h = g * SSM_REP + r
            acs_l = jnp.broadcast_to(acs_c[:, h:h + 1], (tt, tt))
            dec = jnp.exp(jnp.where(mask, acs_l - acs_r[h:h + 1, :], -jnp.inf))
            m = (cb * dec * dtv_r[h:h + 1, :]).astype(BF16)
            c_in = (cg32 * jnp.exp(acs_l)).astype(BF16)
            state = st_ref[h]
            y = jnp.dot(m, xs_ref[:, h * hp:(h + 1) * hp].astype(BF16), preferred_element_type=F32)
            y += lax.dot_general(c_in, state.astype(BF16), NT_DIMS, preferred_element_type=F32)
            y_ref[0, :, h * hp:(h + 1) * hp] = y
            xw = (xst_ref[h * hp:(h + 1) * hp, :] * w_r[h:h + 1, :]).astype(BF16)
            st_ref[h] = state * cdec_r[h:h + 1, :] + jnp.dot(xw, bg, preferred_element_type=F32)

    @pl.when(c == n_chunks - 1)
    def _():
        fin_ref[0, 0] = st_ref[...]


def ssd(xbc, xst, dtc, dtr, init, row0, n_seq, length, prev=None):
    t = xbc.shape[0]
    tt = SSM_CHUNK
    nc = length // tt
    b0 = row0 // tt
    gn = SSM_GROUPS * SSM_STATE

    def blk(b, d, c):
        return b0 + b * nc + jnp.where(d == 0, c, nc - 1 - c)

    st_shape = (SSM_HEADS, SSM_HEADDIM, SSM_STATE)
    st_spec = pl.BlockSpec((1, 1) + st_shape, lambda b, d, c: (b, d, 0, 0, 0))
    in_specs = [pl.BlockSpec((tt, SSM_W), lambda b, d, c: (blk(b, d, c), 0)),
                pl.BlockSpec((SSM_W, tt), lambda b, d, c: (0, blk(b, d, c))),
                pl.BlockSpec((tt, gn), lambda b, d, c: (blk(b, d, c), SSM_W // gn)),
                pl.BlockSpec((tt, gn), lambda b, d, c: (blk(b, d, c), SSM_W // gn + 1)),
                pl.BlockSpec((tt, LANES), lambda b, d, c: (blk(b, d, c), 0)),
                pl.BlockSpec((LANES, tt), lambda b, d, c: (0, blk(b, d, c))),
                st_spec]
    args = [xbc, xst, xbc, xbc, dtc, dtr, init]
    aliases = {}
    if prev is not None:
        in_specs.append(pl.BlockSpec(memory_space=pl.ANY))
        args.append(prev)
        aliases = {7: 0}
    return pl.pallas_call(
        functools.partial(_ssd_kernel, n_chunks=nc, aliased=prev is not None),
        grid=(n_seq, 2, nc),
        in_specs=in_specs,
        out_specs=[pl.BlockSpec((1, tt, SSM_W), lambda b, d, c: (d, blk(b, d, c), 0)), st_spec],
        out_shape=[jax.ShapeDtypeStruct((2, t, SSM_W), F32),
                   jax.ShapeDtypeStruct((n_seq, 2) + st_shape, F32)],
        scratch_shapes=[pltpu.VMEM(st_shape, F32)],
        input_output_aliases=aliases,
        compiler_params=_cparams("parallel", "parallel", "arbitrary"),
        name="ssd",
    )(*args)


def _ssd_gate_kernel(y_ref, xs_ref, zlo_ref, zhi_ref, d_ref, g_ref, o_ref):
    z = jnp.concatenate([zlo_ref[...], zhi_ref[...]], axis=1)
    y = (y_ref[0] + y_ref[1] + d_ref[...] * xs_ref[...]) * (z * jax.nn.sigmoid(z))
    y = y * lax.rsqrt(jnp.mean(y * y, axis=-1, keepdims=True) + RMS_EPS) * g_ref[...]
    o_ref[...] = y.astype(o_ref.dtype)


def ssd_gate(y2, xbc, u, ssm_d, ssm_norm):
    t = xbc.shape[0]
    half = SSM_W // 2
    zb = Z_OFF // half
    vec = pl.BlockSpec((1, SSM_W), lambda i: (0, 0))
    return pl.pallas_call(
        _ssd_gate_kernel,
        grid=(t // ROWS,),
        in_specs=[pl.BlockSpec((2, ROWS, SSM_W), lambda i: (0, i, 0)),
                  pl.BlockSpec((ROWS, SSM_W), lambda i: (i, 0)),
                  pl.BlockSpec((ROWS, half), lambda i: (i, zb)),
                  pl.BlockSpec((ROWS, half), lambda i: (i, zb + 1)),
                  vec, vec],
        out_specs=pl.BlockSpec((ROWS, SSM_W), lambda i: (i, 0)),
        out_shape=jax.ShapeDtypeStruct((t, SSM_W), BF16),
        compiler_params=_cparams("parallel"),
        name="ssd_gate",
    )(y2, xbc, u, u, jnp.repeat(ssm_d, SSM_HEADDIM).reshape(1, SSM_W), ssm_norm.reshape(1, SSM_W))


def dft_matrices(length):
    blk = 64
    two_l = 2 * length
    k = jnp.arange(length, dtype=jnp.int32)[:, None]
    a = jnp.arange(length // blk, dtype=jnp.int32)[None, :]
    b = jnp.arange(blk, dtype=jnp.int32)[None, :]
    xa = ((k * (a * blk)) % two_l).astype(F32) * (math.pi / length)
    xb = ((k * b) % two_l).astype(F32) * (math.pi / length)
    ca, sa, cb, sb = jnp.cos(xa)[:, :, None], jnp.sin(xa)[:, :, None], jnp.cos(xb)[:, None, :], jnp.sin(xb)[:, None, :]
    cos = (ca * cb - sa * sb).reshape(length, length)
    sin = (sa * cb + ca * sb).reshape(length, length)
    idx = jnp.arange(length, dtype=jnp.int32)
    alt = jnp.where(idx % 2 == 0, 1.0, -1.0).astype(F32)
    sin_fwd = jnp.where(idx[:, None] == 0, alt[None, :], sin)
    sin_inv = jnp.where(idx[None, :] == 0, alt[:, None], sin)
    return jnp.stack([cos, sin_fwd]).astype(BF16), jnp.stack([cos, sin_inv]).astype(BF16)


def _hyfilt_kernel(hid_ref, wf_ref, wb_ref, bf_ref, bb_ref, dl_ref, o_ref, nrm_ref, nyq_ref, *, length, tr):
    i = pl.program_id(2)
    hid = hid_ref[...].astype(BF16)
    t_idx = lax.broadcasted_iota(jnp.int32, (tr, 1), 0) + i * tr
    win = jnp.exp(-(t_idx.astype(F32) / length) * dl_ref[...])
    hf = (jnp.dot(hid, wf_ref[...].astype(BF16), preferred_element_type=F32) + bf_ref[...]) * win
    hb = (jnp.dot(hid, wb_ref[...].astype(BF16), preferred_element_type=F32) + bb_ref[...]) * win
    hb = jnp.where(t_idx == 0, 0.0, hb)
    o_ref[0] = (hf + hb).astype(o_ref.dtype)
    o_ref[1] = (hf - hb).astype(o_ref.dtype)
    sign = jnp.where(t_idx % 2 == 0, 1.0, -1.0)

    @pl.when(i == 0)
    def _():
        nrm_ref[...] = jnp.zeros_like(nrm_ref)
        nyq_ref[...] = jnp.zeros_like(nyq_ref)

    nrm_ref[...] += jnp.sum(jnp.abs(hf) + jnp.abs(hb), axis=0, keepdims=True)
    nyq_ref[...] += jnp.sum(sign * (hf + hb), axis=0, keepdims=True)


def hyena_filters(length, p, fwd):
    t = jnp.arange(length, dtype=F32) / length
    bands = jnp.linspace(1e-4, HY_BANDS - 1, HY_BANDS, dtype=F32)
    ang = 2.0 * math.pi * t[:, None] * bands
    feats = jnp.concatenate([t[:, None], jnp.cos(ang), jnp.sin(ang)], axis=-1)
    hid = jnp.sin(p['hy_freq'] * (feats @ p['hy_w1'] + p['hy_b1']))
    hid = jnp.sin(p['hy_freq'] * (hid @ p['hy_w2'] + p['hy_b2']))
    deltas = jnp.abs(jnp.linspace(math.log(HY_DECAY_TARGET) / HY_SLOW_DECAY,
                                  math.log(HY_DECAY_TARGET) / HY_FAST_DECAY, HY_W, dtype=F32)).reshape(1, HY_W)
    ffn = hid.shape[1]
    tr, tc = _pick(length, 256), 512
    nj = HY_W // tc
    w3, b3 = p['hy_w3'], p['hy_b3'].reshape(1, -1)
    ow = HY_ORDER * HY_W
    hsd, nrm, nyq = pl.pallas_call(
        functools.partial(_hyfilt_kernel, length=length, tr=tr),
        grid=(HY_ORDER, nj, length // tr),
        in_specs=[pl.BlockSpec((tr, ffn), lambda n, j, i: (i, 0)),
                  pl.BlockSpec((ffn, tc), lambda n, j, i: (0, (2 * n) * nj + j)),
                  pl.BlockSpec((ffn, tc), lambda n, j, i: (0, (2 * n + 1) * nj + j)),
                  pl.BlockSpec((1, tc), lambda n, j, i: (0, (2 * n) * nj + j)),
                  pl.BlockSpec((1, tc), lambda n, j, i: (0, (2 * n + 1) * nj + j)),
                  pl.BlockSpec((1, tc), lambda n, j, i: (0, j))],
        out_specs=[pl.BlockSpec((2, tr, tc), lambda n, j, i: (0, i, n * nj + j)),
                   pl.BlockSpec((1, tc), lambda n, j, i: (0, n * nj + j)),
                   pl.BlockSpec((1, tc), lambda n, j, i: (0, n * nj + j))],
        out_shape=[jax.ShapeDtypeStruct((2, length, ow), BF16),
                   jax.ShapeDtypeStruct((1, ow), F32), jax.ShapeDtypeStruct((1, ow), F32)],
        compiler_params=_cparams("parallel", "parallel", "arbitrary"),
        name="hyena_filter",
    )(hid, w3, w3, b3, b3, deltas)
    pq = gmm(fwd, hsd)
    return pq, nrm, nyq


def _dft_fwd_kernel(f_ref, z_ref, pq_ref, nrm_ref, nyq_ref, uv_ref, *, length, tm):
    i = pl.program_id(0)
    z = z_ref[...]
    a = jnp.dot(f_ref[0], z, preferred_element_type=F32)
    b = jnp.dot(f_ref[1], z, preferred_element_type=F32)
    k_idx = lax.broadcasted_iota(jnp.int32, (tm, 1), 0) + i * tm
    is0 = k_idx == 0
    wk = jnp.where(is0, 1.0, 2.0) * (0.5 / length) / nrm_ref[...]
    pp, qq = pq_ref[0], pq_ref[1]
    uv_ref[0, 0] = (wk * (a * pp - jnp.where(is0, 0.0, b * qq))).astype(uv_ref.dtype)
    uv_ref[0, 1] = (wk * jnp.where(is0, b * nyq_ref[...], a * qq + b * pp)).astype(uv_ref.dtype)


def dft_fwd(fwd, z, pq, nrm, nyq, order, row0, n_seq, length):
    tm, tn = _pick(length, 512), 512
    nj = HY_W // tn
    r0 = row0 // length
    vec = pl.BlockSpec((1, tn), lambda i, b, j: (0, order * nj + j))
    return pl.pallas_call(
        functools.partial(_dft_fwd_kernel, length=length, tm=tm),
        grid=(length // tm, n_seq, nj),
        in_specs=[pl.BlockSpec((2, tm, length), lambda i, b, j: (0, i, 0)),
                  pl.BlockSpec((length, tn), lambda i, b, j: (r0 + b, j)),
                  pl.BlockSpec((2, tm, tn), lambda i, b, j: (0, i, order * nj + j)),
                  vec, vec],
        out_specs=pl.BlockSpec((1, 2, tm, tn), lambda i, b, j: (b, 0, i, j)),
        out_shape=jax.ShapeDtypeStruct((n_seq, 2, length, HY_W), BF16),
        compiler_params=_cparams("parallel", "parallel", "parallel"),
        name="hyena_dft_fwd",
    )(fwd, z, pq, nrm, nyq)


def _hy_inv_kernel(*refs, n_out):
    f_ref, uv_ref, z_ref, gate_ref, bias_ref = refs[:5]
    o_refs = refs[-n_out:]
    y = jnp.dot(f_ref[0], uv_ref[0, 0], preferred_element_type=F32)
    y += jnp.dot(f_ref[1], uv_ref[0, 1], preferred_element_type=F32)
    y = gate_ref[...] * (y + bias_ref[...] * z_ref[...])
    for o_ref in o_refs:
        o_ref[...] = y.astype(o_ref.dtype)


def hy_inverse(inv, uv, z, zcol0, gate, gcol0, bias, row0, length, out_dtypes, prev=None):
    t = z.shape[0]
    n_seq = uv.shape[0]
    tm, tn = _pick(length, 512), 512
    r0 = row0 // tm
    ni = length // tm
    rmap = lambda c0: (lambda i, b, j: (r0 + b * ni + i, c0 // tn + j))
    in_specs = [pl.BlockSpec((2, tm, length), lambda i, b, j: (0, i, 0)),
                pl.BlockSpec((1, 2, length, tn), lambda i, b, j: (b, 0, 0, j)),
                pl.BlockSpec((tm, tn), rmap(zcol0)),
                pl.BlockSpec((tm, tn), rmap(gcol0)),
                pl.BlockSpec((1, tn), lambda i, b, j: (0, j))]
    args = [inv, uv, z, gate, bias.reshape(1, HY_W)]
    aliases = {}
    if prev is not None:
        for n, pv in enumerate(prev):
            in_specs.append(pl.BlockSpec(memory_space=pl.ANY))
            args.append(pv)
            aliases[5 + n] = n
    return pl.pallas_call(
        functools.partial(_hy_inv_kernel, n_out=len(out_dtypes)),
        grid=(ni, n_seq, HY_W // tn),
        in_specs=in_specs,
        out_specs=[pl.BlockSpec((tm, tn), rmap(0)) for _ in out_dtypes],
        out_shape=[jax.ShapeDtypeStruct((t, HY_W), dt) for dt in out_dtypes],
        input_output_aliases=aliases,
        compiler_params=_cparams("parallel", "parallel", "parallel"),
        name="hyena_dft_inv",
    )(*args)


def hyena_group(v32, v16, x12, filt, mats, p, row0, n_seq, length, prev):
    fwd, inv = mats
    pq, nrm, nyq = filt
    prev1, prev2 = prev
    uv = dft_fwd(fwd, v16, pq, nrm, nyq, 0, row0, n_seq, length)
    z1 = hy_inverse(inv, uv, v32, 0, x12, 0, p['hy_bias'][0], row0, length, (F32, BF16), prev1)
    uv = dft_fwd(fwd, z1[1], pq, nrm, nyq, 1, row0, n_seq, length)
    z2 = hy_inverse(inv, uv, z1[0], 0, x12, HY_W, p['hy_bias'][1], row0, length, (BF16,), prev2)
    return z1, z2


def _merge_kernel(a_ref, h_ref, s_ref, wa_ref, wh_ref, ws_ref, ga_ref, gh_ref, gs_ref, o_ref):
    acc = ga_ref[...].astype(F32) * jnp.dot(a_ref[...], wa_ref[...], preferred_element_type=F32)
    acc += gh_ref[...].astype(F32) * jnp.dot(h_ref[...], wh_ref[...], preferred_element_type=F32)
    acc += gs_ref[...].astype(F32) * jnp.dot(s_ref[...], ws_ref[...], preferred_element_type=F32)
    o_ref[...] = acc.astype(o_ref.dtype)


def branch_merge(att, hy, ssm, wa, wh, ws, gate_logits, tm=512, tn=512):
    t, kd = att.shape
    d = wa.shape[1]
    nj = d // tn
    xs = pl.BlockSpec((tm, kd), lambda j, i: (i, 0))
    ws_ = pl.BlockSpec((kd, tn), lambda j, i: (0, j))
    gs = lambda b: pl.BlockSpec((tm, tn), lambda j, i: (i, b * nj + j))
    return pl.pallas_call(
        _merge_kernel,
        grid=(nj, t // tm),
        in_specs=[xs, xs, xs, ws_, ws_, ws_, gs(0), gs(1), gs(2)],
        out_specs=pl.BlockSpec((tm, tn), lambda j, i: (i, j)),
        out_shape=jax.ShapeDtypeStruct((t, d), BF16),
        compiler_params=_cparams("parallel", "parallel"),
        name="branch_merge",
    )(att, hy, ssm, wa, wh, ws, gate_logits, gate_logits, gate_logits)


def _swiglu_kernel(x_ref, wg_ref, wu_ref, o_ref):
    x = x_ref[0]
    g = jnp.dot(x, wg_ref[0].astype(BF16), preferred_element_type=F32)
    u = jnp.dot(x, wu_ref[0].astype(BF16), preferred_element_type=F32)
    o_ref[0] = (g * jax.nn.sigmoid(g) * u).astype(o_ref.dtype)


def expert_swiglu(xs, w_gate, w_up, layer, tn=256):
    e, m, d = xs.shape
    f = w_gate.shape[3]
    wspec = pl.BlockSpec((None, 1, d, tn), lambda e, j: (layer, e, 0, j))
    return pl.pallas_call(
        _swiglu_kernel,
        grid=(e, f // tn),
        in_specs=[pl.BlockSpec((1, m, d), lambda e, j: (e, 0, 0)), wspec, wspec],
        out_specs=pl.BlockSpec((1, m, tn), lambda e, j: (e, 0, j)),
        out_shape=jax.ShapeDtypeStruct((e, m, f), BF16),
        compiler_params=_cparams("parallel", "parallel"),
        name="expert_swiglu",
    )(xs, w_gate, w_up)


def _down_kernel(h_ref, w_ref, g_ref, o_ref):
    o_ref[0] = jnp.dot(h_ref[0], w_ref[0].astype(BF16), preferred_element_type=F32) * g_ref[0]


def expert_down(hid, w_down, gates, layer, tn=512):
    e, m, f = hid.shape
    d = w_down.shape[3]
    return pl.pallas_call(
        _down_kernel,
        grid=(e, d // tn),
        in_specs=[pl.BlockSpec((1, m, f), lambda e, j: (e, 0, 0)),
                  pl.BlockSpec((None, 1, f, tn), lambda e, j: (layer, e, 0, j)),
                  pl.BlockSpec((1, m, 1), lambda e, j: (e, 0, 0))],
        out_specs=pl.BlockSpec((1, m, tn), lambda e, j: (e, 0, j)),
        out_shape=jax.ShapeDtypeStruct((e, m, d), F32),
        compiler_params=_cparams("parallel", "parallel"),
        name="expert_down",
    )(hid, w_down, gates)


def ec_moe(xm, p, lay):
    t, d = xm.shape
    w_r = jnp.zeros((d, LANES), BF16).at[:, :N_EXPERTS].set(p['w_router'].astype(BF16))
    logits = mm(xm, w_r, tn=LANES)[:, :N_EXPERTS]
    aff = jax.nn.softmax(logits, axis=-1)
    gates, rows = [], []
    for row0, n_seq, length in ((0, lay.n_p, lay.l_p), (lay.tp, lay.n_s, lay.l_s)):
        cap = EC_CAPACITY * length // N_EXPERTS
        a = aff[row0:row0 + n_seq * length].reshape(n_seq, length, N_EXPERTS)
        g, idx = lax.top_k(jnp.swapaxes(a, 1, 2), cap)
        idx = idx + (row0 + jnp.arange(n_seq, dtype=idx.dtype) * length)[:, None, None]
        gates.append(jnp.swapaxes(g, 0, 1).reshape(N_EXPERTS, n_seq * cap))
        rows.append(jnp.swapaxes(idx, 0, 1).reshape(N_EXPERTS, n_seq * cap))
    gates = jnp.concatenate(gates, axis=1)
    rows = jnp.concatenate(rows, axis=1)
    xs = jnp.take(xm, rows.reshape(-1), axis=0).reshape(N_EXPERTS, -1, d)
    hid = expert_swiglu(xs, p['w_gate'], p['w_up'], p['layer'])
    y = expert_down(hid, p['w_down'], gates[..., None], p['layer'])
    return jnp.zeros((t, d), F32).at[rows.reshape(-1)].add(y.reshape(-1, d))


def rope_tables(lay):
    pos = jnp.arange(lay.l_s)
    row = (pos // GRID_W).astype(F32)
    col = (pos % GRID_W).astype(F32)
    inv = ROPE_THETA ** (-jnp.arange(ROT_FREQS, dtype=F32) / ROT_FREQS)
    ang = jnp.concatenate([row[:, None] * inv] * 2 + [col[:, None] * inv] * 2, axis=1)
    sign = jnp.where((jnp.arange(HEAD_DIM) % (2 * ROT_FREQS)) < ROT_FREQS, -1.0, 1.0).astype(F32)
    cos_s, sin_s = jnp.cos(ang), jnp.sin(ang) * sign
    cos_t = jnp.concatenate([jnp.ones((lay.tp, HEAD_DIM), F32)] + [cos_s] * lay.n_s, axis=0)
    sin_t = jnp.concatenate([jnp.zeros((lay.tp, HEAD_DIM), F32)] + [sin_s] * lay.n_s, axis=0)
    return cos_t, sin_t


def trunk_layer(x, xm, p, mod, mod_next, consts, lay, cache_k, cache_v, state_ssm):
    cos_t, sin_t, mats_p, mats_s = consts
    tp = lay.tp
    u = mm_cols(xm, p['w_in'], p['layer'], 0, DT_OFF)
    dt_raw = mm_cols(xm, p['w_in'], p['layer'], DT_OFF, LANES, tn=LANES)
    gates = mm_cols(xm, p['w_in_gate'], None, 0, N_BRANCH * D_MODEL, out_dtype=BF16, sigmoid=True,
                    tm=1024, tn=512)

    q, kr, vb, kf, vf = qkv_prep(u, cos_t, sin_t, p['q_norm'], p['k_norm'])
    k_p = kr[:tp].reshape(lay.n_p, lay.l_p, KV_W)
    v_p = vb[:tp].reshape(lay.n_p, lay.l_p, KV_W)
    k_s = jnp.concatenate([cache_k.reshape(lay.n_s, -1, KV_W).astype(BF16),
                           kr[tp:].reshape(lay.n_s, lay.l_s, KV_W)], axis=1)
    v_s = jnp.concatenate([cache_v.reshape(lay.n_s, -1, KV_W).astype(BF16),
                           vb[tp:].reshape(lay.n_s, lay.l_s, KV_W)], axis=1)
    att = attention(q, k_p, v_p, 0, lay.n_p, lay.l_p, lay.l_p)
    att = attention(q, k_s, v_s, tp, lay.n_s, lay.l_s, ROWS, prev=att)

    v32, v16 = conv3(u, HY_OFF, HY_W, p['hy_conv_w'][:, :HY_W], p['hy_conv_b'][:HY_W], lay, silu=False,
                     out_dtypes=(F32, BF16))
    x12 = conv3(u, HY_OFF + HY_W, 2 * HY_W, p['hy_conv_w'][:, HY_W:], p['hy_conv_b'][HY_W:], lay, silu=False)
    filt_p = hyena_filters(lay.l_p, p, mats_p[0])
    filt_s = hyena_filters(lay.l_s, p, mats_s[0])
    z1, hy = hyena_group(v32, v16, x12, filt_p, mats_p, p, 0, lay.n_p, lay.l_p, (None, None))
    _, hy = hyena_group(v32, v16, x12, filt_s, mats_s, p, tp, lay.n_s, lay.l_s, (z1, hy))
    hy = hy[0]

    xbc = conv3(u, XBC_OFF, SSM_CONV_DIM, p['ssm_conv_w'], p['ssm_conv_b'], lay, silu=True)
    dtc = dt_prep(dt_raw, p['ssm_dt_bias'], p['ssm_a_log'])
    dtr = dtc.T
    zero_state = jnp.zeros((lay.n_p, 2, SSM_HEADS, SSM_HEADDIM, SSM_STATE), F32)
    xst = xbc[:, :SSM_W].T
    y2, states = ssd(xbc, xst, dtc, dtr, zero_state, 0, lay.n_p, lay.l_p)
    y2, _ = ssd(xbc, xst, dtc, dtr, state_ssm, tp, lay.n_s, lay.l_s, prev=y2)
    ssm = ssd_gate(y2, xbc, u, p['ssm_d'], p['ssm_norm'])

    merged = branch_merge(att, hy, ssm, p['w_br_att'], p['w_br_hy'], p['w_br_ssm'], gates)
    m = mm(merged, p['w_out'])
    x1, xm2 = ln_mod(x, m, mod, 2, p['ln1_g'], p['ln1_b'], lay, mod_next=mod, sec_sc=4, sec_sh=3)
    f = ec_moe(xm2, p, lay)
    x2, xm_next = ln_mod(x1, f, mod, 5, p['ln2_g'], p['ln2_b'], lay, mod_next=mod_next, sec_sc=1, sec_sh=0)
    new_k = kf[:tp].reshape(lay.n_p, lay.l_p, N_KV_HEADS, HEAD_DIM)
    new_v = vf[:tp].reshape(lay.n_p, lay.l_p, N_KV_HEADS, HEAD_DIM)
    return x2, xm_next, (new_k, new_v, states)


def kernel(x_prompt, x_sample, cache_k, cache_v, state_ssm, c, c_ctx, w_mod, b_mod, w_in, q_norm, k_norm, hy_conv_w, hy_conv_b, hy_w1, hy_b1, hy_freq, hy_w2, hy_b2, hy_w3, hy_b3, hy_bias, ssm_conv_w, ssm_conv_b, ssm_dt_bias, ssm_a_log, ssm_d, ssm_norm, w_br_att, w_br_hy, w_br_ssm, w_out, ln1_g, ln1_b, w_router, w_gate, w_up, w_down, ln2_g, ln2_b):
    n_p, l_p, d = x_prompt.shape
    n_s, l_s, _ = x_sample.shape
    depth = w_in.shape[0]
    lay = Layout(n_p, l_p, n_s, l_s)
    x = jnp.concatenate([x_prompt.reshape(lay.tp, d), x_sample.reshape(lay.ts, d)], axis=0)

    cond = jnp.zeros((N_COND_PAD, d), F32).at[0].set(c_ctx).at[1:1 + n_s].set(c)
    act = (cond * jax.nn.sigmoid(cond)).astype(BF16)
    mod_all = gmm(act[None], w_mod, tm=N_COND_PAD, tn=2048, tk=1024, share_x=True) + b_mod[:, None, :]
    mods = [mod_all[l].reshape(N_COND_PAD, 1, 6 * d) for l in range(depth)]

    consts = rope_tables(lay) + (dft_matrices(l_p), dft_matrices(l_s))
    xm = modulate(x, mods[0], 1, 0, lay)
    new_k, new_v, new_s = [], [], []
    for l in range(depth):
        p = dict(w_in=w_in, w_in_gate=w_in[l, :, GATE_OFF:].astype(BF16),
                 q_norm=q_norm[l], k_norm=k_norm[l],
                 hy_conv_w=hy_conv_w[l], hy_conv_b=hy_conv_b[l], hy_w1=hy_w1[l], hy_b1=hy_b1[l],
                 hy_freq=hy_freq[l], hy_w2=hy_w2[l], hy_b2=hy_b2[l], hy_w3=hy_w3[l], hy_b3=hy_b3[l],
                 hy_bias=hy_bias[l], ssm_conv_w=ssm_conv_w[l], ssm_conv_b=ssm_conv_b[l],
                 ssm_dt_bias=ssm_dt_bias[l], ssm_a_log=ssm_a_log[l], ssm_d=ssm_d[l], ssm_norm=ssm_norm[l],
                 w_br_att=w_br_att[l].astype(BF16), w_br_hy=w_br_hy[l].astype(BF16),
                 w_br_ssm=w_br_ssm[l].astype(BF16), w_out=w_out[l].astype(BF16),
                 ln1_g=ln1_g[l], ln1_b=ln1_b[l], w_router=w_router[l],
                 w_gate=w_gate, w_up=w_up, w_down=w_down, layer=l,
                 ln2_g=ln2_g[l], ln2_b=ln2_b[l])
        mod_next = mods[l + 1] if l + 1 < depth else None
        x, xm, (k_l, v_l, s_l) = trunk_layer(x, xm, p, mods[l], mod_next, consts, lay,
                                             cache_k[:, l], cache_v[:, l], state_ssm[:, l])
        new_k.append(k_l)
        new_v.append(v_l)
        new_s.append(s_l)
    y_prompt = x[:lay.tp].reshape(n_p, l_p, d)
    y_sample = x[lay.tp:].reshape(n_s, l_s, d)
    return (y_prompt, y_sample, jnp.stack(new_k, axis=1), jnp.stack(new_v, axis=1), jnp.stack(new_s, axis=1))
```

```python
import functools
import math

import jax
import jax.numpy as jnp
from jax import lax
from jax.experimental import pallas as pl
from jax.experimental.pallas import tpu as pltpu

F32 = jnp.float32
BF16 = jnp.bfloat16

D_MODEL = 4096
DEPTH = 2
GRID_W = 64
N_HEADS = 16
N_KV_HEADS = 4
KV_REP = N_HEADS // N_KV_HEADS
HEAD_DIM = 128
ATT_W = N_HEADS * HEAD_DIM
KV_W = N_KV_HEADS * HEAD_DIM
ROT_FREQS = HEAD_DIM // 4
ROPE_THETA = 10000.0
HY_W = 2048
HY_ORDER = 2
HY_BANDS = 16
HY_DECAY_TARGET = 1e-2
HY_FAST_DECAY = 0.3
HY_SLOW_DECAY = 1.5
SSM_W = 2048
SSM_HEADDIM = 64
SSM_HEADS = SSM_W // SSM_HEADDIM
SSM_GROUPS = 8
SSM_REP = SSM_HEADS // SSM_GROUPS
SSM_STATE = 128
SSM_CHUNK = 128
SSM_CONV_DIM = SSM_W + 2 * SSM_GROUPS * SSM_STATE
N_EXPERTS = 16
EC_CAPACITY = 2
MOE_FF = 2048
N_BRANCH = 3
Q_OFF = 0
K_OFF = Q_OFF + ATT_W
V_OFF = K_OFF + KV_W
HY_OFF = V_OFF + KV_W
Z_OFF = HY_OFF + 3 * HY_W
XBC_OFF = Z_OFF + SSM_W
DT_OFF = XBC_OFF + SSM_CONV_DIM
GATE_OFF = DT_OFF + 2 * SSM_HEADS
ALPHA = (2 * DEPTH) ** 0.25
LN_EPS = 1e-5
RMS_EPS = 1e-6
N_COND_PAD = 8
LANES = 128
SUBLANES = 8

VMEM_LIMIT_BYTES = 56 * 1024 * 1024

NT_DIMS = (((1,), (1,)), ((), ()))
TN_DIMS = (((0,), (0,)), ((), ()))


def _cparams(*sem):
    return pltpu.CompilerParams(dimension_semantics=sem, vmem_limit_bytes=VMEM_LIMIT_BYTES)


def _pick(dim, pref):
    t = min(dim, pref)
    while dim % t:
        t //= 2
    return t


def _mm_kernel(x_ref, w_ref, o_ref, acc_ref):
    k = pl.program_id(3)

    @pl.when(k == 0)
    def _():
        acc_ref[...] = jnp.zeros_like(acc_ref)

    acc_ref[...] += jnp.dot(x_ref[0].astype(BF16), w_ref[0].astype(BF16), preferred_element_type=F32)

    @pl.when(k == pl.num_programs(3) - 1)
    def _():
        o_ref[0] = acc_ref[...].astype(o_ref.dtype)


def _mm_fullk_kernel(x_ref, w_ref, o_ref):
    o_ref[0] = jnp.dot(x_ref[0].astype(BF16), w_ref[0].astype(BF16),
                       preferred_element_type=F32).astype(o_ref.dtype)


def gmm(x, w, out_dtype=F32, tm=1024, tn=512, tk=4096, share_x=False):
    g, kd, n = w.shape
    m = x.shape[1]
    tm, tn, tk = _pick(m, tm), _pick(n, tn), _pick(kd, tk)
    if tk == kd:
        xmap = (lambda e, i, j: (0, i, 0)) if share_x else (lambda e, i, j: (e, i, 0))
        return pl.pallas_call(
            _mm_fullk_kernel,
            grid=(g, m // tm, n // tn),
            in_specs=[pl.BlockSpec((1, tm, kd), xmap),
                      pl.BlockSpec((1, kd, tn), lambda e, i, j: (e, 0, j))],
            out_specs=pl.BlockSpec((1, tm, tn), lambda e, i, j: (e, i, j)),
            out_shape=jax.ShapeDtypeStruct((g, m, n), out_dtype),
            compiler_params=_cparams("parallel", "parallel", "parallel"),
            name="gmm",
        )(x, w)
    xmap = (lambda e, i, j, k: (0, i, k)) if share_x else (lambda e, i, j, k: (e, i, k))
    return pl.pallas_call(
        _mm_kernel,
        grid=(g, m // tm, n // tn, kd // tk),
        in_specs=[pl.BlockSpec((1, tm, tk), xmap),
                  pl.BlockSpec((1, tk, tn), lambda e, i, j, k: (e, k, j))],
        out_specs=pl.BlockSpec((1, tm, tn), lambda e, i, j, k: (e, i, j)),
        out_shape=jax.ShapeDtypeStruct((g, m, n), out_dtype),
        scratch_shapes=[pltpu.VMEM((tm, tn), F32)],
        compiler_params=_cparams("parallel", "parallel", "parallel", "arbitrary"),
        name="gmm",
    )(x, w)


def mm(x, w, out_dtype=F32, **kw):
    return gmm(x[None], w[None], out_dtype, **kw)[0]


def _mm_act_kernel(x_ref, w_ref, o_ref, *, sigmoid):
    y = jnp.dot(x_ref[...], w_ref[...].astype(BF16), preferred_element_type=F32)
    if sigmoid:
        y = jax.nn.sigmoid(y)
    o_ref[...] = y.astype(o_ref.dtype)


def mm_cols(x, w, layer, col0, n_cols, out_dtype=F32, sigmoid=False, tm=2048, tn=256):
    m, kd = x.shape
    tm, tn = _pick(m, tm), _pick(n_cols, tn)
    c0 = col0 // tn
    if layer is None:
        wspec = pl.BlockSpec((kd, tn), lambda i, j: (0, c0 + j))
    else:
        wspec = pl.BlockSpec((None, kd, tn), lambda i, j: (layer, 0, c0 + j))
    return pl.pallas_call(
        functools.partial(_mm_act_kernel, sigmoid=sigmoid),
        grid=(m // tm, n_cols // tn),
        in_specs=[pl.BlockSpec((tm, kd), lambda i, j: (i, 0)), wspec],
        out_specs=pl.BlockSpec((tm, tn), lambda i, j: (i, j)),
        out_shape=jax.ShapeDtypeStruct((m, n_cols), out_dtype),
        compiler_params=_cparams("parallel", "parallel"),
        name="mm_cols",
    )(x, w)


class Layout:
    def __init__(self, n_p, l_p, n_s, l_s):
        self.n_p, self.l_p, self.n_s, self.l_s = n_p, l_p, n_s, l_s
        self.tp = n_p * l_p
        self.ts = n_s * l_s
        self.t = self.tp + self.ts

    def group_of_block(self, i, rows):
        bp = self.tp // rows
        return jnp.where(i < bp, 0, 1 + (i - bp) // (self.l_s // rows))

    def seq_edges(self, i, rows):
        bp = self.tp // rows
        per_p, per_s = self.l_p // rows, self.l_s // rows
        first = jnp.where(i < bp, i % per_p == 0, (i - bp) % per_s == 0)
        last = jnp.where(i < bp, i % per_p == per_p - 1, (i - bp) % per_s == per_s - 1)
        return first, last


ROWS = 256


def _modulate_kernel(x_ref, sc_ref, sh_ref, o_ref):
    o_ref[...] = (x_ref[...] * (1.0 + sc_ref[0]) + sh_ref[0]).astype(o_ref.dtype)


def modulate(x, mod, sec_sc, sec_sh, lay):
    t, d = x.shape
    grp = lambda i: lay.group_of_block(i, ROWS)
    return pl.pallas_call(
        _modulate_kernel,
        grid=(t // ROWS,),
        in_specs=[pl.BlockSpec((ROWS, d), lambda i: (i, 0)),
                  pl.BlockSpec((1, 1, d), lambda i: (grp(i), 0, sec_sc)),
                  pl.BlockSpec((1, 1, d), lambda i: (grp(i), 0, sec_sh))],
        out_specs=pl.BlockSpec((ROWS, d), lambda i: (i, 0)),
        out_shape=jax.ShapeDtypeStruct((t, d), BF16),
        compiler_params=_cparams("parallel"),
        name="modulate",
    )(x, mod, mod)


def _ln_mod_kernel(*refs, with_mod):
    if with_mod:
        x_ref, m_ref, gt_ref, lg_ref, lb_ref, sc_ref, sh_ref, o_ref, om_ref = refs
    else:
        x_ref, m_ref, gt_ref, lg_ref, lb_ref, o_ref = refs
    r = ALPHA * x_ref[...] + gt_ref[0] * m_ref[...]
    mu = jnp.mean(r, axis=-1, keepdims=True)
    dlt = r - mu
    var = jnp.mean(dlt * dlt, axis=-1, keepdims=True)
    y = dlt * lax.rsqrt(var + LN_EPS) * lg_ref[...] + lb_ref[...]
    o_ref[...] = y
    if with_mod:
        om_ref[...] = (y * (1.0 + sc_ref[0]) + sh_ref[0]).astype(om_ref.dtype)


def ln_mod(x, m, mod, sec_gate, ln_g, ln_b, lay, mod_next=None, sec_sc=0, sec_sh=0):
    t, d = x.shape
    rows = ROWS // 2
    grp = lambda i: lay.group_of_block(i, rows)
    with_mod = mod_next is not None
    row_spec = pl.BlockSpec((rows, d), lambda i: (i, 0))
    vec_spec = pl.BlockSpec((1, d), lambda i: (0, 0))
    in_specs = [row_spec, row_spec,
                pl.BlockSpec((1, 1, d), lambda i: (grp(i), 0, sec_gate)), vec_spec, vec_spec]
    args = [x, m, mod, ln_g.reshape(1, d), ln_b.reshape(1, d)]
    out_specs = [row_spec]
    out_shape = [jax.ShapeDtypeStruct((t, d), F32)]
    if with_mod:
        in_specs += [pl.BlockSpec((1, 1, d), lambda i: (grp(i), 0, sec_sc)),
                     pl.BlockSpec((1, 1, d), lambda i: (grp(i), 0, sec_sh))]
        args += [mod_next, mod_next]
        out_specs.append(row_spec)
        out_shape.append(jax.ShapeDtypeStruct((t, d), BF16))
    res = pl.pallas_call(
        functools.partial(_ln_mod_kernel, with_mod=with_mod),
        grid=(t // rows,),
        in_specs=in_specs, out_specs=out_specs, out_shape=out_shape,
        compiler_params=_cparams("parallel"),
        name="ln_mod",
    )(*args)
    return (res[0], res[1]) if with_mod else (res[0], None)


def _qkv_prep_kernel(u_ref, cos_ref, sin_ref, qn_ref, kn_ref, q_ref, kr_ref, vb_ref, kf_ref, vf_ref):
    cos = cos_ref[...]
    sin = sin_ref[...]
    lane = lax.broadcasted_iota(jnp.int32, cos.shape, 1)
    lane_lo = (lane % (2 * ROT_FREQS)) < ROT_FREQS

    def norm(x, g):
        return x * lax.rsqrt(jnp.mean(x * x, axis=-1, keepdims=True) + RMS_EPS) * g

    def rope(x):
        sw = jnp.where(lane_lo, pltpu.roll(x, LANES - ROT_FREQS, 1), pltpu.roll(x, ROT_FREQS, 1))
        return x * cos + sw * sin

    for h in range(N_HEADS):
        sl = slice(Q_OFF + h * HEAD_DIM, Q_OFF + (h + 1) * HEAD_DIM)
        q = rope(norm(u_ref[:, sl], qn_ref[...])) * (HEAD_DIM ** -0.5)
        q_ref[:, h * HEAD_DIM:(h + 1) * HEAD_DIM] = q.astype(q_ref.dtype)
    for h in range(N_KV_HEADS):
        o = slice(h * HEAD_DIM, (h + 1) * HEAD_DIM)
        kk = norm(u_ref[:, K_OFF + h * HEAD_DIM:K_OFF + (h + 1) * HEAD_DIM], kn_ref[...])
        kf_ref[:, o] = kk
        kr_ref[:, o] = rope(kk).astype(kr_ref.dtype)
        vv = u_ref[:, V_OFF + h * HEAD_DIM:V_OFF + (h + 1) * HEAD_DIM]
        vf_ref[:, o] = vv
        vb_ref[:, o] = vv.astype(vb_ref.dtype)


def qkv_prep(u, cos_t, sin_t, q_norm, k_norm):
    t = u.shape[0]
    row = lambda w: pl.BlockSpec((ROWS, w), lambda i: (i, 0))
    vec = pl.BlockSpec((1, HEAD_DIM), lambda i: (0, 0))
    return pl.pallas_call(
        _qkv_prep_kernel,
        grid=(t // ROWS,),
        in_specs=[row(HY_OFF), row(HEAD_DIM), row(HEAD_DIM), vec, vec],
        out_specs=[row(ATT_W), row(KV_W), row(KV_W), row(KV_W), row(KV_W)],
        out_shape=[jax.ShapeDtypeStruct((t, ATT_W), BF16), jax.ShapeDtypeStruct((t, KV_W), BF16),
                   jax.ShapeDtypeStruct((t, KV_W), BF16), jax.ShapeDtypeStruct((t, KV_W), F32),
                   jax.ShapeDtypeStruct((t, KV_W), F32)],
        compiler_params=_cparams("parallel"),
        name="qkv_prep",
    )(u, cos_t, sin_t, q_norm.reshape(1, HEAD_DIM), k_norm.reshape(1, HEAD_DIM))


def _attn_kernel(*refs, aliased):
    q_ref, k_ref, v_ref = refs[:3]
    o_ref = refs[-1]
    k = k_ref[0]
    v = v_ref[0]
    for r in range(KV_REP):
        sl = slice(r * HEAD_DIM, (r + 1) * HEAD_DIM)
        s = lax.dot_general(q_ref[:, sl], k, NT_DIMS, preferred_element_type=F32)
        m = jnp.max(s, axis=1, keepdims=True)
        p = jnp.exp(s - m)
        l = jnp.sum(p, axis=1, keepdims=True)
        o = jnp.dot(p.astype(BF16), v, preferred_element_type=F32)
        o_ref[:, sl] = (o / l).astype(o_ref.dtype)


def attention(q, k, v, row0, n_seq, l_q, tq, prev=None):
    t = q.shape[0]
    l_k = k.shape[1]
    gw = KV_REP * HEAD_DIM
    nq = l_q // tq
    b0 = row0 // tq
    qmap = lambda b, g, i: (b0 + b * nq + i, g)
    in_specs = [pl.BlockSpec((tq, gw), qmap),
                pl.BlockSpec((1, l_k, HEAD_DIM), lambda b, g, i: (b, 0, g)),
                pl.BlockSpec((1, l_k, HEAD_DIM), lambda b, g, i: (b, 0, g))]
    args = [q, k, v]
    aliases = {}
    if prev is not None:
        in_specs.append(pl.BlockSpec(memory_space=pl.ANY))
        args.append(prev)
        aliases = {3: 0}
    return pl.pallas_call(
        functools.partial(_attn_kernel, aliased=prev is not None),
        grid=(n_seq, N_KV_HEADS, nq),
        in_specs=in_specs,
        out_specs=pl.BlockSpec((tq, gw), qmap),
        out_shape=jax.ShapeDtypeStruct((t, ATT_W), BF16),
        input_output_aliases=aliases,
        compiler_params=_cparams("parallel", "parallel", "parallel"),
        name="attention",
    )(*args)


CONV_ROWS = 1024


def _conv3_kernel(x_ref, prev_ref, next_ref, w_ref, b_ref, *o_refs, lay, silu):
    i = pl.program_id(0)
    x = x_ref[...]
    rows = lax.broadcasted_iota(jnp.int32, (CONV_ROWS, 1), 0)
    grow = rows + i * CONV_ROWS
    in_p = grow < lay.tp
    pos = jnp.where(in_p, grow & (lay.l_p - 1), (grow - lay.tp) & (lay.l_s - 1))
    last = jnp.where(in_p, lay.l_p - 1, lay.l_s - 1)
    xm1 = jnp.where(rows == 0, prev_ref[SUBLANES - 1:SUBLANES, :], pltpu.roll(x, 1, 0))
    xp1 = jnp.where(rows == CONV_ROWS - 1, next_ref[0:1, :], pltpu.roll(x, CONV_ROWS - 1, 0))
    xm1 = jnp.where(pos == 0, 0.0, xm1)
    xp1 = jnp.where(pos == last, 0.0, xp1)
    y = xm1 * w_ref[0:1, :] + x * w_ref[1:2, :] + xp1 * w_ref[2:3, :] + b_ref[...]
    if silu:
        y = y * jax.nn.sigmoid(y)
    for o_ref in o_refs:
        o_ref[...] = y.astype(o_ref.dtype)


def conv3(u, col0, width, w, b, lay, silu, out_dtypes=(F32,), tc=1024):
    t = u.shape[0]
    assert lay.l_p & (lay.l_p - 1) == 0 and lay.l_s & (lay.l_s - 1) == 0
    c0 = col0 // tc
    sub = CONV_ROWS // SUBLANES
    n_sub = t // SUBLANES
    res = pl.pallas_call(
        functools.partial(_conv3_kernel, lay=lay, silu=silu),
        grid=(t // CONV_ROWS, width // tc),
        in_specs=[pl.BlockSpec((CONV_ROWS, tc), lambda i, j: (i, c0 + j)),
                  pl.BlockSpec((SUBLANES, tc), lambda i, j: (jnp.maximum(i * sub - 1, 0), c0 + j)),
                  pl.BlockSpec((SUBLANES, tc), lambda i, j: (jnp.minimum((i + 1) * sub, n_sub - 1), c0 + j)),
                  pl.BlockSpec((3, tc), lambda i, j: (0, j)),
                  pl.BlockSpec((1, tc), lambda i, j: (0, j))],
        out_specs=[pl.BlockSpec((CONV_ROWS, tc), lambda i, j: (i, j)) for _ in out_dtypes],
        out_shape=[jax.ShapeDtypeStruct((t, width), dt) for dt in out_dtypes],
        compiler_params=_cparams("parallel", "parallel"),
        name="conv3",
    )(u, u, u, w, b.reshape(1, width))
    return res if len(out_dtypes) > 1 else res[0]


def _dt_prep_kernel(raw_ref, bias_ref, a_ref, o_ref):
    x = raw_ref[...] + bias_ref[...]
    dt = jnp.maximum(x, 0.0) + jnp.log1p(jnp.exp(-jnp.abs(x)))
    o_ref[...] = jnp.where(lax.broadcasted_iota(jnp.int32, x.shape, 1) < 2 * SSM_HEADS,
                           dt, pltpu.roll(dt, 2 * SSM_HEADS, 1) * a_ref[...])


def dt_prep(raw, dt_bias, a_log):
    t = raw.shape[0]
    nh2 = 2 * SSM_HEADS
    bias = jnp.zeros((1, LANES), F32).at[0, :nh2].set(dt_bias.reshape(-1))
    a = jnp.zeros((1, LANES), F32).at[0, nh2:2 * nh2].set(-jnp.exp(a_log.reshape(-1)))
    return pl.pallas_call(
        _dt_prep_kernel,
        grid=(t // ROWS,),
        in_specs=[pl.BlockSpec((ROWS, LANES), lambda i: (i, 0)),
                  pl.BlockSpec((1, LANES), lambda i: (0, 0)),
                  pl.BlockSpec((1, LANES), lambda i: (0, 0))],
        out_specs=pl.BlockSpec((ROWS, LANES), lambda i: (i, 0)),
        out_shape=jax.ShapeDtypeStruct((t, LANES), F32),
        compiler_params=_cparams("parallel"),
        name="dt_prep",
    )(raw, bias, a)


def _prefix_sum(x, axis):
    idx = lax.broadcasted_iota(jnp.int32, x.shape, axis)
    d = 1
    while d < SSM_CHUNK:
        x = x + jnp.where(idx >= d, pltpu.roll(x, d, axis), 0.0)
        d *= 2
    return x


def _ssd_kernel(*refs, n_chunks, aliased):
    xs_ref, xst_ref, b_ref, c_ref, dtc_ref, dtr_ref, init_ref = refs[:7]
    y_ref, fin_ref, st_ref = refs[-3:]
    d = pl.program_id(1)
    c = pl.program_id(2)
    tt = SSM_CHUNK
    nh = SSM_HEADS

    @pl.when(c == 0)
    def _():
        st_ref[...] = init_ref[0, 0]

    fwd = d == 0
    dtc = dtc_ref[...]
    dtr = dtr_ref[...]
    da_c = jnp.where(fwd, dtc[:, 2 * nh:3 * nh], dtc[:, 3 * nh:4 * nh])
    dtv_r = jnp.where(fwd, dtr[0:nh, :], dtr[nh:2 * nh, :])
    da_r = jnp.where(fwd, dtr[2 * nh:3 * nh, :], dtr[3 * nh:4 * nh, :])
    pc = _prefix_sum(da_c, 0)
    pr = _prefix_sum(da_r, 1)
    tot_c = pc[tt - 1:tt, :]
    tot_r = pr[:, tt - 1:tt]
    acs_c = jnp.where(fwd, pc, tot_c - pc + da_c)
    acs_r = jnp.where(fwd, pr, tot_r - pr + da_r)
    li = lax.broadcasted_iota(jnp.int32, (tt, tt), 0)
    si = lax.broadcasted_iota(jnp.int32, (tt, tt), 1)
    mask = jnp.where(fwd, li - si, si - li) >= 0
    w_r = dtv_r * jnp.exp(tot_r - acs_r)
    cdec_r = jnp.exp(tot_r)
    ns = SSM_STATE
    hp = SSM_HEADDIM
    for g in range(SSM_GROUPS):
        bg = b_ref[:, g * ns:(g + 1) * ns].astype(BF16)
        cg32 = c_ref[:, g * ns:(g + 1) * ns]
        cb = lax.dot_general(cg32.astype(BF16), bg, NT_DIMS, preferred_element_type=F32)
        for r in range(SSM_REP):
            h = g * SSM_REP + r
            acs_l = jnp.broadcast_to(acs_c[:, h:h + 1], (tt, tt))
            dec = jnp.exp(jnp.where(mask, acs_l - acs_r[h:h + 1, :], -jnp.inf))
            m = (cb * dec * dtv_r[h:h + 1, :]).astype(BF16)
            c_in = (cg32 * jnp.exp(acs_l)).astype(BF16)
            state = st_ref[h]
            y = jnp.dot(m, xs_ref[:, h * hp:(h + 1) * hp].astype(BF16), preferred_element_type=F32)
            y += lax.dot_general(c_in, state.astype(BF16), NT_DIMS, preferred_element_type=F32)
            y_ref[0, :, h * hp:(h + 1) * hp] = y
            xw = (xst_ref[h * hp:(h + 1) * hp, :] * w_r[h:h + 1, :]).astype(BF16)
            st_ref[h] = state * cdec_r[h:h + 1, :] + jnp.dot(xw, bg, preferred_element_type=F32)

    @pl.when(c == n_chunks - 1)
    def _():
        fin_ref[0, 0] = st_ref[...]


def ssd(xbc, xst, dtc, dtr, init, row0, n_seq, length, prev=None):
    t = xbc.shape[0]
    tt = SSM_CHUNK
    nc = length // tt
    b0 = row0 // tt
    gn = SSM_GROUPS * SSM_STATE

    def blk(b, d, c):
        return b0 + b * nc + jnp.where(d == 0, c, nc - 1 - c)

    st_shape = (SSM_HEADS, SSM_HEADDIM, SSM_STATE)
    st_spec = pl.BlockSpec((1, 1) + st_shape, lambda b, d, c: (b, d, 0, 0, 0))
    in_specs = [pl.BlockSpec((tt, SSM_W), lambda b, d, c: (blk(b, d, c), 0)),
                pl.BlockSpec((SSM_W, tt), lambda b, d, c: (0, blk(b, d, c))),
                pl.BlockSpec((tt, gn), lambda b, d, c: (blk(b, d, c), SSM_W // gn)),
                pl.BlockSpec((tt, gn), lambda b, d, c: (blk(b, d, c), SSM_W // gn + 1)),
                pl.BlockSpec((tt, LANES), lambda b, d, c: (blk(b, d, c), 0)),
                pl.BlockSpec((LANES, tt), lambda b, d, c: (0, blk(b, d, c))),
                st_spec]
    args = [xbc, xst, xbc, xbc, dtc, dtr, init]
    aliases = {}
    if prev is not None:
        in_specs.append(pl.BlockSpec(memory_space=pl.ANY))
        args.append(prev)
        aliases = {7: 0}
    return pl.pallas_call(
        functools.partial(_ssd_kernel, n_chunks=nc, aliased=prev is not None),
        grid=(n_seq, 2, nc),
        in_specs=in_specs,
        out_specs=[pl.BlockSpec((1, tt, SSM_W), lambda b, d, c: (d, blk(b, d, c), 0)), st_spec],
        out_shape=[jax.ShapeDtypeStruct((2, t, SSM_W), F32),
                   jax.ShapeDtypeStruct((n_seq, 2) + st_shape, F32)],
        scratch_shapes=[pltpu.VMEM(st_shape, F32)],
        input_output_aliases=aliases,
        compiler_params=_cparams("parallel", "parallel", "arbitrary"),
        name="ssd",
    )(*args)


def _ssd_gate_kernel(y_ref, xs_ref, zlo_ref, zhi_ref, d_ref, g_ref, o_ref):
    z = jnp.concatenate([zlo_ref[...], zhi_ref[...]], axis=1)
    y = (y_ref[0] + y_ref[1] + d_ref[...] * xs_ref[...]) * (z * jax.nn.sigmoid(z))
    y = y * lax.rsqrt(jnp.mean(y * y, axis=-1, keepdims=True) + RMS_EPS) * g_ref[...]
    o_ref[...] = y.astype(o_ref.dtype)


def ssd_gate(y2, xbc, u, ssm_d, ssm_norm):
    t = xbc.shape[0]
    half = SSM_W // 2
    zb = Z_OFF // half
    vec = pl.BlockSpec((1, SSM_W), lambda i: (0, 0))
    return pl.pallas_call(
        _ssd_gate_kernel,
        grid=(t // ROWS,),
        in_specs=[pl.BlockSpec((2, ROWS, SSM_W), lambda i: (0, i, 0)),
                  pl.BlockSpec((ROWS, SSM_W), lambda i: (i, 0)),
                  pl.BlockSpec((ROWS, half), lambda i: (i, zb)),
                  pl.BlockSpec((ROWS, half), lambda i: (i, zb + 1)),
                  vec, vec],
        out_specs=pl.BlockSpec((ROWS, SSM_W), lambda i: (i, 0)),
        out_shape=jax.ShapeDtypeStruct((t, SSM_W), BF16),
        compiler_params=_cparams("parallel"),
        name="ssd_gate",
    )(y2, xbc, u, u, jnp.repeat(ssm_d, SSM_HEADDIM).reshape(1, SSM_W), ssm_norm.reshape(1, SSM_W))


def dft_matrices(length):
    blk = 64
    two_l = 2 * length
    k = jnp.arange(length, dtype=jnp.int32)[:, None]
    a = jnp.arange(length // blk, dtype=jnp.int32)[None, :]
    b = jnp.arange(blk, dtype=jnp.int32)[None, :]
    xa = ((k * (a * blk)) % two_l).astype(F32) * (math.pi / length)
    xb = ((k * b) % two_l).astype(F32) * (math.pi / length)
    ca, sa, cb, sb = jnp.cos(xa)[:, :, None], jnp.sin(xa)[:, :, None], jnp.cos(xb)[:, None, :], jnp.sin(xb)[:, None, :]
    cos = (ca * cb - sa * sb).reshape(length, length)
    sin = (sa * cb + ca * sb).reshape(length, length)
    idx = jnp.arange(length, dtype=jnp.int32)
    alt = jnp.where(idx % 2 == 0, 1.0, -1.0).astype(F32)
    sin_fwd = jnp.where(idx[:, None] == 0, alt[None, :], sin)
    sin_inv = jnp.where(idx[None, :] == 0, alt[:, None], sin)
    return jnp.stack([cos, sin_fwd]).astype(BF16), jnp.stack([cos, sin_inv]).astype(BF16)


def _hyfilt_kernel(bands_ref, w1_ref, b1_ref, fr_ref, w2_ref, b2_ref, wf_ref, wb_ref, bf_ref, bb_ref, dl_ref,
                   o_ref, nrm_ref, nyq_ref, *, length, tr):
    i = pl.program_id(2)
    t_idx = lax.broadcasted_iota(jnp.int32, (tr, 1), 0) + i * tr
    t = t_idx.astype(F32) / length
    lane = lax.broadcasted_iota(jnp.int32, (tr, LANES), 1)
    ang = (2.0 * math.pi * t) * bands_ref[...]
    feats = jnp.where(lane < HY_BANDS, jnp.cos(ang),
                      jnp.where(lane < 2 * HY_BANDS, jnp.sin(ang), jnp.where(lane == 2 * HY_BANDS, t, 0.0)))
    fr = fr_ref[...]
    hid = jnp.sin(fr * (jnp.dot(feats.astype(BF16), w1_ref[...].astype(BF16),
                                preferred_element_type=F32) + b1_ref[...]))
    hid = jnp.sin(fr * (jnp.dot(hid.astype(BF16), w2_ref[...].astype(BF16),
                                preferred_element_type=F32) + b2_ref[...])).astype(BF16)
    win = jnp.exp(-t * dl_ref[...])
    hf = (jnp.dot(hid, wf_ref[...].astype(BF16), preferred_element_type=F32) + bf_ref[...]) * win
    hb = (jnp.dot(hid, wb_ref[...].astype(BF16), preferred_element_type=F32) + bb_ref[...]) * win
    hb = jnp.where(t_idx == 0, 0.0, hb)
    o_ref[0] = (hf + hb).astype(o_ref.dtype)
    o_ref[1] = (hf - hb).astype(o_ref.dtype)
    sign = jnp.where(t_idx % 2 == 0, 1.0, -1.0)

    @pl.when(i == 0)
    def _():
        nrm_ref[...] = jnp.zeros_like(nrm_ref)
        nyq_ref[...] = jnp.zeros_like(nyq_ref)

    nrm_ref[...] += jnp.sum(jnp.abs(hf) + jnp.abs(hb), axis=0, keepdims=True)
    nyq_ref[...] += jnp.sum(sign * (hf + hb), axis=0, keepdims=True)


def hyena_filters(length, p, fwd):
    bands = jnp.linspace(1e-4, HY_BANDS - 1, HY_BANDS, dtype=F32)
    bands = jnp.zeros((1, LANES), F32).at[0, :2 * HY_BANDS].set(jnp.concatenate([bands, bands]))
    w1 = p['hy_w1']
    ffn = w1.shape[1]
    w1p = jnp.zeros((LANES, ffn), F32).at[:2 * HY_BANDS].set(w1[1:]).at[2 * HY_BANDS].set(w1[0])
    deltas = jnp.abs(jnp.linspace(math.log(HY_DECAY_TARGET) / HY_SLOW_DECAY,
                                  math.log(HY_DECAY_TARGET) / HY_FAST_DECAY, HY_W, dtype=F32)).reshape(1, HY_W)
    tr, tc = _pick(length, 256), 512
    nj = HY_W // tc
    w3, b3 = p['hy_w3'], p['hy_b3'].reshape(1, -1)
    ow = HY_ORDER * HY_W
    full = lambda shape: pl.BlockSpec(shape, lambda n, j, i: (0, 0))
    hsd, nrm, nyq = pl.pallas_call(
        functools.partial(_hyfilt_kernel, length=length, tr=tr),
        grid=(HY_ORDER, nj, length // tr),
        in_specs=[full((1, LANES)), full((LANES, ffn)), full((1, ffn)), full((1, ffn)), full((ffn, ffn)),
                  full((1, ffn)),
                  pl.BlockSpec((ffn, tc), lambda n, j, i: (0, (2 * n) * nj + j)),
                  pl.BlockSpec((ffn, tc), lambda n, j, i: (0, (2 * n + 1) * nj + j)),
                  pl.BlockSpec((1, tc), lambda n, j, i: (0, (2 * n) * nj + j)),
                  pl.BlockSpec((1, tc), lambda n, j, i: (0, (2 * n + 1) * nj + j)),
                  pl.BlockSpec((1, tc), lambda n, j, i: (0, j))],
        out_specs=[pl.BlockSpec((2, tr, tc), lambda n, j, i: (0, i, n * nj + j)),
                   pl.BlockSpec((1, tc), lambda n, j, i: (0, n * nj + j)),
                   pl.BlockSpec((1, tc), lambda n, j, i: (0, n * nj + j))],
        out_shape=[jax.ShapeDtypeStruct((2, length, ow), BF16),
                   jax.ShapeDtypeStruct((1, ow), F32), jax.ShapeDtypeStruct((1, ow), F32)],
        compiler_params=_cparams("parallel", "parallel", "arbitrary"),
        name="hyena_filter",
    )(bands, w1p, p['hy_b1'].reshape(1, ffn), p['hy_freq'].reshape(1, ffn), p['hy_w2'],
      p['hy_b2'].reshape(1, ffn), w3, w3, b3, b3, deltas)
    pq = gmm(fwd, hsd)
    return pq, nrm, nyq


def _dft_fwd_kernel(f_ref, z_ref, pq_ref, nrm_ref, nyq_ref, uv_ref, *, length, tm):
    i = pl.program_id(0)
    z = z_ref[...]
    a = jnp.dot(f_ref[0], z, preferred_element_type=F32)
    b = jnp.dot(f_ref[1], z, preferred_element_type=F32)
    k_idx = lax.broadcasted_iota(jnp.int32, (tm, 1), 0) + i * tm
    is0 = k_idx == 0
    wk = jnp.where(is0, 1.0, 2.0) * (0.5 / length) / nrm_ref[...]
    pp, qq = pq_ref[0], pq_ref[1]
    uv_ref[0, 0] = (wk * (a * pp - jnp.where(is0, 0.0, b * qq))).astype(uv_ref.dtype)
    uv_ref[0, 1] = (wk * jnp.where(is0, b * nyq_ref[...], a * qq + b * pp)).astype(uv_ref.dtype)


def dft_fwd(fwd, z, pq, nrm, nyq, order, row0, n_seq, length):
    tm, tn = _pick(length, 512), 512
    nj = HY_W // tn
    r0 = row0 // length
    vec = pl.BlockSpec((1, tn), lambda i, b, j: (0, order * nj + j))
    return pl.pallas_call(
        functools.partial(_dft_fwd_kernel, length=length, tm=tm),
        grid=(length // tm, n_seq, nj),
        in_specs=[pl.BlockSpec((2, tm, length), lambda i, b, j: (0, i, 0)),
                  pl.BlockSpec((length, tn), lambda i, b, j: (r0 + b, j)),
                  pl.BlockSpec((2, tm, tn), lambda i, b, j: (0, i, order * nj + j)),
                  vec, vec],
        out_specs=pl.BlockSpec((1, 2, tm, tn), lambda i, b, j: (b, 0, i, j)),
        out_shape=jax.ShapeDtypeStruct((n_seq, 2, length, HY_W), BF16),
        compiler_params=_cparams("parallel", "parallel", "parallel"),
        name="hyena_dft_fwd",
    )(fwd, z, pq, nrm, nyq)


def _hy_inv_kernel(*refs, n_out):
    f_ref, uv_ref, z_ref, gate_ref, bias_ref = refs[:5]
    o_refs = refs[-n_out:]
    y = jnp.dot(f_ref[0], uv_ref[0, 0], preferred_element_type=F32)
    y += jnp.dot(f_ref[1], uv_ref[0, 1], preferred_element_type=F32)
    y = gate_ref[...] * (y + bias_ref[...] * z_ref[...])
    for o_ref in o_refs:
        o_ref[...] = y.astype(o_ref.dtype)


def hy_inverse(inv, uv, z, zcol0, gate, gcol0, bias, row0, length, out_dtypes, prev=None):
    t = z.shape[0]
    n_seq = uv.shape[0]
    tm, tn = _pick(length, 512), 512
    r0 = row0 // tm
    ni = length // tm
    rmap = lambda c0: (lambda i, b, j: (r0 + b * ni + i, c0 // tn + j))
    in_specs = [pl.BlockSpec((2, tm, length), lambda i, b, j: (0, i, 0)),
                pl.BlockSpec((1, 2, length, tn), lambda i, b, j: (b, 0, 0, j)),
                pl.BlockSpec((tm, tn), rmap(zcol0)),
                pl.BlockSpec((tm, tn), rmap(gcol0)),
                pl.BlockSpec((1, tn), lambda i, b, j: (0, j))]
    args = [inv, uv, z, gate, bias.reshape(1, HY_W)]
    aliases = {}
    if prev is not None:
        for n, pv in enumerate(prev):
            in_specs.append(pl.BlockSpec(memory_space=pl.ANY))
            args.append(pv)
            aliases[5 + n] = n
    return pl.pallas_call(
        functools.partial(_hy_inv_kernel, n_out=len(out_dtypes)),
        grid=(ni, n_seq, HY_W // tn),
        in_specs=in_specs,
        out_specs=[pl.BlockSpec((tm, tn), rmap(0)) for _ in out_dtypes],
        out_shape=[jax.ShapeDtypeStruct((t, HY_W), dt) for dt in out_dtypes],
        input_output_aliases=aliases,
        compiler_params=_cparams("parallel", "parallel", "parallel"),
        name="hyena_dft_inv",
    )(*args)


def hyena_group(v32, v16, x12, filt, mats, p, row0, n_seq, length, prev):
    fwd, inv = mats
    pq, nrm, nyq = filt
    prev1, prev2 = prev
    uv = dft_fwd(fwd, v16, pq, nrm, nyq, 0, row0, n_seq, length)
    z1 = hy_inverse(inv, uv, v32, 0, x12, 0, p['hy_bias'][0], row0, length, (F32, BF16), prev1)
    uv = dft_fwd(fwd, z1[1], pq, nrm, nyq, 1, row0, n_seq, length)
    z2 = hy_inverse(inv, uv, z1[0], 0, x12, HY_W, p['hy_bias'][1], row0, length, (BF16,), prev2)
    return z1, z2


def _merge_kernel(a_ref, h_ref, s_ref, wa_ref, wh_ref, ws_ref, ga_ref, gh_ref, gs_ref, o_ref):
    acc = ga_ref[...].astype(F32) * jnp.dot(a_ref[...], wa_ref[...], preferred_element_type=F32)
    acc += gh_ref[...].astype(F32) * jnp.dot(h_ref[...], wh_ref[...], preferred_element_type=F32)
    acc += gs_ref[...].astype(F32) * jnp.dot(s_ref[...], ws_ref[...], preferred_element_type=F32)
    o_ref[...] = acc.astype(o_ref.dtype)


def branch_merge(att, hy, ssm, wa, wh, ws, gate_logits, tm=512, tn=512):
    t, kd = att.shape
    d = wa.shape[1]
    nj = d // tn
    xs = pl.BlockSpec((tm, kd), lambda j, i: (i, 0))
    ws_ = pl.BlockSpec((kd, tn), lambda j, i: (0, j))
    gs = lambda b: pl.BlockSpec((tm, tn), lambda j, i: (i, b * nj + j))
    return pl.pallas_call(
        _merge_kernel,
        grid=(nj, t // tm),
        in_specs=[xs, xs, xs, ws_, ws_, ws_, gs(0), gs(1), gs(2)],
        out_specs=pl.BlockSpec((tm, tn), lambda j, i: (i, j)),
        out_shape=jax.ShapeDtypeStruct((t, d), BF16),
        compiler_params=_cparams("parallel", "parallel"),
        name="branch_merge",
    )(att, hy, ssm, wa, wh, ws, gate_logits, gate_logits, gate_logits)


def _swiglu_kernel(x_ref, wg_ref, wu_ref, o_ref):
    x = x_ref[0]
    g = jnp.dot(x, wg_ref[0].astype(BF16), preferred_element_type=F32)
    u = jnp.dot(x, wu_ref[0].astype(BF16), preferred_element_type=F32)
    o_ref[0] = (g * jax.nn.sigmoid(g) * u).astype(o_ref.dtype)


def expert_swiglu(xs, w_gate, w_up, layer, tn=256):
    e, m, d = xs.shape
    f = w_gate.shape[3]
    wspec = pl.BlockSpec((None, 1, d, tn), lambda e, j: (layer, e, 0, j))
    return pl.pallas_call(
        _swiglu_kernel,
        grid=(e, f // tn),
        in_specs=[pl.BlockSpec((1, m, d), lambda e, j: (e, 0, 0)), wspec, wspec],
        out_specs=pl.BlockSpec((1, m, tn), lambda e, j: (e, 0, j)),
        out_shape=jax.ShapeDtypeStruct((e, m, f), BF16),
        compiler_params=_cparams("parallel", "parallel"),
        name="expert_swiglu",
    )(xs, w_gate, w_up)


def _down_kernel(h_ref, w_ref, g_ref, o_ref):
    o_ref[0] = jnp.dot(h_ref[0], w_ref[0].astype(BF16), preferred_element_type=F32) * g_ref[0]


def expert_down(hid, w_down, gates, layer, tn=512):
    e, m, f = hid.shape
    d = w_down.shape[3]
    return pl.pallas_call(
        _down_kernel,
        grid=(e, d // tn),
        in_specs=[pl.BlockSpec((1, m, f), lambda e, j: (e, 0, 0)),
                  pl.BlockSpec((None, 1, f, tn), lambda e, j: (layer, e, 0, j)),
                  pl.BlockSpec((1, m, 1), lambda e, j: (e, 0, 0))],
        out_specs=pl.BlockSpec((1, m, tn), lambda e, j: (e, 0, j)),
        out_shape=jax.ShapeDtypeStruct((e, m, d), F32),
        compiler_params=_cparams("parallel", "parallel"),
        name="expert_down",
    )(hid, w_down, gates)


def ec_moe(xm, p, lay):
    t, d = xm.shape
    w_r = jnp.zeros((d, LANES), BF16).at[:, :N_EXPERTS].set(p['w_router'].astype(BF16))
    logits = mm(xm, w_r, tn=LANES)[:, :N_EXPERTS]
    aff = jax.nn.softmax(logits, axis=-1)
    gates, rows = [], []
    for row0, n_seq, length in ((0, lay.n_p, lay.l_p), (lay.tp, lay.n_s, lay.l_s)):
        cap = EC_CAPACITY * length // N_EXPERTS
        a = aff[row0:row0 + n_seq * length].reshape(n_seq, length, N_EXPERTS)
        g, idx = lax.top_k(jnp.swapaxes(a, 1, 2), cap)
        idx = idx + (row0 + jnp.arange(n_seq, dtype=idx.dtype) * length)[:, None, None]
        gates.append(jnp.swapaxes(g, 0, 1).reshape(N_EXPERTS, n_seq * cap))
        rows.append(jnp.swapaxes(idx, 0, 1).reshape(N_EXPERTS, n_seq * cap))
    gates = jnp.concatenate(gates, axis=1)
    rows = jnp.concatenate(rows, axis=1)
    xs = xm.at[rows.reshape(-1)].get(mode='promise_in_bounds').reshape(N_EXPERTS, -1, d)
    hid = expert_swiglu(xs, p['w_gate'], p['w_up'], p['layer'])
    y = expert_down(hid, p['w_down'], gates[..., None], p['layer'])
    return jnp.zeros((t, d), F32).at[rows.reshape(-1)].add(y.reshape(-1, d), mode='promise_in_bounds')


def rope_tables(lay):
    pos = jnp.arange(lay.l_s)
    row = (pos // GRID_W).astype(F32)
    col = (pos % GRID_W).astype(F32)
    inv = ROPE_THETA ** (-jnp.arange(ROT_FREQS, dtype=F32) / ROT_FREQS)
    ang = jnp.concatenate([row[:, None] * inv] * 2 + [col[:, None] * inv] * 2, axis=1)
    sign = jnp.where((jnp.arange(HEAD_DIM) % (2 * ROT_FREQS)) < ROT_FREQS, -1.0, 1.0).astype(F32)
    cos_s, sin_s = jnp.cos(ang), jnp.sin(ang) * sign
    cos_t = jnp.concatenate([jnp.ones((lay.tp, HEAD_DIM), F32)] + [cos_s] * lay.n_s, axis=0)
    sin_t = jnp.concatenate([jnp.zeros((lay.tp, HEAD_DIM), F32)] + [sin_s] * lay.n_s, axis=0)
    return cos_t, sin_t


def trunk_layer(x, xm, p, mod, mod_next, consts, lay, cache_k, cache_v, state_ssm):
    cos_t, sin_t, mats_p, mats_s = consts
    tp = lay.tp
    u = mm_cols(xm, p['w_in'], p['layer'], 0, DT_OFF, tm=1024, tn=512)
    dt_raw = mm_cols(xm, p['w_in'], p['layer'], DT_OFF, LANES, tn=LANES)
    gates = mm_cols(xm, p['w_in_gate'], None, 0, N_BRANCH * D_MODEL, out_dtype=BF16, sigmoid=True,
                    tm=1024, tn=512)

    q, kr, vb, kf, vf = qkv_prep(u, cos_t, sin_t, p['q_norm'], p['k_norm'])
    k_p = kr[:tp].reshape(lay.n_p, lay.l_p, KV_W)
    v_p = vb[:tp].reshape(lay.n_p, lay.l_p, KV_W)
    k_s = jnp.concatenate([cache_k.reshape(lay.n_s, -1, KV_W).astype(BF16),
                           kr[tp:].reshape(lay.n_s, lay.l_s, KV_W)], axis=1)
    v_s = jnp.concatenate([cache_v.reshape(lay.n_s, -1, KV_W).astype(BF16),
                           vb[tp:].reshape(lay.n_s, lay.l_s, KV_W)], axis=1)
    att = attention(q, k_p, v_p, 0, lay.n_p, lay.l_p, lay.l_p)
    att = attention(q, k_s, v_s, tp, lay.n_s, lay.l_s, ROWS, prev=att)

    v32, v16 = conv3(u, HY_OFF, HY_W, p['hy_conv_w'][:, :HY_W], p['hy_conv_b'][:HY_W], lay, silu=False,
                     out_dtypes=(F32, BF16))
    x12 = conv3(u, HY_OFF + HY_W, 2 * HY_W, p['hy_conv_w'][:, HY_W:], p['hy_conv_b'][HY_W:], lay, silu=False)
    filt_p = hyena_filters(lay.l_p, p, mats_p[0])
    filt_s = hyena_filters(lay.l_s, p, mats_s[0])
    z1, hy = hyena_group(v32, v16, x12, filt_p, mats_p, p, 0, lay.n_p, lay.l_p, (None, None))
    _, hy = hyena_group(v32, v16, x12, filt_s, mats_s, p, tp, lay.n_s, lay.l_s, (z1, hy))
    hy = hy[0]

    xbc = conv3(u, XBC_OFF, SSM_CONV_DIM, p['ssm_conv_w'], p['ssm_conv_b'], lay, silu=True)
    dtc = dt_prep(dt_raw, p['ssm_dt_bias'], p['ssm_a_log'])
    dtr = dtc.T
    zero_state = jnp.zeros((lay.n_p, 2, SSM_HEADS, SSM_HEADDIM, SSM_STATE), F32)
    xst = xbc[:, :SSM_W].T
    y2, states = ssd(xbc, xst, dtc, dtr, zero_state, 0, lay.n_p, lay.l_p)
    y2, _ = ssd(xbc, xst, dtc, dtr, state_ssm, tp, lay.n_s, lay.l_s, prev=y2)
    ssm = ssd_gate(y2, xbc, u, p['ssm_d'], p['ssm_norm'])

    merged = branch_merge(att, hy, ssm, p['w_br_att'], p['w_br_hy'], p['w_br_ssm'], gates)
    m = mm(merged, p['w_out'])
    x1, xm2 = ln_mod(x, m, mod, 2, p['ln1_g'], p['ln1_b'], lay, mod_next=mod, sec_sc=4, sec_sh=3)
    f = ec_moe(xm2, p, lay)
    x2, xm_next = ln_mod(x1, f, mod, 5, p['ln2_g'], p['ln2_b'], lay, mod_next=mod_next, sec_sc=1, sec_sh=0)
    new_k = kf[:tp].reshape(lay.n_p, lay.l_p, N_KV_HEADS, HEAD_DIM)
    new_v = vf[:tp].reshape(lay.n_p, lay.l_p, N_KV_HEADS, HEAD_DIM)
    return x2, xm_next, (new_k, new_v, states)


def kernel(x_prompt, x_sample, cache_k, cache_v, state_ssm, c, c_ctx, w_mod, b_mod, w_in, q_norm, k_norm, hy_conv_w, hy_conv_b, hy_w1, hy_b1, hy_freq, hy_w2, hy_b2, hy_w3, hy_b3, hy_bias, ssm_conv_w, ssm_conv_b, ssm_dt_bias, ssm_a_log, ssm_d, ssm_norm, w_br_att, w_br_hy, w_br_ssm, w_out, ln1_g, ln1_b, w_router, w_gate, w_up, w_down, ln2_g, ln2_b):
    n_p, l_p, d = x_prompt.shape
    n_s, l_s, _ = x_sample.shape
    depth = w_in.shape[0]
    lay = Layout(n_p, l_p, n_s, l_s)
    x = jnp.concatenate([x_prompt.reshape(lay.tp, d), x_sample.reshape(lay.ts, d)], axis=0)

    cond = jnp.zeros((N_COND_PAD, d), F32).at[0].set(c_ctx).at[1:1 + n_s].set(c)
    act = (cond * jax.nn.sigmoid(cond)).astype(BF16)
    mod_all = gmm(act[None], w_mod, tm=N_COND_PAD, tn=2048, tk=1024, share_x=True) + b_mod[:, None, :]
    mods = [mod_all[l].reshape(N_COND_PAD, 1, 6 * d) for l in range(depth)]

    consts = rope_tables(lay) + (dft_matrices(l_p), dft_matrices(l_s))
    xm = modulate(x, mods[0], 1, 0, lay)
    w_in_bf = w_in.astype(BF16)
    new_k, new_v, new_s = [], [], []
    for l in range(depth):
        p = dict(w_in=w_in_bf, w_in_gate=w_in_bf[l, :, GATE_OFF:],
                 q_norm=q_norm[l], k_norm=k_norm[l],
                 hy_conv_w=hy_conv_w[l], hy_conv_b=hy_conv_b[l], hy_w1=hy_w1[l], hy_b1=hy_b1[l],
                 hy_freq=hy_freq[l], hy_w2=hy_w2[l], hy_b2=hy_b2[l], hy_w3=hy_w3[l], hy_b3=hy_b3[l],
                 hy_bias=hy_bias[l], ssm_conv_w=ssm_conv_w[l], ssm_conv_b=ssm_conv_b[l],
                 ssm_dt_bias=ssm_dt_bias[l], ssm_a_log=ssm_a_log[l], ssm_d=ssm_d[l], ssm_norm=ssm_norm[l],
                 w_br_att=w_br_att[l].astype(BF16), w_br_hy=w_br_hy[l].astype(BF16),
                 w_br_ssm=w_br_ssm[l].astype(BF16), w_out=w_out[l].astype(BF16),
                 ln1_g=ln1_g[l], ln1_b=ln1_b[l], w_router=w_router[l],
                 w_gate=w_gate, w_up=w_up, w_down=w_down, layer=l,
                 ln2_g=ln2_g[l], ln2_b=ln2_b[l])
        mod_next = mods[l + 1] if l + 1 < depth else None
        x, xm, (k_l, v_l, s_l) = trunk_layer(x, xm, p, mods[l], mod_next, consts, lay,
                                             cache_k[:, l], cache_v[:, l], state_ssm[:, l])
        new_k.append(k_l)
        new_v.append(v_l)
        new_s.append(s_l)
    y_prompt = x[:lay.tp].reshape(n_p, l_p, d)
    y_sample = x[lay.tp:].reshape(n_s, l_s, d)
    return (y_prompt, y_sample, jnp.stack(new_k, axis=1), jnp.stack(new_v, axis=1), jnp.stack(new_s, axis=1))
```

```python
import functools
import math

import jax
import jax.numpy as jnp
from jax import lax
from jax.experimental import pallas as pl
from jax.experimental.pallas import tpu as pltpu

F32 = jnp.float32
BF16 = jnp.bfloat16

D_MODEL = 4096
DEPTH = 2
GRID_W = 64
N_HEADS = 16
N_KV_HEADS = 4
KV_REP = N_HEADS // N_KV_HEADS
HEAD_DIM = 128
ATT_W = N_HEADS * HEAD_DIM
KV_W = N_KV_HEADS * HEAD_DIM
ROT_FREQS = HEAD_DIM // 4
ROPE_THETA = 10000.0
HY_W = 2048
HY_ORDER = 2
HY_BANDS = 16
HY_DECAY_TARGET = 1e-2
HY_FAST_DECAY = 0.3
HY_SLOW_DECAY = 1.5
SSM_W = 2048
SSM_HEADDIM = 64
SSM_HEADS = SSM_W // SSM_HEADDIM
SSM_GROUPS = 8
SSM_REP = SSM_HEADS // SSM_GROUPS
SSM_STATE = 128
SSM_CHUNK = 128
SSM_CONV_DIM = SSM_W + 2 * SSM_GROUPS * SSM_STATE
N_EXPERTS = 16
EC_CAPACITY = 2
MOE_FF = 2048
N_BRANCH = 3
Q_OFF = 0
K_OFF = Q_OFF + ATT_W
V_OFF = K_OFF + KV_W
HY_OFF = V_OFF + KV_W
Z_OFF = HY_OFF + 3 * HY_W
XBC_OFF = Z_OFF + SSM_W
DT_OFF = XBC_OFF + SSM_CONV_DIM
GATE_OFF = DT_OFF + 2 * SSM_HEADS
ALPHA = (2 * DEPTH) ** 0.25
LN_EPS = 1e-5
RMS_EPS = 1e-6
N_COND_PAD = 8
LANES = 128
SUBLANES = 8

VMEM_LIMIT_BYTES = 56 * 1024 * 1024

NT_DIMS = (((1,), (1,)), ((), ()))
TN_DIMS = (((0,), (0,)), ((), ()))


def _cparams(*sem):
    return pltpu.CompilerParams(dimension_semantics=sem, vmem_limit_bytes=VMEM_LIMIT_BYTES)


def _pick(dim, pref):
    t = min(dim, pref)
    while dim % t:
        t //= 2
    return t


def _mm_kernel(x_ref, w_ref, o_ref, acc_ref):
    k = pl.program_id(3)

    @pl.when(k == 0)
    def _():
        acc_ref[...] = jnp.zeros_like(acc_ref)

    acc_ref[...] += jnp.dot(x_ref[0].astype(BF16), w_ref[0].astype(BF16), preferred_element_type=F32)

    @pl.when(k == pl.num_programs(3) - 1)
    def _():
        o_ref[0] = acc_ref[...].astype(o_ref.dtype)


def _mm_fullk_kernel(x_ref, w_ref, o_ref):
    o_ref[0] = jnp.dot(x_ref[0].astype(BF16), w_ref[0].astype(BF16),
                       preferred_element_type=F32).astype(o_ref.dtype)


def gmm(x, w, out_dtype=F32, tm=1024, tn=512, tk=4096, share_x=False):
    g, kd, n = w.shape
    m = x.shape[1]
    tm, tn, tk = _pick(m, tm), _pick(n, tn), _pick(kd, tk)
    if tk == kd:
        xmap = (lambda e, i, j: (0, i, 0)) if share_x else (lambda e, i, j: (e, i, 0))
        return pl.pallas_call(
            _mm_fullk_kernel,
            grid=(g, m // tm, n // tn),
            in_specs=[pl.BlockSpec((1, tm, kd), xmap),
                      pl.BlockSpec((1, kd, tn), lambda e, i, j: (e, 0, j))],
            out_specs=pl.BlockSpec((1, tm, tn), lambda e, i, j: (e, i, j)),
            out_shape=jax.ShapeDtypeStruct((g, m, n), out_dtype),
            compiler_params=_cparams("parallel", "parallel", "parallel"),
            name="gmm",
        )(x, w)
    xmap = (lambda e, i, j, k: (0, i, k)) if share_x else (lambda e, i, j, k: (e, i, k))
    return pl.pallas_call(
        _mm_kernel,
        grid=(g, m // tm, n // tn, kd // tk),
        in_specs=[pl.BlockSpec((1, tm, tk), xmap),
                  pl.BlockSpec((1, tk, tn), lambda e, i, j, k: (e, k, j))],
        out_specs=pl.BlockSpec((1, tm, tn), lambda e, i, j, k: (e, i, j)),
        out_shape=jax.ShapeDtypeStruct((g, m, n), out_dtype),
        scratch_shapes=[pltpu.VMEM((tm, tn), F32)],
        compiler_params=_cparams("parallel", "parallel", "parallel", "arbitrary"),
        name="gmm",
    )(x, w)


def mm(x, w, out_dtype=F32, **kw):
    return gmm(x[None], w[None], out_dtype, **kw)[0]


def _mm_nt_kernel(x_ref, wt_ref, o_ref, *, sigmoid):
    y = lax.dot_general(x_ref[...], wt_ref[...].astype(BF16), NT_DIMS, preferred_element_type=F32)
    if sigmoid:
        y = jax.nn.sigmoid(y)
    o_ref[...] = y.astype(o_ref.dtype)


def mm_rows_t(x, wt, layer, row0, n_rows, out_dtype=F32, sigmoid=False, tm=2048, tn=256):
    m, kd = x.shape
    tm, tn = _pick(m, tm), _pick(n_rows, tn)
    r0 = row0 // tn
    if layer is None:
        wspec = pl.BlockSpec((tn, kd), lambda i, j: (r0 + j, 0))
    else:
        wspec = pl.BlockSpec((None, tn, kd), lambda i, j: (layer, r0 + j, 0))
    return pl.pallas_call(
        functools.partial(_mm_nt_kernel, sigmoid=sigmoid),
        grid=(m // tm, n_rows // tn),
        in_specs=[pl.BlockSpec((tm, kd), lambda i, j: (i, 0)), wspec],
        out_specs=pl.BlockSpec((tm, tn), lambda i, j: (i, j)),
        out_shape=jax.ShapeDtypeStruct((m, n_rows), out_dtype),
        compiler_params=_cparams("parallel", "parallel"),
        name="mm_rows_t",
    )(x, wt)


def _mm_act_kernel(x_ref, w_ref, o_ref, *, sigmoid):
    y = jnp.dot(x_ref[...], w_ref[...].astype(BF16), preferred_element_type=F32)
    if sigmoid:
        y = jax.nn.sigmoid(y)
    o_ref[...] = y.astype(o_ref.dtype)


def mm_cols(x, w, layer, col0, n_cols, out_dtype=F32, sigmoid=False, tm=2048, tn=256):
    m, kd = x.shape
    tm, tn = _pick(m, tm), _pick(n_cols, tn)
    c0 = col0 // tn
    if layer is None:
        wspec = pl.BlockSpec((kd, tn), lambda i, j: (0, c0 + j))
    else:
        wspec = pl.BlockSpec((None, kd, tn), lambda i, j: (layer, 0, c0 + j))
    return pl.pallas_call(
        functools.partial(_mm_act_kernel, sigmoid=sigmoid),
        grid=(m // tm, n_cols // tn),
        in_specs=[pl.BlockSpec((tm, kd), lambda i, j: (i, 0)), wspec],
        out_specs=pl.BlockSpec((tm, tn), lambda i, j: (i, j)),
        out_shape=jax.ShapeDtypeStruct((m, n_cols), out_dtype),
        compiler_params=_cparams("parallel", "parallel"),
        name="mm_cols",
    )(x, w)


class Layout:
    def __init__(self, n_p, l_p, n_s, l_s):
        self.n_p, self.l_p, self.n_s, self.l_s = n_p, l_p, n_s, l_s
        self.tp = n_p * l_p
        self.ts = n_s * l_s
        self.t = self.tp + self.ts

    def group_of_block(self, i, rows):
        bp = self.tp // rows
        return jnp.where(i < bp, 0, 1 + (i - bp) // (self.l_s // rows))

    def seq_edges(self, i, rows):
        bp = self.tp // rows
        per_p, per_s = self.l_p // rows, self.l_s // rows
        first = jnp.where(i < bp, i % per_p == 0, (i - bp) % per_s == 0)
        last = jnp.where(i < bp, i % per_p == per_p - 1, (i - bp) % per_s == per_s - 1)
        return first, last


ROWS = 256


def _modulate_kernel(x_ref, sc_ref, sh_ref, o_ref):
    o_ref[...] = (x_ref[...] * (1.0 + sc_ref[0]) + sh_ref[0]).astype(o_ref.dtype)


def modulate(x, mod, sec_sc, sec_sh, lay):
    t, d = x.shape
    grp = lambda i: lay.group_of_block(i, ROWS)
    return pl.pallas_call(
        _modulate_kernel,
        grid=(t // ROWS,),
        in_specs=[pl.BlockSpec((ROWS, d), lambda i: (i, 0)),
                  pl.BlockSpec((1, 1, d), lambda i: (grp(i), 0, sec_sc)),
                  pl.BlockSpec((1, 1, d), lambda i: (grp(i), 0, sec_sh))],
        out_specs=pl.BlockSpec((ROWS, d), lambda i: (i, 0)),
        out_shape=jax.ShapeDtypeStruct((t, d), BF16),
        compiler_params=_cparams("parallel"),
        name="modulate",
    )(x, mod, mod)


def _ln_mod_kernel(*refs, with_mod):
    if with_mod:
        x_ref, m_ref, gt_ref, lg_ref, lb_ref, sc_ref, sh_ref, o_ref, om_ref = refs
    else:
        x_ref, m_ref, gt_ref, lg_ref, lb_ref, o_ref = refs
    r = ALPHA * x_ref[...] + gt_ref[0] * m_ref[...]
    mu = jnp.mean(r, axis=-1, keepdims=True)
    dlt = r - mu
    var = jnp.mean(dlt * dlt, axis=-1, keepdims=True)
    y = dlt * lax.rsqrt(var + LN_EPS) * lg_ref[...] + lb_ref[...]
    o_ref[...] = y
    if with_mod:
        om_ref[...] = (y * (1.0 + sc_ref[0]) + sh_ref[0]).astype(om_ref.dtype)


def ln_mod(x, m, mod, sec_gate, ln_g, ln_b, lay, mod_next=None, sec_sc=0, sec_sh=0):
    t, d = x.shape
    rows = ROWS // 2
    grp = lambda i: lay.group_of_block(i, rows)
    with_mod = mod_next is not None
    row_spec = pl.BlockSpec((rows, d), lambda i: (i, 0))
    vec_spec = pl.BlockSpec((1, d), lambda i: (0, 0))
    in_specs = [row_spec, row_spec,
                pl.BlockSpec((1, 1, d), lambda i: (grp(i), 0, sec_gate)), vec_spec, vec_spec]
    args = [x, m, mod, ln_g.reshape(1, d), ln_b.reshape(1, d)]
    out_specs = [row_spec]
    out_shape = [jax.ShapeDtypeStruct((t, d), F32)]
    if with_mod:
        in_specs += [pl.BlockSpec((1, 1, d), lambda i: (grp(i), 0, sec_sc)),
                     pl.BlockSpec((1, 1, d), lambda i: (grp(i), 0, sec_sh))]
        args += [mod_next, mod_next]
        out_specs.append(row_spec)
        out_shape.append(jax.ShapeDtypeStruct((t, d), BF16))
    res = pl.pallas_call(
        functools.partial(_ln_mod_kernel, with_mod=with_mod),
        grid=(t // rows,),
        in_specs=in_specs, out_specs=out_specs, out_shape=out_shape,
        compiler_params=_cparams("parallel"),
        name="ln_mod",
    )(*args)
    return (res[0], res[1]) if with_mod else (res[0], None)


def _qkv_prep_kernel(u_ref, cos_ref, sin_ref, qn_ref, kn_ref, q_ref, kr_ref, vb_ref, kf_ref, vf_ref):
    cos = cos_ref[...]
    sin = sin_ref[...]
    lane = lax.broadcasted_iota(jnp.int32, cos.shape, 1)
    lane_lo = (lane % (2 * ROT_FREQS)) < ROT_FREQS

    def norm(x, g):
        return x * lax.rsqrt(jnp.mean(x * x, axis=-1, keepdims=True) + RMS_EPS) * g

    def rope(x):
        sw = jnp.where(lane_lo, pltpu.roll(x, LANES - ROT_FREQS, 1), pltpu.roll(x, ROT_FREQS, 1))
        return x * cos + sw * sin

    for h in range(N_HEADS):
        sl = slice(Q_OFF + h * HEAD_DIM, Q_OFF + (h + 1) * HEAD_DIM)
        q = rope(norm(u_ref[:, sl], qn_ref[...])) * (HEAD_DIM ** -0.5)
        q_ref[:, h * HEAD_DIM:(h + 1) * HEAD_DIM] = q.astype(q_ref.dtype)
    for h in range(N_KV_HEADS):
        o = slice(h * HEAD_DIM, (h + 1) * HEAD_DIM)
        kk = norm(u_ref[:, K_OFF + h * HEAD_DIM:K_OFF + (h + 1) * HEAD_DIM], kn_ref[...])
        kf_ref[:, o] = kk
        kr_ref[:, o] = rope(kk).astype(kr_ref.dtype)
        vv = u_ref[:, V_OFF + h * HEAD_DIM:V_OFF + (h + 1) * HEAD_DIM]
        vf_ref[:, o] = vv
        vb_ref[:, o] = vv.astype(vb_ref.dtype)


def qkv_prep(u, cos_t, sin_t, q_norm, k_norm):
    t = u.shape[0]
    row = lambda w: pl.BlockSpec((ROWS, w), lambda i: (i, 0))
    vec = pl.BlockSpec((1, HEAD_DIM), lambda i: (0, 0))
    return pl.pallas_call(
        _qkv_prep_kernel,
        grid=(t // ROWS,),
        in_specs=[row(HY_OFF), row(HEAD_DIM), row(HEAD_DIM), vec, vec],
        out_specs=[row(ATT_W), row(KV_W), row(KV_W), row(KV_W), row(KV_W)],
        out_shape=[jax.ShapeDtypeStruct((t, ATT_W), BF16), jax.ShapeDtypeStruct((t, KV_W), BF16),
                   jax.ShapeDtypeStruct((t, KV_W), BF16), jax.ShapeDtypeStruct((t, KV_W), F32),
                   jax.ShapeDtypeStruct((t, KV_W), F32)],
        compiler_params=_cparams("parallel"),
        name="qkv_prep",
    )(u, cos_t, sin_t, q_norm.reshape(1, HEAD_DIM), k_norm.reshape(1, HEAD_DIM))


def _attn_kernel(*refs, aliased):
    q_ref, k_ref, v_ref = refs[:3]
    o_ref = refs[-1]
    k = k_ref[0]
    v = v_ref[0]
    for r in range(KV_REP):
        sl = slice(r * HEAD_DIM, (r + 1) * HEAD_DIM)
        s = lax.dot_general(q_ref[:, sl], k, NT_DIMS, preferred_element_type=F32)
        m = jnp.max(s, axis=1, keepdims=True)
        p = jnp.exp(s - m)
        l = jnp.sum(p, axis=1, keepdims=True)
        o = jnp.dot(p.astype(BF16), v, preferred_element_type=F32)
        o_ref[:, sl] = (o / l).astype(o_ref.dtype)


def attention(q, k, v, row0, n_seq, l_q, tq, prev=None):
    t = q.shape[0]
    l_k = k.shape[1]
    gw = KV_REP * HEAD_DIM
    nq = l_q // tq
    b0 = row0 // tq
    qmap = lambda b, g, i: (b0 + b * nq + i, g)
    in_specs = [pl.BlockSpec((tq, gw), qmap),
                pl.BlockSpec((1, l_k, HEAD_DIM), lambda b, g, i: (b, 0, g)),
                pl.BlockSpec((1, l_k, HEAD_DIM), lambda b, g, i: (b, 0, g))]
    args = [q, k, v]
    aliases = {}
    if prev is not None:
        in_specs.append(pl.BlockSpec(memory_space=pl.ANY))
        args.append(prev)
        aliases = {3: 0}
    return pl.pallas_call(
        functools.partial(_attn_kernel, aliased=prev is not None),
        grid=(n_seq, N_KV_HEADS, nq),
        in_specs=in_specs,
        out_specs=pl.BlockSpec((tq, gw), qmap),
        out_shape=jax.ShapeDtypeStruct((t, ATT_W), BF16),
        input_output_aliases=aliases,
        compiler_params=_cparams("parallel", "parallel", "parallel"),
        name="attention",
    )(*args)


CONV_ROWS = 1024


def _conv3_kernel(x_ref, prev_ref, next_ref, w_ref, b_ref, *o_refs, lay, silu):
    i = pl.program_id(0)
    x = x_ref[...]
    rows = lax.broadcasted_iota(jnp.int32, (CONV_ROWS, 1), 0)
    grow = rows + i * CONV_ROWS
    in_p = grow < lay.tp
    pos = jnp.where(in_p, grow & (lay.l_p - 1), (grow - lay.tp) & (lay.l_s - 1))
    last = jnp.where(in_p, lay.l_p - 1, lay.l_s - 1)
    xm1 = jnp.where(rows == 0, prev_ref[SUBLANES - 1:SUBLANES, :], pltpu.roll(x, 1, 0))
    xp1 = jnp.where(rows == CONV_ROWS - 1, next_ref[0:1, :], pltpu.roll(x, CONV_ROWS - 1, 0))
    xm1 = jnp.where(pos == 0, 0.0, xm1)
    xp1 = jnp.where(pos == last, 0.0, xp1)
    y = xm1 * w_ref[0:1, :] + x * w_ref[1:2, :] + xp1 * w_ref[2:3, :] + b_ref[...]
    if silu:
        y = y * jax.nn.sigmoid(y)
    for o_ref in o_refs:
        o_ref[...] = y.astype(o_ref.dtype)


def conv3(u, col0, width, w, b, lay, silu, out_dtypes=(F32,), tc=1024):
    t = u.shape[0]
    assert lay.l_p & (lay.l_p - 1) == 0 and lay.l_s & (lay.l_s - 1) == 0
    c0 = col0 // tc
    sub = CONV_ROWS // SUBLANES
    n_sub = t // SUBLANES
    res = pl.pallas_call(
        functools.partial(_conv3_kernel, lay=lay, silu=silu),
        grid=(t // CONV_ROWS, width // tc),
        in_specs=[pl.BlockSpec((CONV_ROWS, tc), lambda i, j: (i, c0 + j)),
                  pl.BlockSpec((SUBLANES, tc), lambda i, j: (jnp.maximum(i * sub - 1, 0), c0 + j)),
                  pl.BlockSpec((SUBLANES, tc), lambda i, j: (jnp.minimum((i + 1) * sub, n_sub - 1), c0 + j)),
                  pl.BlockSpec((3, tc), lambda i, j: (0, j)),
                  pl.BlockSpec((1, tc), lambda i, j: (0, j))],
        out_specs=[pl.BlockSpec((CONV_ROWS, tc), lambda i, j: (i, j)) for _ in out_dtypes],
        out_shape=[jax.ShapeDtypeStruct((t, width), dt) for dt in out_dtypes],
        compiler_params=_cparams("parallel", "parallel"),
        name="conv3",
    )(u, u, u, w, b.reshape(1, width))
    return res if len(out_dtypes) > 1 else res[0]


def _dt_prep_kernel(raw_ref, bias_ref, a_ref, o_ref):
    x = raw_ref[...] + bias_ref[...]
    dt = jnp.maximum(x, 0.0) + jnp.log1p(jnp.exp(-jnp.abs(x)))
    o_ref[...] = jnp.where(lax.broadcasted_iota(jnp.int32, x.shape, 1) < 2 * SSM_HEADS,
                           dt, pltpu.roll(dt, 2 * SSM_HEADS, 1) * a_ref[...])


def dt_prep(raw, dt_bias, a_log):
    t = raw.shape[0]
    nh2 = 2 * SSM_HEADS
    bias = jnp.zeros((1, LANES), F32).at[0, :nh2].set(dt_bias.reshape(-1))
    a = jnp.zeros((1, LANES), F32).at[0, nh2:2 * nh2].set(-jnp.exp(a_log.reshape(-1)))
    return pl.pallas_call(
        _dt_prep_kernel,
        grid=(t // ROWS,),
        in_specs=[pl.BlockSpec((ROWS, LANES), lambda i: (i, 0)),
                  pl.BlockSpec((1, LANES), lambda i: (0, 0)),
                  pl.BlockSpec((1, LANES), lambda i: (0, 0))],
        out_specs=pl.BlockSpec((ROWS, LANES), lambda i: (i, 0)),
        out_shape=jax.ShapeDtypeStruct((t, LANES), F32),
        compiler_params=_cparams("parallel"),
        name="dt_prep",
    )(raw, bias, a)


def _prefix_sum(x, axis):
    idx = lax.broadcasted_iota(jnp.int32, x.shape, axis)
    d = 1
    while d < SSM_CHUNK:
        x = x + jnp.where(idx >= d, pltpu.roll(x, d, axis), 0.0)
        d *= 2
    return x


def _ssd_kernel(*refs, n_chunks, aliased):
    xs_ref, xst_ref, b_ref, c_ref, dtc_ref, dtr_ref, init_ref = refs[:7]
    y_ref, fin_ref, st_ref = refs[-3:]
    d = pl.program_id(1)
    c = pl.program_id(2)
    tt = SSM_CHUNK
    nh = SSM_HEADS

    @pl.when(c == 0)
    def _():
        st_ref[...] = init_ref[0, 0]

    fwd = d == 0
    dtc = dtc_ref[...]
    dtr = dtr_ref[...]
    da_c = jnp.where(fwd, dtc[:, 2 * nh:3 * nh], dtc[:, 3 * nh:4 * nh])
    dtv_r = jnp.where(fwd, dtr[0:nh, :], dtr[nh:2 * nh, :])
    da_r = jnp.where(fwd, dtr[2 * nh:3 * nh, :], dtr[3 * nh:4 * nh, :])
    pc = _prefix_sum(da_c, 0)
    pr = _prefix_sum(da_r, 1)
    tot_c = pc[tt - 1:tt, :]
    tot_r = pr[:, tt - 1:tt]
    acs_c = jnp.where(fwd, pc, tot_c - pc + da_c)
    acs_r = jnp.where(fwd, pr, tot_r - pr + da_r)
    li = lax.broadcasted_iota(jnp.int32, (tt, tt), 0)
    si = lax.broadcasted_iota(jnp.int32, (tt, tt), 1)
    mask = jnp.where(fwd, li - si, si - li) >= 0
    w_r = dtv_r * jnp.exp(tot_r - acs_r)
    cdec_r = jnp.exp(tot_r)
    ns = SSM_STATE
    hp = SSM_HEADDIM
    for g in range(SSM_GROUPS):
        bg = b_ref[:, g * ns:(g + 1) * ns].astype(BF16)
        cg32 = c_ref[:, g * ns:(g + 1) * ns]
        cb = lax.dot_general(cg32.astype(BF16), bg, NT_DIMS, preferred_element_type=F32)
        for r in range(SSM_REP):
            h = g * SSM_REP + r
            acs_l = jnp.broadcast_to(acs_c[:, h:h + 1], (tt, tt))
            dec = jnp.exp(jnp.where(mask, acs_l - acs_r[h:h + 1, :], -jnp.inf))
            m = (cb * dec * dtv_r[h:h + 1, :]).astype(BF16)
            c_in = (cg32 * jnp.exp(acs_l)).astype(BF16)
            state = st_ref[h]
            y = jnp.dot(m, xs_ref[:, h * hp:(h + 1) * hp].astype(BF16), preferred_element_type=F32)
            y += lax.dot_general(c_in, state.astype(BF16), NT_DIMS, preferred_element_type=F32)
            y_ref[0, :, h * hp:(h + 1) * hp] = y
            xw = (xst_ref[h * hp:(h + 1) * hp, :] * w_r[h:h + 1, :]).astype(BF16)
            st_ref[h] = state * cdec_r[h:h + 1, :] + jnp.dot(xw, bg, preferred_element_type=F32)

    @pl.when(c == n_chunks - 1)
    def _():
        fin_ref[0, 0] = st_ref[...]


def ssd(xbc, xst, dtc, dtr, init, row0, n_seq, length, prev=None):
    t = xbc.shape[0]
    tt = SSM_CHUNK
    nc = length // tt
    b0 = row0 // tt
    gn = SSM_GROUPS * SSM_STATE

    def blk(b, d, c):
        return b0 + b * nc + jnp.where(d == 0, c, nc - 1 - c)

    st_shape = (SSM_HEADS, SSM_HEADDIM, SSM_STATE)
    st_spec = pl.BlockSpec((1, 1) + st_shape, lambda b, d, c: (b, d, 0, 0, 0))
    in_specs = [pl.BlockSpec((tt, SSM_W), lambda b, d, c: (blk(b, d, c), 0)),
                pl.BlockSpec((SSM_W, tt), lambda b, d, c: (0, blk(b, d, c))),
                pl.BlockSpec((tt, gn), lambda b, d, c: (blk(b, d, c), SSM_W // gn)),
                pl.BlockSpec((tt, gn), lambda b, d, c: (blk(b, d, c), SSM_W // gn + 1)),
                pl.BlockSpec((tt, LANES), lambda b, d, c: (blk(b, d, c), 0)),
                pl.BlockSpec((LANES, tt), lambda b, d, c: (0, blk(b, d, c))),
                st_spec]
    args = [xbc, xst, xbc, xbc, dtc, dtr, init]
    aliases = {}
    if prev is not None:
        in_specs.append(pl.BlockSpec(memory_space=pl.ANY))
        args.append(prev)
        aliases = {7: 0}
    return pl.pallas_call(
        functools.partial(_ssd_kernel, n_chunks=nc, aliased=prev is not None),
        grid=(n_seq, 2, nc),
        in_specs=in_specs,
        out_specs=[pl.BlockSpec((1, tt, SSM_W), lambda b, d, c: (d, blk(b, d, c), 0)), st_spec],
        out_shape=[jax.ShapeDtypeStruct((2, t, SSM_W), F32),
                   jax.ShapeDtypeStruct((n_seq, 2) + st_shape, F32)],
        scratch_shapes=[pltpu.VMEM(st_shape, F32)],
        input_output_aliases=aliases,
        compiler_params=_cparams("parallel", "parallel", "arbitrary"),
        name="ssd",
    )(*args)


def _ssd_gate_kernel(y_ref, xs_ref, zlo_ref, zhi_ref, d_ref, g_ref, o_ref):
    z = jnp.concatenate([zlo_ref[...], zhi_ref[...]], axis=1)
    y = (y_ref[0] + y_ref[1] + d_ref[...] * xs_ref[...]) * (z * jax.nn.sigmoid(z))
    y = y * lax.rsqrt(jnp.mean(y * y, axis=-1, keepdims=True) + RMS_EPS) * g_ref[...]
    o_ref[...] = y.astype(o_ref.dtype)


def ssd_gate(y2, xbc, u, ssm_d, ssm_norm):
    t = xbc.shape[0]
    half = SSM_W // 2
    zb = Z_OFF // half
    vec = pl.BlockSpec((1, SSM_W), lambda i: (0, 0))
    return pl.pallas_call(
        _ssd_gate_kernel,
        grid=(t // ROWS,),
        in_specs=[pl.BlockSpec((2, ROWS, SSM_W), lambda i: (0, i, 0)),
                  pl.BlockSpec((ROWS, SSM_W), lambda i: (i, 0)),
                  pl.BlockSpec((ROWS, half), lambda i: (i, zb)),
                  pl.BlockSpec((ROWS, half), lambda i: (i, zb + 1)),
                  vec, vec],
        out_specs=pl.BlockSpec((ROWS, SSM_W), lambda i: (i, 0)),
        out_shape=jax.ShapeDtypeStruct((t, SSM_W), BF16),
        compiler_params=_cparams("parallel"),
        name="ssd_gate",
    )(y2, xbc, u, u, jnp.repeat(ssm_d, SSM_HEADDIM).reshape(1, SSM_W), ssm_norm.reshape(1, SSM_W))


def dft_matrices(length):
    blk = 64
    two_l = 2 * length
    k = jnp.arange(length, dtype=jnp.int32)[:, None]
    a = jnp.arange(length // blk, dtype=jnp.int32)[None, :]
    b = jnp.arange(blk, dtype=jnp.int32)[None, :]
    xa = ((k * (a * blk)) % two_l).astype(F32) * (math.pi / length)
    xb = ((k * b) % two_l).astype(F32) * (math.pi / length)
    ca, sa, cb, sb = jnp.cos(xa)[:, :, None], jnp.sin(xa)[:, :, None], jnp.cos(xb)[:, None, :], jnp.sin(xb)[:, None, :]
    cos = (ca * cb - sa * sb).reshape(length, length)
    sin = (sa * cb + ca * sb).reshape(length, length)
    idx = jnp.arange(length, dtype=jnp.int32)
    alt = jnp.where(idx % 2 == 0, 1.0, -1.0).astype(F32)
    sin_fwd = jnp.where(idx[:, None] == 0, alt[None, :], sin)
    sin_inv = jnp.where(idx[None, :] == 0, alt[:, None], sin)
    return jnp.stack([cos, sin_fwd]).astype(BF16), jnp.stack([cos, sin_inv]).astype(BF16)


def _hymlp_kernel(bands_ref, w1_ref, b1_ref, fr_ref, w2_ref, b2_ref, o_ref, *, length, tr):
    i = pl.program_id(0)
    t = (lax.broadcasted_iota(jnp.int32, (tr, 1), 0) + i * tr).astype(F32) / length
    lane = lax.broadcasted_iota(jnp.int32, (tr, LANES), 1)
    ang = (2.0 * math.pi * t) * bands_ref[...]
    feats = jnp.where(lane < HY_BANDS, jnp.cos(ang),
                      jnp.where(lane < 2 * HY_BANDS, jnp.sin(ang), jnp.where(lane == 2 * HY_BANDS, t, 0.0)))
    fr = fr_ref[...]
    hid = jnp.sin(fr * (jnp.dot(feats.astype(BF16), w1_ref[...].astype(BF16),
                                preferred_element_type=F32) + b1_ref[...]))
    o_ref[...] = jnp.sin(fr * (jnp.dot(hid.astype(BF16), w2_ref[...].astype(BF16),
                                       preferred_element_type=F32) + b2_ref[...]))


def _hyfilt_kernel(hid_ref, wf_ref, wb_ref, bf_ref, bb_ref, dl_ref, o_ref, nrm_ref, nyq_ref, *, length, tr):
    i = pl.program_id(2)
    hid = hid_ref[...].astype(BF16)
    t_idx = lax.broadcasted_iota(jnp.int32, (tr, 1), 0) + i * tr
    t = t_idx.astype(F32) / length
    win = jnp.exp(-t * dl_ref[...])
    hf = (jnp.dot(hid, wf_ref[...].astype(BF16), preferred_element_type=F32) + bf_ref[...]) * win
    hb = (jnp.dot(hid, wb_ref[...].astype(BF16), preferred_element_type=F32) + bb_ref[...]) * win
    hb = jnp.where(t_idx == 0, 0.0, hb)
    o_ref[0] = (hf + hb).astype(o_ref.dtype)
    o_ref[1] = (hf - hb).astype(o_ref.dtype)
    sign = jnp.where(t_idx % 2 == 0, 1.0, -1.0)

    @pl.when(i == 0)
    def _():
        nrm_ref[...] = jnp.zeros_like(nrm_ref)
        nyq_ref[...] = jnp.zeros_like(nyq_ref)

    nrm_ref[...] += jnp.sum(jnp.abs(hf) + jnp.abs(hb), axis=0, keepdims=True)
    nyq_ref[...] += jnp.sum(sign * (hf + hb), axis=0, keepdims=True)


def hyena_filters(length, p, fwd):
    bands = jnp.linspace(1e-4, HY_BANDS - 1, HY_BANDS, dtype=F32)
    bands = jnp.zeros((1, LANES), F32).at[0, :2 * HY_BANDS].set(jnp.concatenate([bands, bands]))
    w1 = p['hy_w1']
    ffn = w1.shape[1]
    w1p = jnp.zeros((LANES, ffn), F32).at[:2 * HY_BANDS].set(w1[1:]).at[2 * HY_BANDS].set(w1[0])
    deltas = jnp.abs(jnp.linspace(math.log(HY_DECAY_TARGET) / HY_SLOW_DECAY,
                                  math.log(HY_DECAY_TARGET) / HY_FAST_DECAY, HY_W, dtype=F32)).reshape(1, HY_W)
    tr, tc = _pick(length, 256), 512
    nj = HY_W // tc
    w3, b3 = p['hy_w3'], p['hy_b3'].reshape(1, -1)
    ow = HY_ORDER * HY_W
    full = lambda shape: pl.BlockSpec(shape, lambda i: (0, 0))
    hid = pl.pallas_call(
        functools.partial(_hymlp_kernel, length=length, tr=tr),
        grid=(length // tr,),
        in_specs=[full((1, LANES)), full((LANES, ffn)), full((1, ffn)), full((1, ffn)), full((ffn, ffn)),
                  full((1, ffn))],
        out_specs=pl.BlockSpec((tr, ffn), lambda i: (i, 0)),
        out_shape=jax.ShapeDtypeStruct((length, ffn), F32),
        compiler_params=_cparams("parallel"),
        name="hyena_mlp",
    )(bands, w1p, p['hy_b1'].reshape(1, ffn), p['hy_freq'].reshape(1, ffn), p['hy_w2'],
      p['hy_b2'].reshape(1, ffn))
    hsd, nrm, nyq = pl.pallas_call(
        functools.partial(_hyfilt_kernel, length=length, tr=tr),
        grid=(HY_ORDER, nj, length // tr),
        in_specs=[pl.BlockSpec((tr, ffn), lambda n, j, i: (i, 0)),
                  pl.BlockSpec((ffn, tc), lambda n, j, i: (0, (2 * n) * nj + j)),
                  pl.BlockSpec((ffn, tc), lambda n, j, i: (0, (2 * n + 1) * nj + j)),
                  pl.BlockSpec((1, tc), lambda n, j, i: (0, (2 * n) * nj + j)),
                  pl.BlockSpec((1, tc), lambda n, j, i: (0, (2 * n + 1) * nj + j)),
                  pl.BlockSpec((1, tc), lambda n, j, i: (0, j))],
        out_specs=[pl.BlockSpec((2, tr, tc), lambda n, j, i: (0, i, n * nj + j)),
                   pl.BlockSpec((1, tc), lambda n, j, i: (0, n * nj + j)),
                   pl.BlockSpec((1, tc), lambda n, j, i: (0, n * nj + j))],
        out_shape=[jax.ShapeDtypeStruct((2, length, ow), BF16),
                   jax.ShapeDtypeStruct((1, ow), F32), jax.ShapeDtypeStruct((1, ow), F32)],
        compiler_params=_cparams("parallel", "parallel", "arbitrary"),
        name="hyena_filter",
    )(hid, w3, w3, b3, b3, deltas)
    pq = gmm(fwd, hsd)
    return pq, nrm, nyq


def _dft_fwd_kernel(f_ref, z_ref, pq_ref, nrm_ref, nyq_ref, uv_ref, *, length, tm):
    i = pl.program_id(0)
    z = z_ref[...]
    a = jnp.dot(f_ref[0], z, preferred_element_type=F32)
    b = jnp.dot(f_ref[1], z, preferred_element_type=F32)
    k_idx = lax.broadcasted_iota(jnp.int32, (tm, 1), 0) + i * tm
    is0 = k_idx == 0
    wk = jnp.where(is0, 1.0, 2.0) * (0.5 / length) / nrm_ref[...]
    pp, qq = pq_ref[0], pq_ref[1]
    uv_ref[0, 0] = (wk * (a * pp - jnp.where(is0, 0.0, b * qq))).astype(uv_ref.dtype)
    uv_ref[0, 1] = (wk * jnp.where(is0, b * nyq_ref[...], a * qq + b * pp)).astype(uv_ref.dtype)


def dft_fwd(fwd, z, pq, nrm, nyq, order, row0, n_seq, length):
    tm, tn = _pick(length, 512), 512
    nj = HY_W // tn
    r0 = row0 // length
    vec = pl.BlockSpec((1, tn), lambda i, b, j: (0, order * nj + j))
    return pl.pallas_call(
        functools.partial(_dft_fwd_kernel, length=length, tm=tm),
        grid=(length // tm, n_seq, nj),
        in_specs=[pl.BlockSpec((2, tm, length), lambda i, b, j: (0, i, 0)),
                  pl.BlockSpec((length, tn), lambda i, b, j: (r0 + b, j)),
                  pl.BlockSpec((2, tm, tn), lambda i, b, j: (0, i, order * nj + j)),
                  vec, vec],
        out_specs=pl.BlockSpec((1, 2, tm, tn), lambda i, b, j: (b, 0, i, j)),
        out_shape=jax.ShapeDtypeStruct((n_seq, 2, length, HY_W), BF16),
        compiler_params=_cparams("parallel", "parallel", "parallel"),
        name="hyena_dft_fwd",
    )(fwd, z, pq, nrm, nyq)


def _hy_inv_kernel(*refs, n_out):
    f_ref, uv_ref, z_ref, gate_ref, bias_ref = refs[:5]
    o_refs = refs[-n_out:]
    y = jnp.dot(f_ref[0], uv_ref[0, 0], preferred_element_type=F32)
    y += jnp.dot(f_ref[1], uv_ref[0, 1], preferred_element_type=F32)
    y = gate_ref[...] * (y + bias_ref[...] * z_ref[...])
    for o_ref in o_refs:
        o_ref[...] = y.astype(o_ref.dtype)


def hy_inverse(inv, uv, z, zcol0, gate, gcol0, bias, row0, length, out_dtypes, prev=None):
    t = z.shape[0]
    n_seq = uv.shape[0]
    tm, tn = _pick(length, 512), 512
    r0 = row0 // tm
    ni = length // tm
    rmap = lambda c0: (lambda i, b, j: (r0 + b * ni + i, c0 // tn + j))
    in_specs = [pl.BlockSpec((2, tm, length), lambda i, b, j: (0, i, 0)),
                pl.BlockSpec((1, 2, length, tn), lambda i, b, j: (b, 0, 0, j)),
                pl.BlockSpec((tm, tn), rmap(zcol0)),
                pl.BlockSpec((tm, tn), rmap(gcol0)),
                pl.BlockSpec((1, tn), lambda i, b, j: (0, j))]
    args = [inv, uv, z, gate, bias.reshape(1, HY_W)]
    aliases = {}
    if prev is not None:
        for n, pv in enumerate(prev):
            in_specs.append(pl.BlockSpec(memory_space=pl.ANY))
            args.append(pv)
            aliases[5 + n] = n
    return pl.pallas_call(
        functools.partial(_hy_inv_kernel, n_out=len(out_dtypes)),
        grid=(ni, n_seq, HY_W // tn),
        in_specs=in_specs,
        out_specs=[pl.BlockSpec((tm, tn), rmap(0)) for _ in out_dtypes],
        out_shape=[jax.ShapeDtypeStruct((t, HY_W), dt) for dt in out_dtypes],
        input_output_aliases=aliases,
        compiler_params=_cparams("parallel", "parallel", "parallel"),
        name="hyena_dft_inv",
    )(*args)


def hyena_group(v32, v16, x12, filt, mats, p, row0, n_seq, length, prev):
    fwd, inv = mats
    pq, nrm, nyq = filt
    prev1, prev2 = prev
    uv = dft_fwd(fwd, v16, pq, nrm, nyq, 0, row0, n_seq, length)
    z1 = hy_inverse(inv, uv, v32, 0, x12, 0, p['hy_bias'][0], row0, length, (F32, BF16), prev1)
    uv = dft_fwd(fwd, z1[1], pq, nrm, nyq, 1, row0, n_seq, length)
    z2 = hy_inverse(inv, uv, z1[0], 0, x12, HY_W, p['hy_bias'][1], row0, length, (BF16,), prev2)
    return z1, z2


def _merge_kernel(a_ref, h_ref, s_ref, wa_ref, wh_ref, ws_ref, ga_ref, gh_ref, gs_ref, o_ref):
    acc = ga_ref[...].astype(F32) * jnp.dot(a_ref[...], wa_ref[...], preferred_element_type=F32)
    acc += gh_ref[...].astype(F32) * jnp.dot(h_ref[...], wh_ref[...], preferred_element_type=F32)
    acc += gs_ref[...].astype(F32) * jnp.dot(s_ref[...], ws_ref[...], preferred_element_type=F32)
    o_ref[...] = acc.astype(o_ref.dtype)


def branch_merge(att, hy, ssm, wa, wh, ws, gate_logits, tm=512, tn=512):
    t, kd = att.shape
    d = wa.shape[1]
    nj = d // tn
    xs = pl.BlockSpec((tm, kd), lambda j, i: (i, 0))
    ws_ = pl.BlockSpec((kd, tn), lambda j, i: (0, j))
    gs = lambda b: pl.BlockSpec((tm, tn), lambda j, i: (i, b * nj + j))
    return pl.pallas_call(
        _merge_kernel,
        grid=(nj, t // tm),
        in_specs=[xs, xs, xs, ws_, ws_, ws_, gs(0), gs(1), gs(2)],
        out_specs=pl.BlockSpec((tm, tn), lambda j, i: (i, j)),
        out_shape=jax.ShapeDtypeStruct((t, d), BF16),
        compiler_params=_cparams("parallel", "parallel"),
        name="branch_merge",
    )(att, hy, ssm, wa, wh, ws, gate_logits, gate_logits, gate_logits)


def _swiglu_kernel(x_ref, wg_ref, wu_ref, o_ref):
    x = x_ref[0]
    g = jnp.dot(x, wg_ref[0].astype(BF16), preferred_element_type=F32)
    u = jnp.dot(x, wu_ref[0].astype(BF16), preferred_element_type=F32)
    o_ref[0] = (g * jax.nn.sigmoid(g) * u).astype(o_ref.dtype)


def expert_swiglu(xs, w_gate, w_up, layer, tn=256):
    e, m, d = xs.shape
    f = w_gate.shape[3]
    wspec = pl.BlockSpec((None, 1, d, tn), lambda e, j: (layer, e, 0, j))
    return pl.pallas_call(
        _swiglu_kernel,
        grid=(e, f // tn),
        in_specs=[pl.BlockSpec((1, m, d), lambda e, j: (e, 0, 0)), wspec, wspec],
        out_specs=pl.BlockSpec((1, m, tn), lambda e, j: (e, 0, j)),
        out_shape=jax.ShapeDtypeStruct((e, m, f), BF16),
        compiler_params=_cparams("parallel", "parallel"),
        name="expert_swiglu",
    )(xs, w_gate, w_up)


def _down_kernel(h_ref, w_ref, g_ref, o_ref):
    o_ref[0] = jnp.dot(h_ref[0], w_ref[0].astype(BF16), preferred_element_type=F32) * g_ref[0]


def expert_down(hid, w_down, gates, layer, tn=512):
    e, m, f = hid.shape
    d = w_down.shape[3]
    return pl.pallas_call(
        _down_kernel,
        grid=(e, d // tn),
        in_specs=[pl.BlockSpec((1, m, f), lambda e, j: (e, 0, 0)),
                  pl.BlockSpec((None, 1, f, tn), lambda e, j: (layer, e, 0, j)),
                  pl.BlockSpec((1, m, 1), lambda e, j: (e, 0, 0))],
        out_specs=pl.BlockSpec((1, m, tn), lambda e, j: (e, 0, j)),
        out_shape=jax.ShapeDtypeStruct((e, m, d), F32),
        compiler_params=_cparams("parallel", "parallel"),
        name="expert_down",
    )(hid, w_down, gates)


def ec_moe(xm, p, lay):
    t, d = xm.shape
    w_r = jnp.zeros((d, LANES), BF16).at[:, :N_EXPERTS].set(p['w_router'].astype(BF16))
    logits = mm(xm, w_r, tn=LANES)[:, :N_EXPERTS]
    aff = jax.nn.softmax(logits, axis=-1)
    gates, rows = [], []
    for row0, n_seq, length in ((0, lay.n_p, lay.l_p), (lay.tp, lay.n_s, lay.l_s)):
        cap = EC_CAPACITY * length // N_EXPERTS
        a = aff[row0:row0 + n_seq * length].reshape(n_seq, length, N_EXPERTS)
        g, idx = lax.top_k(jnp.swapaxes(a, 1, 2), cap)
        idx = idx + (row0 + jnp.arange(n_seq, dtype=idx.dtype) * length)[:, None, None]
        gates.append(jnp.swapaxes(g, 0, 1).reshape(N_EXPERTS, n_seq * cap))
        rows.append(jnp.swapaxes(idx, 0, 1).reshape(N_EXPERTS, n_seq * cap))
    gates = jnp.concatenate(gates, axis=1)
    rows = jnp.concatenate(rows, axis=1)
    xs = xm.at[rows.reshape(-1)].get(mode='promise_in_bounds').reshape(N_EXPERTS, -1, d)
    hid = expert_swiglu(xs, p['w_gate'], p['w_up'], p['layer'])
    y = expert_down(hid, p['w_down'], gates[..., None], p['layer'])
    return jnp.zeros((t, d), F32).at[rows.reshape(-1)].add(y.reshape(-1, d), mode='promise_in_bounds')


def rope_tables(lay):
    pos = jnp.arange(lay.l_s)
    row = (pos // GRID_W).astype(F32)
    col = (pos % GRID_W).astype(F32)
    inv = ROPE_THETA ** (-jnp.arange(ROT_FREQS, dtype=F32) / ROT_FREQS)
    ang = jnp.concatenate([row[:, None] * inv] * 2 + [col[:, None] * inv] * 2, axis=1)
    sign = jnp.where((jnp.arange(HEAD_DIM) % (2 * ROT_FREQS)) < ROT_FREQS, -1.0, 1.0).astype(F32)
    cos_s, sin_s = jnp.cos(ang), jnp.sin(ang) * sign
    cos_t = jnp.concatenate([jnp.ones((lay.tp, HEAD_DIM), F32)] + [cos_s] * lay.n_s, axis=0)
    sin_t = jnp.concatenate([jnp.zeros((lay.tp, HEAD_DIM), F32)] + [sin_s] * lay.n_s, axis=0)
    return cos_t, sin_t


def trunk_layer(x, xm, p, mod, mod_next, consts, lay, cache_k, cache_v, state_ssm):
    cos_t, sin_t, mats_p, mats_s = consts
    tp = lay.tp
    u = mm_rows_t(xm, p['w_in_t'], p['layer'], 0, DT_OFF, tm=1024, tn=512)
    dt_raw = mm_rows_t(xm, p['w_in_t'], p['layer'], DT_OFF, LANES, tn=LANES)
    gates = mm_rows_t(xm, p['w_in_gate_t'], None, 0, N_BRANCH * D_MODEL, out_dtype=BF16, sigmoid=True,
                      tm=1024, tn=512)

    q, kr, vb, kf, vf = qkv_prep(u, cos_t, sin_t, p['q_norm'], p['k_norm'])
    k_p = kr[:tp].reshape(lay.n_p, lay.l_p, KV_W)
    v_p = vb[:tp].reshape(lay.n_p, lay.l_p, KV_W)
    k_s = jnp.concatenate([cache_k.reshape(lay.n_s, -1, KV_W).astype(BF16),
                           kr[tp:].reshape(lay.n_s, lay.l_s, KV_W)], axis=1)
    v_s = jnp.concatenate([cache_v.reshape(lay.n_s, -1, KV_W).astype(BF16),
                           vb[tp:].reshape(lay.n_s, lay.l_s, KV_W)], axis=1)
    att = attention(q, k_p, v_p, 0, lay.n_p, lay.l_p, lay.l_p)
    att = attention(q, k_s, v_s, tp, lay.n_s, lay.l_s, ROWS, prev=att)

    v32, v16 = conv3(u, HY_OFF, HY_W, p['hy_conv_w'][:, :HY_W], p['hy_conv_b'][:HY_W], lay, silu=False,
                     out_dtypes=(F32, BF16))
    x12 = conv3(u, HY_OFF + HY_W, 2 * HY_W, p['hy_conv_w'][:, HY_W:], p['hy_conv_b'][HY_W:], lay, silu=False)
    filt_p = hyena_filters(lay.l_p, p, mats_p[0])
    filt_s = hyena_filters(lay.l_s, p, mats_s[0])
    z1, hy = hyena_group(v32, v16, x12, filt_p, mats_p, p, 0, lay.n_p, lay.l_p, (None, None))
    _, hy = hyena_group(v32, v16, x12, filt_s, mats_s, p, tp, lay.n_s, lay.l_s, (z1, hy))
    hy = hy[0]

    xbc = conv3(u, XBC_OFF, SSM_CONV_DIM, p['ssm_conv_w'], p['ssm_conv_b'], lay, silu=True)
    dtc = dt_prep(dt_raw, p['ssm_dt_bias'], p['ssm_a_log'])
    dtr = dtc.T
    zero_state = jnp.zeros((lay.n_p, 2, SSM_HEADS, SSM_HEADDIM, SSM_STATE), F32)
    xst = xbc[:, :SSM_W].T
    y2, states = ssd(xbc, xst, dtc, dtr, zero_state, 0, lay.n_p, lay.l_p)
    y2, _ = ssd(xbc, xst, dtc, dtr, state_ssm, tp, lay.n_s, lay.l_s, prev=y2)
    ssm = ssd_gate(y2, xbc, u, p['ssm_d'], p['ssm_norm'])

    merged = branch_merge(att, hy, ssm, p['w_br_att'], p['w_br_hy'], p['w_br_ssm'], gates)
    m = mm(merged, p['w_out'])
    x1, xm2 = ln_mod(x, m, mod, 2, p['ln1_g'], p['ln1_b'], lay, mod_next=mod, sec_sc=4, sec_sh=3)
    f = ec_moe(xm2, p, lay)
    x2, xm_next = ln_mod(x1, f, mod, 5, p['ln2_g'], p['ln2_b'], lay, mod_next=mod_next, sec_sc=1, sec_sh=0)
    new_k = kf[:tp].reshape(lay.n_p, lay.l_p, N_KV_HEADS, HEAD_DIM)
    new_v = vf[:tp].reshape(lay.n_p, lay.l_p, N_KV_HEADS, HEAD_DIM)
    return x2, xm_next, (new_k, new_v, states)


def kernel(x_prompt, x_sample, cache_k, cache_v, state_ssm, c, c_ctx, w_mod, b_mod, w_in, q_norm, k_norm, hy_conv_w, hy_conv_b, hy_w1, hy_b1, hy_freq, hy_w2, hy_b2, hy_w3, hy_b3, hy_bias, ssm_conv_w, ssm_conv_b, ssm_dt_bias, ssm_a_log, ssm_d, ssm_norm, w_br_att, w_br_hy, w_br_ssm, w_out, ln1_g, ln1_b, w_router, w_gate, w_up, w_down, ln2_g, ln2_b):
    n_p, l_p, d = x_prompt.shape
    n_s, l_s, _ = x_sample.shape
    depth = w_in.shape[0]
    lay = Layout(n_p, l_p, n_s, l_s)
    x = jnp.concatenate([x_prompt.reshape(lay.tp, d), x_sample.reshape(lay.ts, d)], axis=0)

    cond = jnp.zeros((N_COND_PAD, d), F32).at[0].set(c_ctx).at[1:1 + n_s].set(c)
    act = (cond * jax.nn.sigmoid(cond)).astype(BF16)
    mod_all = gmm(act[None], w_mod, tm=N_COND_PAD, tn=2048, tk=1024, share_x=True) + b_mod[:, None, :]
    mods = [mod_all[l].reshape(N_COND_PAD, 1, 6 * d) for l in range(depth)]

    consts = rope_tables(lay) + (dft_matrices(l_p), dft_matrices(l_s))
    xm = modulate(x, mods[0], 1, 0, lay)
    w_in_t = jnp.swapaxes(w_in, 1, 2).astype(BF16)
    new_k, new_v, new_s = [], [], []
    for l in range(depth):
        p = dict(w_in_t=w_in_t, w_in_gate_t=w_in_t[l, GATE_OFF:, :],
                 q_norm=q_norm[l], k_norm=k_norm[l],
                 hy_conv_w=hy_conv_w[l], hy_conv_b=hy_conv_b[l], hy_w1=hy_w1[l], hy_b1=hy_b1[l],
                 hy_freq=hy_freq[l], hy_w2=hy_w2[l], hy_b2=hy_b2[l], hy_w3=hy_w3[l], hy_b3=hy_b3[l],
                 hy_bias=hy_bias[l], ssm_conv_w=ssm_conv_w[l], ssm_conv_b=ssm_conv_b[l],
                 ssm_dt_bias=ssm_dt_bias[l], ssm_a_log=ssm_a_log[l], ssm_d=ssm_d[l], ssm_norm=ssm_norm[l],
                 w_br_att=w_br_att[l].astype(BF16), w_br_hy=w_br_hy[l].astype(BF16),
                 w_br_ssm=w_br_ssm[l].astype(BF16), w_out=w_out[l].astype(BF16),
                 ln1_g=ln1_g[l], ln1_b=ln1_b[l], w_router=w_router[l],
                 w_gate=w_gate, w_up=w_up, w_down=w_down, layer=l,
                 ln2_g=ln2_g[l], ln2_b=ln2_b[l])
        mod_next = mods[l + 1] if l + 1 < depth else None
        x, xm, (k_l, v_l, s_l) = trunk_layer(x, xm, p, mods[l], mod_next, consts, lay,
                                             cache_k[:, l], cache_v[:, l], state_ssm[:, l])
        new_k.append(k_l)
        new_v.append(v_l)
        new_s.append(s_l)
    y_prompt = x[:lay.tp].reshape(n_p, l_p, d)
    y_sample = x[lay.tp:].reshape(n_s, l_s, d)
    return (y_prompt, y_sample, jnp.stack(new_k, axis=1), jnp.stack(new_v, axis=1), jnp.stack(new_s, axis=1))
```

```python
import functools
import math

import jax
import jax.numpy as jnp
from jax import lax
from jax.experimental import pallas as pl
from jax.experimental.pallas import tpu as pltpu

F32 = jnp.float32
BF16 = jnp.bfloat16

D_MODEL = 4096
DEPTH = 2
GRID_W = 64
N_HEADS = 16
N_KV_HEADS = 4
KV_REP = N_HEADS // N_KV_HEADS
HEAD_DIM = 128
ATT_W = N_HEADS * HEAD_DIM
KV_W = N_KV_HEADS * HEAD_DIM
ROT_FREQS = HEAD_DIM // 4
ROPE_THETA = 10000.0
HY_W = 2048
HY_ORDER = 2
HY_BANDS = 16
HY_DECAY_TARGET = 1e-2
HY_FAST_DECAY = 0.3
HY_SLOW_DECAY = 1.5
SSM_W = 2048
SSM_HEADDIM = 64
SSM_HEADS = SSM_W // SSM_HEADDIM
SSM_GROUPS = 8
SSM_REP = SSM_HEADS // SSM_GROUPS
SSM_STATE = 128
SSM_CHUNK = 128
SSM_CONV_DIM = SSM_W + 2 * SSM_GROUPS * SSM_STATE
N_EXPERTS = 16
EC_CAPACITY = 2
MOE_FF = 2048
N_BRANCH = 3
Q_OFF = 0
K_OFF = Q_OFF + ATT_W
V_OFF = K_OFF + KV_W
HY_OFF = V_OFF + KV_W
Z_OFF = HY_OFF + 3 * HY_W
XBC_OFF = Z_OFF + SSM_W
DT_OFF = XBC_OFF + SSM_CONV_DIM
GATE_OFF = DT_OFF + 2 * SSM_HEADS
ALPHA = (2 * DEPTH) ** 0.25
LN_EPS = 1e-5
RMS_EPS = 1e-6
N_COND_PAD = 8
LANES = 128
SUBLANES = 8

VMEM_LIMIT_BYTES = 56 * 1024 * 1024

NT_DIMS = (((1,), (1,)), ((), ()))
TN_DIMS = (((0,), (0,)), ((), ()))


def _cparams(*sem):
    return pltpu.CompilerParams(dimension_semantics=sem, vmem_limit_bytes=VMEM_LIMIT_BYTES)


def _pick(dim, pref):
    t = min(dim, pref)
    while dim % t:
        t //= 2
    return t


def _mm_kernel(x_ref, w_ref, o_ref, acc_ref):
    k = pl.program_id(3)

    @pl.when(k == 0)
    def _():
        acc_ref[...] = jnp.zeros_like(acc_ref)

    acc_ref[...] += jnp.dot(x_ref[0].astype(BF16), w_ref[0].astype(BF16), preferred_element_type=F32)

    @pl.when(k == pl.num_programs(3) - 1)
    def _():
        o_ref[0] = acc_ref[...].astype(o_ref.dtype)


def _mm_fullk_kernel(x_ref, w_ref, o_ref):
    o_ref[0] = jnp.dot(x_ref[0].astype(BF16), w_ref[0].astype(BF16),
                       preferred_element_type=F32).astype(o_ref.dtype)


def gmm(x, w, out_dtype=F32, tm=1024, tn=512, tk=4096, share_x=False):
    g, kd, n = w.shape
    m = x.shape[1]
    tm, tn, tk = _pick(m, tm), _pick(n, tn), _pick(kd, tk)
    if tk == kd:
        xmap = (lambda e, i, j: (0, i, 0)) if share_x else (lambda e, i, j: (e, i, 0))
        return pl.pallas_call(
            _mm_fullk_kernel,
            grid=(g, m // tm, n // tn),
            in_specs=[pl.BlockSpec((1, tm, kd), xmap),
                      pl.BlockSpec((1, kd, tn), lambda e, i, j: (e, 0, j))],
            out_specs=pl.BlockSpec((1, tm, tn), lambda e, i, j: (e, i, j)),
            out_shape=jax.ShapeDtypeStruct((g, m, n), out_dtype),
            compiler_params=_cparams("parallel", "parallel", "parallel"),
            name="gmm",
        )(x, w)
    xmap = (lambda e, i, j, k: (0, i, k)) if share_x else (lambda e, i, j, k: (e, i, k))
    return pl.pallas_call(
        _mm_kernel,
        grid=(g, m // tm, n // tn, kd // tk),
        in_specs=[pl.BlockSpec((1, tm, tk), xmap),
                  pl.BlockSpec((1, tk, tn), lambda e, i, j, k: (e, k, j))],
        out_specs=pl.BlockSpec((1, tm, tn), lambda e, i, j, k: (e, i, j)),
        out_shape=jax.ShapeDtypeStruct((g, m, n), out_dtype),
        scratch_shapes=[pltpu.VMEM((tm, tn), F32)],
        compiler_params=_cparams("parallel", "parallel", "parallel", "arbitrary"),
        name="gmm",
    )(x, w)


def mm(x, w, out_dtype=F32, **kw):
    return gmm(x[None], w[None], out_dtype, **kw)[0]


def _mm_nt_kernel(x_ref, wt_ref, o_ref, *, sigmoid):
    y = lax.dot_general(x_ref[...], wt_ref[...].astype(BF16), NT_DIMS, preferred_element_type=F32)
    if sigmoid:
        y = jax.nn.sigmoid(y)
    o_ref[...] = y.astype(o_ref.dtype)


def mm_rows_t(x, wt, layer, row0, n_rows, out_dtype=F32, sigmoid=False, tm=2048, tn=256):
    m, kd = x.shape
    tm, tn = _pick(m, tm), _pick(n_rows, tn)
    r0 = row0 // tn
    if layer is None:
        wspec = pl.BlockSpec((tn, kd), lambda i, j: (r0 + j, 0))
    else:
        wspec = pl.BlockSpec((None, tn, kd), lambda i, j: (layer, r0 + j, 0))
    return pl.pallas_call(
        functools.partial(_mm_nt_kernel, sigmoid=sigmoid),
        grid=(m // tm, n_rows // tn),
        in_specs=[pl.BlockSpec((tm, kd), lambda i, j: (i, 0)), wspec],
        out_specs=pl.BlockSpec((tm, tn), lambda i, j: (i, j)),
        out_shape=jax.ShapeDtypeStruct((m, n_rows), out_dtype),
        compiler_params=_cparams("parallel", "parallel"),
        name="mm_rows_t",
    )(x, wt)


def _mm_act_kernel(x_ref, w_ref, o_ref, *, sigmoid):
    y = jnp.dot(x_ref[...], w_ref[...].astype(BF16), preferred_element_type=F32)
    if sigmoid:
        y = jax.nn.sigmoid(y)
    o_ref[...] = y.astype(o_ref.dtype)


def mm_cols(x, w, layer, col0, n_cols, out_dtype=F32, sigmoid=False, tm=2048, tn=256):
    m, kd = x.shape
    tm, tn = _pick(m, tm), _pick(n_cols, tn)
    c0 = col0 // tn
    if layer is None:
        wspec = pl.BlockSpec((kd, tn), lambda i, j: (0, c0 + j))
    else:
        wspec = pl.BlockSpec((None, kd, tn), lambda i, j: (layer, 0, c0 + j))
    return pl.pallas_call(
        functools.partial(_mm_act_kernel, sigmoid=sigmoid),
        grid=(m // tm, n_cols // tn),
        in_specs=[pl.BlockSpec((tm, kd), lambda i, j: (i, 0)), wspec],
        out_specs=pl.BlockSpec((tm, tn), lambda i, j: (i, j)),
        out_shape=jax.ShapeDtypeStruct((m, n_cols), out_dtype),
        compiler_params=_cparams("parallel", "parallel"),
        name="mm_cols",
    )(x, w)


class Layout:
    def __init__(self, n_p, l_p, n_s, l_s):
        self.n_p, self.l_p, self.n_s, self.l_s = n_p, l_p, n_s, l_s
        self.tp = n_p * l_p
        self.ts = n_s * l_s
        self.t = self.tp + self.ts

    def group_of_block(self, i, rows):
        bp = self.tp // rows
        return jnp.where(i < bp, 0, 1 + (i - bp) // (self.l_s // rows))

    def seq_edges(self, i, rows):
        bp = self.tp // rows
        per_p, per_s = self.l_p // rows, self.l_s // rows
        first = jnp.where(i < bp, i % per_p == 0, (i - bp) % per_s == 0)
        last = jnp.where(i < bp, i % per_p == per_p - 1, (i - bp) % per_s == per_s - 1)
        return first, last


ROWS = 256


def _modulate_kernel(x_ref, sc_ref, sh_ref, o_ref):
    o_ref[...] = (x_ref[...] * (1.0 + sc_ref[0]) + sh_ref[0]).astype(o_ref.dtype)


def modulate(x, mod, sec_sc, sec_sh, lay):
    t, d = x.shape
    grp = lambda i: lay.group_of_block(i, ROWS)
    return pl.pallas_call(
        _modulate_kernel,
        grid=(t // ROWS,),
        in_specs=[pl.BlockSpec((ROWS, d), lambda i: (i, 0)),
                  pl.BlockSpec((1, 1, d), lambda i: (grp(i), 0, sec_sc)),
                  pl.BlockSpec((1, 1, d), lambda i: (grp(i), 0, sec_sh))],
        out_specs=pl.BlockSpec((ROWS, d), lambda i: (i, 0)),
        out_shape=jax.ShapeDtypeStruct((t, d), BF16),
        compiler_params=_cparams("parallel"),
        name="modulate",
    )(x, mod, mod)


def _ln_mod_kernel(*refs, with_mod):
    if with_mod:
        x_ref, m_ref, gt_ref, lg_ref, lb_ref, sc_ref, sh_ref, o_ref, om_ref = refs
    else:
        x_ref, m_ref, gt_ref, lg_ref, lb_ref, o_ref = refs
    r = ALPHA * x_ref[...] + gt_ref[0] * m_ref[...]
    mu = jnp.mean(r, axis=-1, keepdims=True)
    dlt = r - mu
    var = jnp.mean(dlt * dlt, axis=-1, keepdims=True)
    y = dlt * lax.rsqrt(var + LN_EPS) * lg_ref[...] + lb_ref[...]
    o_ref[...] = y
    if with_mod:
        om_ref[...] = (y * (1.0 + sc_ref[0]) + sh_ref[0]).astype(om_ref.dtype)


def ln_mod(x, m, mod, sec_gate, ln_g, ln_b, lay, mod_next=None, sec_sc=0, sec_sh=0):
    t, d = x.shape
    rows = ROWS // 2
    grp = lambda i: lay.group_of_block(i, rows)
    with_mod = mod_next is not None
    row_spec = pl.BlockSpec((rows, d), lambda i: (i, 0))
    vec_spec = pl.BlockSpec((1, d), lambda i: (0, 0))
    in_specs = [row_spec, row_spec,
                pl.BlockSpec((1, 1, d), lambda i: (grp(i), 0, sec_gate)), vec_spec, vec_spec]
    args = [x, m, mod, ln_g.reshape(1, d), ln_b.reshape(1, d)]
    out_specs = [row_spec]
    out_shape = [jax.ShapeDtypeStruct((t, d), F32)]
    if with_mod:
        in_specs += [pl.BlockSpec((1, 1, d), lambda i: (grp(i), 0, sec_sc)),
                     pl.BlockSpec((1, 1, d), lambda i: (grp(i), 0, sec_sh))]
        args += [mod_next, mod_next]
        out_specs.append(row_spec)
        out_shape.append(jax.ShapeDtypeStruct((t, d), BF16))
    res = pl.pallas_call(
        functools.partial(_ln_mod_kernel, with_mod=with_mod),
        grid=(t // rows,),
        in_specs=in_specs, out_specs=out_specs, out_shape=out_shape,
        compiler_params=_cparams("parallel"),
        name="ln_mod",
    )(*args)
    return (res[0], res[1]) if with_mod else (res[0], None)


def _qkv_prep_kernel(u_ref, cos_ref, sin_ref, qn_ref, kn_ref, q_ref, kr_ref, vb_ref, kf_ref, vf_ref):
    cos = cos_ref[...]
    sin = sin_ref[...]
    lane = lax.broadcasted_iota(jnp.int32, cos.shape, 1)
    lane_lo = (lane % (2 * ROT_FREQS)) < ROT_FREQS

    def norm(x, g):
        return x * lax.rsqrt(jnp.mean(x * x, axis=-1, keepdims=True) + RMS_EPS) * g

    def rope(x):
        sw = jnp.where(lane_lo, pltpu.roll(x, LANES - ROT_FREQS, 1), pltpu.roll(x, ROT_FREQS, 1))
        return x * cos + sw * sin

    for h in range(N_HEADS):
        sl = slice(Q_OFF + h * HEAD_DIM, Q_OFF + (h + 1) * HEAD_DIM)
        q = rope(norm(u_ref[:, sl].astype(F32), qn_ref[...])) * (HEAD_DIM ** -0.5)
        q_ref[:, h * HEAD_DIM:(h + 1) * HEAD_DIM] = q.astype(q_ref.dtype)
    for h in range(N_KV_HEADS):
        o = slice(h * HEAD_DIM, (h + 1) * HEAD_DIM)
        kk = norm(u_ref[:, K_OFF + h * HEAD_DIM:K_OFF + (h + 1) * HEAD_DIM].astype(F32), kn_ref[...])
        kf_ref[:, o] = kk
        kr_ref[:, o] = rope(kk).astype(kr_ref.dtype)
        vv = u_ref[:, V_OFF + h * HEAD_DIM:V_OFF + (h + 1) * HEAD_DIM].astype(F32)
        vf_ref[:, o] = vv
        vb_ref[:, o] = vv.astype(vb_ref.dtype)


def qkv_prep(u, cos_t, sin_t, q_norm, k_norm):
    t = u.shape[0]
    row = lambda w: pl.BlockSpec((ROWS, w), lambda i: (i, 0))
    vec = pl.BlockSpec((1, HEAD_DIM), lambda i: (0, 0))
    return pl.pallas_call(
        _qkv_prep_kernel,
        grid=(t // ROWS,),
        in_specs=[row(HY_OFF), row(HEAD_DIM), row(HEAD_DIM), vec, vec],
        out_specs=[row(ATT_W), row(KV_W), row(KV_W), row(KV_W), row(KV_W)],
        out_shape=[jax.ShapeDtypeStruct((t, ATT_W), BF16), jax.ShapeDtypeStruct((t, KV_W), BF16),
                   jax.ShapeDtypeStruct((t, KV_W), BF16), jax.ShapeDtypeStruct((t, KV_W), F32),
                   jax.ShapeDtypeStruct((t, KV_W), F32)],
        compiler_params=_cparams("parallel"),
        name="qkv_prep",
    )(u, cos_t, sin_t, q_norm.reshape(1, HEAD_DIM), k_norm.reshape(1, HEAD_DIM))


def _attn_kernel(*refs, aliased):
    q_ref, k_ref, v_ref = refs[:3]
    o_ref = refs[-1]
    k = k_ref[0]
    v = v_ref[0]
    for r in range(KV_REP):
        sl = slice(r * HEAD_DIM, (r + 1) * HEAD_DIM)
        s = lax.dot_general(q_ref[:, sl], k, NT_DIMS, preferred_element_type=F32)
        m = jnp.max(s, axis=1, keepdims=True)
        p = jnp.exp(s - m)
        l = jnp.sum(p, axis=1, keepdims=True)
        o = jnp.dot(p.astype(BF16), v, preferred_element_type=F32)
        o_ref[:, sl] = (o / l).astype(o_ref.dtype)


def attention(q, k, v, row0, n_seq, l_q, tq, prev=None):
    t = q.shape[0]
    l_k = k.shape[1]
    gw = KV_REP * HEAD_DIM
    nq = l_q // tq
    b0 = row0 // tq
    qmap = lambda b, g, i: (b0 + b * nq + i, g)
    in_specs = [pl.BlockSpec((tq, gw), qmap),
                pl.BlockSpec((1, l_k, HEAD_DIM), lambda b, g, i: (b, 0, g)),
                pl.BlockSpec((1, l_k, HEAD_DIM), lambda b, g, i: (b, 0, g))]
    args = [q, k, v]
    aliases = {}
    if prev is not None:
        in_specs.append(pl.BlockSpec(memory_space=pl.ANY))
        args.append(prev)
        aliases = {3: 0}
    return pl.pallas_call(
        functools.partial(_attn_kernel, aliased=prev is not None),
        grid=(n_seq, N_KV_HEADS, nq),
        in_specs=in_specs,
        out_specs=pl.BlockSpec((tq, gw), qmap),
        out_shape=jax.ShapeDtypeStruct((t, ATT_W), BF16),
        input_output_aliases=aliases,
        compiler_params=_cparams("parallel", "parallel", "parallel"),
        name="attention",
    )(*args)


CONV_ROWS = 1024


def _conv3_kernel(x_ref, prev_ref, next_ref, w_ref, b_ref, *o_refs, lay, silu):
    i = pl.program_id(0)
    x = x_ref[...].astype(F32)
    halo = prev_ref.shape[0]
    before = prev_ref[...].astype(F32)[halo - 1:halo, :]
    after = next_ref[...].astype(F32)[0:1, :]
    rows = lax.broadcasted_iota(jnp.int32, (CONV_ROWS, 1), 0)
    grow = rows + i * CONV_ROWS
    in_p = grow < lay.tp
    pos = jnp.where(in_p, grow & (lay.l_p - 1), (grow - lay.tp) & (lay.l_s - 1))
    last = jnp.where(in_p, lay.l_p - 1, lay.l_s - 1)
    xm1 = jnp.where(rows == 0, before, pltpu.roll(x, 1, 0))
    xp1 = jnp.where(rows == CONV_ROWS - 1, after, pltpu.roll(x, CONV_ROWS - 1, 0))
    xm1 = jnp.where(pos == 0, 0.0, xm1)
    xp1 = jnp.where(pos == last, 0.0, xp1)
    y = xm1 * w_ref[0:1, :] + x * w_ref[1:2, :] + xp1 * w_ref[2:3, :] + b_ref[...]
    if silu:
        y = y * jax.nn.sigmoid(y)
    for o_ref in o_refs:
        o_ref[...] = y.astype(o_ref.dtype)


def conv3(u, col0, width, w, b, lay, silu, out_dtypes=(F32,), tc=1024):
    t = u.shape[0]
    assert lay.l_p & (lay.l_p - 1) == 0 and lay.l_s & (lay.l_s - 1) == 0
    c0 = col0 // tc
    halo = SUBLANES * (4 // u.dtype.itemsize)
    sub = CONV_ROWS // halo
    n_sub = t // halo
    res = pl.pallas_call(
        functools.partial(_conv3_kernel, lay=lay, silu=silu),
        grid=(t // CONV_ROWS, width // tc),
        in_specs=[pl.BlockSpec((CONV_ROWS, tc), lambda i, j: (i, c0 + j)),
                  pl.BlockSpec((halo, tc), lambda i, j: (jnp.maximum(i * sub - 1, 0), c0 + j)),
                  pl.BlockSpec((halo, tc), lambda i, j: (jnp.minimum((i + 1) * sub, n_sub - 1), c0 + j)),
                  pl.BlockSpec((3, tc), lambda i, j: (0, j)),
                  pl.BlockSpec((1, tc), lambda i, j: (0, j))],
        out_specs=[pl.BlockSpec((CONV_ROWS, tc), lambda i, j: (i, j)) for _ in out_dtypes],
        out_shape=[jax.ShapeDtypeStruct((t, width), dt) for dt in out_dtypes],
        compiler_params=_cparams("parallel", "parallel"),
        name="conv3",
    )(u, u, u, w, b.reshape(1, width))
    return res if len(out_dtypes) > 1 else res[0]


def _dt_prep_kernel(raw_ref, bias_ref, a_ref, o_ref):
    x = raw_ref[...] + bias_ref[...]
    dt = jnp.maximum(x, 0.0) + jnp.log1p(jnp.exp(-jnp.abs(x)))
    o_ref[...] = jnp.where(lax.broadcasted_iota(jnp.int32, x.shape, 1) < 2 * SSM_HEADS,
                           dt, pltpu.roll(dt, 2 * SSM_HEADS, 1) * a_ref[...])


def dt_prep(raw, dt_bias, a_log):
    t = raw.shape[0]
    nh2 = 2 * SSM_HEADS
    bias = jnp.zeros((1, LANES), F32).at[0, :nh2].set(dt_bias.reshape(-1))
    a = jnp.zeros((1, LANES), F32).at[0, nh2:2 * nh2].set(-jnp.exp(a_log.reshape(-1)))
    return pl.pallas_call(
        _dt_prep_kernel,
        grid=(t // ROWS,),
        in_specs=[pl.BlockSpec((ROWS, LANES), lambda i: (i, 0)),
                  pl.BlockSpec((1, LANES), lambda i: (0, 0)),
                  pl.BlockSpec((1, LANES), lambda i: (0, 0))],
        out_specs=pl.BlockSpec((ROWS, LANES), lambda i: (i, 0)),
        out_shape=jax.ShapeDtypeStruct((t, LANES), F32),
        compiler_params=_cparams("parallel"),
        name="dt_prep",
    )(raw, bias, a)


def _prefix_sum(x, axis):
    idx = lax.broadcasted_iota(jnp.int32, x.shape, axis)
    d = 1
    while d < SSM_CHUNK:
        x = x + jnp.where(idx >= d, pltpu.roll(x, d, axis), 0.0)
        d *= 2
    return x


def _ssd_kernel(*refs, n_chunks, aliased):
    xs_ref, xst_ref, b_ref, c_ref, dtc_ref, dtr_ref, init_ref = refs[:7]
    y_ref, fin_ref, st_ref = refs[-3:]
    d = pl.program_id(1)
    c = pl.program_id(2)
    tt = SSM_CHUNK
    nh = SSM_HEADS

    @pl.when(c == 0)
    def _():
        st_ref[...] = init_ref[0, 0]

    fwd = d == 0
    dtc = dtc_ref[...]
    dtr = dtr_ref[...]
    da_c = jnp.where(fwd, dtc[:, 2 * nh:3 * nh], dtc[:, 3 * nh:4 * nh])
    dtv_r = jnp.where(fwd, dtr[0:nh, :], dtr[nh:2 * nh, :])
    da_r = jnp.where(fwd, dtr[2 * nh:3 * nh, :], dtr[3 * nh:4 * nh, :])
    pc = _prefix_sum(da_c, 0)
    pr = _prefix_sum(da_r, 1)
    tot_c = pc[tt - 1:tt, :]
    tot_r = pr[:, tt - 1:tt]
    acs_c = jnp.where(fwd, pc, tot_c - pc + da_c)
    acs_r = jnp.where(fwd, pr, tot_r - pr + da_r)
    li = lax.broadcasted_iota(jnp.int32, (tt, tt), 0)
    si = lax.broadcasted_iota(jnp.int32, (tt, tt), 1)
    mask = jnp.where(fwd, li - si, si - li) >= 0
    w_r = dtv_r * jnp.exp(tot_r - acs_r)
    cdec_r = jnp.exp(tot_r)
    ns = SSM_STATE
    hp = SSM_HEADDIM
    for g in range(SSM_GROUPS):
        bg = b_ref[:, g * ns:(g + 1) * ns].astype(BF16)
        cg32 = c_ref[:, g * ns:(g + 1) * ns]
        cb = lax.dot_general(cg32.astype(BF16), bg, NT_DIMS, preferred_element_type=F32)
        for r in range(SSM_REP):
            h = g * SSM_REP + r
            acs_l = jnp.broadcast_to(acs_c[:, h:h + 1], (tt, tt))
            dec = jnp.exp(jnp.where(mask, acs_l - acs_r[h:h + 1, :], -jnp.inf))
            m = (cb * dec * dtv_r[h:h + 1, :]).astype(BF16)
            c_in = (cg32 * jnp.exp(acs_l)).astype(BF16)
            state = st_ref[h]
            y = jnp.dot(m, xs_ref[:, h * hp:(h + 1) * hp].astype(BF16), preferred_element_type=F32)
            y += lax.dot_general(c_in, state.astype(BF16), NT_DIMS, preferred_element_type=F32)
            y_ref[0, :, h * hp:(h + 1) * hp] = y
            xw = (xst_ref[h * hp:(h + 1) * hp, :] * w_r[h:h + 1, :]).astype(BF16)
            st_ref[h] = state * cdec_r[h:h + 1, :] + jnp.dot(xw, bg, preferred_element_type=F32)

    @pl.when(c == n_chunks - 1)
    def _():
        fin_ref[0, 0] = st_ref[...]


def ssd(xbc, xst, dtc, dtr, init, row0, n_seq, length, prev=None):
    t = xbc.shape[0]
    tt = SSM_CHUNK
    nc = length // tt
    b0 = row0 // tt
    gn = SSM_GROUPS * SSM_STATE

    def blk(b, d, c):
        return b0 + b * nc + jnp.where(d == 0, c, nc - 1 - c)

    st_shape = (SSM_HEADS, SSM_HEADDIM, SSM_STATE)
    st_spec = pl.BlockSpec((1, 1) + st_shape, lambda b, d, c: (b, d, 0, 0, 0))
    in_specs = [pl.BlockSpec((tt, SSM_W), lambda b, d, c: (blk(b, d, c), 0)),
                pl.BlockSpec((SSM_W, tt), lambda b, d, c: (0, blk(b, d, c))),
                pl.BlockSpec((tt, gn), lambda b, d, c: (blk(b, d, c), SSM_W // gn)),
                pl.BlockSpec((tt, gn), lambda b, d, c: (blk(b, d, c), SSM_W // gn + 1)),
                pl.BlockSpec((tt, LANES), lambda b, d, c: (blk(b, d, c), 0)),
                pl.BlockSpec((LANES, tt), lambda b, d, c: (0, blk(b, d, c))),
                st_spec]
    args = [xbc, xst, xbc, xbc, dtc, dtr, init]
    aliases = {}
    if prev is not None:
        in_specs.append(pl.BlockSpec(memory_space=pl.ANY))
        args.append(prev)
        aliases = {7: 0}
    return pl.pallas_call(
        functools.partial(_ssd_kernel, n_chunks=nc, aliased=prev is not None),
        grid=(n_seq, 2, nc),
        in_specs=in_specs,
        out_specs=[pl.BlockSpec((1, tt, SSM_W), lambda b, d, c: (d, blk(b, d, c), 0)), st_spec],
        out_shape=[jax.ShapeDtypeStruct((2, t, SSM_W), F32),
                   jax.ShapeDtypeStruct((n_seq, 2) + st_shape, F32)],
        scratch_shapes=[pltpu.VMEM(st_shape, F32)],
        input_output_aliases=aliases,
        compiler_params=_cparams("parallel", "parallel", "arbitrary"),
        name="ssd",
    )(*args)


def _ssd_gate_kernel(y_ref, xs_ref, zlo_ref, zhi_ref, d_ref, g_ref, o_ref):
    z = jnp.concatenate([zlo_ref[...], zhi_ref[...]], axis=1).astype(F32)
    y = (y_ref[0] + y_ref[1] + d_ref[...] * xs_ref[...]) * (z * jax.nn.sigmoid(z))
    y = y * lax.rsqrt(jnp.mean(y * y, axis=-1, keepdims=True) + RMS_EPS) * g_ref[...]
    o_ref[...] = y.astype(o_ref.dtype)


def ssd_gate(y2, xbc, u, ssm_d, ssm_norm):
    t = xbc.shape[0]
    half = SSM_W // 2
    zb = Z_OFF // half
    vec = pl.BlockSpec((1, SSM_W), lambda i: (0, 0))
    return pl.pallas_call(
        _ssd_gate_kernel,
        grid=(t // ROWS,),
        in_specs=[pl.BlockSpec((2, ROWS, SSM_W), lambda i: (0, i, 0)),
                  pl.BlockSpec((ROWS, SSM_W), lambda i: (i, 0)),
                  pl.BlockSpec((ROWS, half), lambda i: (i, zb)),
                  pl.BlockSpec((ROWS, half), lambda i: (i, zb + 1)),
                  vec, vec],
        out_specs=pl.BlockSpec((ROWS, SSM_W), lambda i: (i, 0)),
        out_shape=jax.ShapeDtypeStruct((t, SSM_W), BF16),
        compiler_params=_cparams("parallel"),
        name="ssd_gate",
    )(y2, xbc, u, u, jnp.repeat(ssm_d, SSM_HEADDIM).reshape(1, SSM_W), ssm_norm.reshape(1, SSM_W))


def dft_matrices(length):
    blk = 64
    two_l = 2 * length
    k = jnp.arange(length, dtype=jnp.int32)[:, None]
    a = jnp.arange(length // blk, dtype=jnp.int32)[None, :]
    b = jnp.arange(blk, dtype=jnp.int32)[None, :]
    xa = ((k * (a * blk)) % two_l).astype(F32) * (math.pi / length)
    xb = ((k * b) % two_l).astype(F32) * (math.pi / length)
    ca, sa, cb, sb = jnp.cos(xa)[:, :, None], jnp.sin(xa)[:, :, None], jnp.cos(xb)[:, None, :], jnp.sin(xb)[:, None, :]
    cos = (ca * cb - sa * sb).reshape(length, length)
    sin = (sa * cb + ca * sb).reshape(length, length)
    idx = jnp.arange(length, dtype=jnp.int32)
    alt = jnp.where(idx % 2 == 0, 1.0, -1.0).astype(F32)
    sin_fwd = jnp.where(idx[:, None] == 0, alt[None, :], sin)
    sin_inv = jnp.where(idx[None, :] == 0, alt[:, None], sin)
    return jnp.stack([cos, sin_fwd]).astype(BF16), jnp.stack([cos, sin_inv]).astype(BF16)


def _hymlp_kernel(bands_ref, w1_ref, b1_ref, fr_ref, w2_ref, b2_ref, o_ref, *, length, tr):
    i = pl.program_id(0)
    t = (lax.broadcasted_iota(jnp.int32, (tr, 1), 0) + i * tr).astype(F32) / length
    lane = lax.broadcasted_iota(jnp.int32, (tr, LANES), 1)
    ang = (2.0 * math.pi * t) * bands_ref[...]
    feats = jnp.where(lane < HY_BANDS, jnp.cos(ang),
                      jnp.where(lane < 2 * HY_BANDS, jnp.sin(ang), jnp.where(lane == 2 * HY_BANDS, t, 0.0)))
    fr = fr_ref[...]
    hid = jnp.sin(fr * (jnp.dot(feats.astype(BF16), w1_ref[...].astype(BF16),
                                preferred_element_type=F32) + b1_ref[...]))
    o_ref[...] = jnp.sin(fr * (jnp.dot(hid.astype(BF16), w2_ref[...].astype(BF16),
                                       preferred_element_type=F32) + b2_ref[...]))


def _hyfilt_kernel(hid_ref, wf_ref, wb_ref, bf_ref, bb_ref, dl_ref, o_ref, nrm_ref, nyq_ref, *, length, tr):
    i = pl.program_id(2)
    hid = hid_ref[...].astype(BF16)
    t_idx = lax.broadcasted_iota(jnp.int32, (tr, 1), 0) + i * tr
    t = t_idx.astype(F32) / length
    win = jnp.exp(-t * dl_ref[...])
    hf = (jnp.dot(hid, wf_ref[...].astype(BF16), preferred_element_type=F32) + bf_ref[...]) * win
    hb = (jnp.dot(hid, wb_ref[...].astype(BF16), preferred_element_type=F32) + bb_ref[...]) * win
    hb = jnp.where(t_idx == 0, 0.0, hb)
    o_ref[0] = (hf + hb).astype(o_ref.dtype)
    o_ref[1] = (hf - hb).astype(o_ref.dtype)
    sign = jnp.where(t_idx % 2 == 0, 1.0, -1.0)

    @pl.when(i == 0)
    def _():
        nrm_ref[...] = jnp.zeros_like(nrm_ref)
        nyq_ref[...] = jnp.zeros_like(nyq_ref)

    nrm_ref[...] += jnp.sum(jnp.abs(hf) + jnp.abs(hb), axis=0, keepdims=True)
    nyq_ref[...] += jnp.sum(sign * (hf + hb), axis=0, keepdims=True)


def hyena_filters(length, p, fwd):
    bands = jnp.linspace(1e-4, HY_BANDS - 1, HY_BANDS, dtype=F32)
    bands = jnp.zeros((1, LANES), F32).at[0, :2 * HY_BANDS].set(jnp.concatenate([bands, bands]))
    w1 = p['hy_w1']
    ffn = w1.shape[1]
    w1p = jnp.zeros((LANES, ffn), F32).at[:2 * HY_BANDS].set(w1[1:]).at[2 * HY_BANDS].set(w1[0])
    deltas = jnp.abs(jnp.linspace(math.log(HY_DECAY_TARGET) / HY_SLOW_DECAY,
                                  math.log(HY_DECAY_TARGET) / HY_FAST_DECAY, HY_W, dtype=F32)).reshape(1, HY_W)
    tr, tc = _pick(length, 256), 512
    nj = HY_W // tc
    w3, b3 = p['hy_w3'], p['hy_b3'].reshape(1, -1)
    ow = HY_ORDER * HY_W
    full = lambda shape: pl.BlockSpec(shape, lambda i: (0, 0))
    hid = pl.pallas_call(
        functools.partial(_hymlp_kernel, length=length, tr=tr),
        grid=(length // tr,),
        in_specs=[full((1, LANES)), full((LANES, ffn)), full((1, ffn)), full((1, ffn)), full((ffn, ffn)),
                  full((1, ffn))],
        out_specs=pl.BlockSpec((tr, ffn), lambda i: (i, 0)),
        out_shape=jax.ShapeDtypeStruct((length, ffn), F32),
        compiler_params=_cparams("parallel"),
        name="hyena_mlp",
    )(bands, w1p, p['hy_b1'].reshape(1, ffn), p['hy_freq'].reshape(1, ffn), p['hy_w2'],
      p['hy_b2'].reshape(1, ffn))
    hsd, nrm, nyq = pl.pallas_call(
        functools.partial(_hyfilt_kernel, length=length, tr=tr),
        grid=(HY_ORDER, nj, length // tr),
        in_specs=[pl.BlockSpec((tr, ffn), lambda n, j, i: (i, 0)),
                  pl.BlockSpec((ffn, tc), lambda n, j, i: (0, (2 * n) * nj + j)),
                  pl.BlockSpec((ffn, tc), lambda n, j, i: (0, (2 * n + 1) * nj + j)),
                  pl.BlockSpec((1, tc), lambda n, j, i: (0, (2 * n) * nj + j)),
                  pl.BlockSpec((1, tc), lambda n, j, i: (0, (2 * n + 1) * nj + j)),
                  pl.BlockSpec((1, tc), lambda n, j, i: (0, j))],
        out_specs=[pl.BlockSpec((2, tr, tc), lambda n, j, i: (0, i, n * nj + j)),
                   pl.BlockSpec((1, tc), lambda n, j, i: (0, n * nj + j)),
                   pl.BlockSpec((1, tc), lambda n, j, i: (0, n * nj + j))],
        out_shape=[jax.ShapeDtypeStruct((2, length, ow), BF16),
                   jax.ShapeDtypeStruct((1, ow), F32), jax.ShapeDtypeStruct((1, ow), F32)],
        compiler_params=_cparams("parallel", "parallel", "arbitrary"),
        name="hyena_filter",
    )(hid, w3, w3, b3, b3, deltas)
    pq = gmm(fwd, hsd)
    return pq, nrm, nyq


def _dft_fwd_kernel(f_ref, z_ref, pq_ref, nrm_ref, nyq_ref, uv_ref, *, length, tm):
    i = pl.program_id(0)
    z = z_ref[...]
    a = jnp.dot(f_ref[0], z, preferred_element_type=F32)
    b = jnp.dot(f_ref[1], z, preferred_element_type=F32)
    k_idx = lax.broadcasted_iota(jnp.int32, (tm, 1), 0) + i * tm
    is0 = k_idx == 0
    wk = jnp.where(is0, 1.0, 2.0) * (0.5 / length) / nrm_ref[...]
    pp, qq = pq_ref[0], pq_ref[1]
    uv_ref[0, 0] = (wk * (a * pp - jnp.where(is0, 0.0, b * qq))).astype(uv_ref.dtype)
    uv_ref[0, 1] = (wk * jnp.where(is0, b * nyq_ref[...], a * qq + b * pp)).astype(uv_ref.dtype)


def dft_fwd(fwd, z, pq, nrm, nyq, order, row0, n_seq, length):
    tm, tn = _pick(length, 512), 512
    nj = HY_W // tn
    r0 = row0 // length
    vec = pl.BlockSpec((1, tn), lambda i, b, j: (0, order * nj + j))
    return pl.pallas_call(
        functools.partial(_dft_fwd_kernel, length=length, tm=tm),
        grid=(length // tm, n_seq, nj),
        in_specs=[pl.BlockSpec((2, tm, length), lambda i, b, j: (0, i, 0)),
                  pl.BlockSpec((length, tn), lambda i, b, j: (r0 + b, j)),
                  pl.BlockSpec((2, tm, tn), lambda i, b, j: (0, i, order * nj + j)),
                  vec, vec],
        out_specs=pl.BlockSpec((1, 2, tm, tn), lambda i, b, j: (b, 0, i, j)),
        out_shape=jax.ShapeDtypeStruct((n_seq, 2, length, HY_W), BF16),
        compiler_params=_cparams("parallel", "parallel", "parallel"),
        name="hyena_dft_fwd",
    )(fwd, z, pq, nrm, nyq)


def _hy_inv_kernel(*refs, n_out):
    f_ref, uv_ref, z_ref, gate_ref, bias_ref = refs[:5]
    o_refs = refs[-n_out:]
    y = jnp.dot(f_ref[0], uv_ref[0, 0], preferred_element_type=F32)
    y += jnp.dot(f_ref[1], uv_ref[0, 1], preferred_element_type=F32)
    y = gate_ref[...] * (y + bias_ref[...] * z_ref[...])
    for o_ref in o_refs:
        o_ref[...] = y.astype(o_ref.dtype)


def hy_inverse(inv, uv, z, zcol0, gate, gcol0, bias, row0, length, out_dtypes, prev=None):
    t = z.shape[0]
    n_seq = uv.shape[0]
    tm, tn = _pick(length, 512), 512
    r0 = row0 // tm
    ni = length // tm
    rmap = lambda c0: (lambda i, b, j: (r0 + b * ni + i, c0 // tn + j))
    in_specs = [pl.BlockSpec((2, tm, length), lambda i, b, j: (0, i, 0)),
                pl.BlockSpec((1, 2, length, tn), lambda i, b, j: (b, 0, 0, j)),
                pl.BlockSpec((tm, tn), rmap(zcol0)),
                pl.BlockSpec((tm, tn), rmap(gcol0)),
                pl.BlockSpec((1, tn), lambda i, b, j: (0, j))]
    args = [inv, uv, z, gate, bias.reshape(1, HY_W)]
    aliases = {}
    if prev is not None:
        for n, pv in enumerate(prev):
            in_specs.append(pl.BlockSpec(memory_space=pl.ANY))
            args.append(pv)
            aliases[5 + n] = n
    return pl.pallas_call(
        functools.partial(_hy_inv_kernel, n_out=len(out_dtypes)),
        grid=(ni, n_seq, HY_W // tn),
        in_specs=in_specs,
        out_specs=[pl.BlockSpec((tm, tn), rmap(0)) for _ in out_dtypes],
        out_shape=[jax.ShapeDtypeStruct((t, HY_W), dt) for dt in out_dtypes],
        input_output_aliases=aliases,
        compiler_params=_cparams("parallel", "parallel", "parallel"),
        name="hyena_dft_inv",
    )(*args)


def hyena_group(v32, v16, x12, filt, mats, p, row0, n_seq, length, prev):
    fwd, inv = mats
    pq, nrm, nyq = filt
    prev1, prev2 = prev
    uv = dft_fwd(fwd, v16, pq, nrm, nyq, 0, row0, n_seq, length)
    z1 = hy_inverse(inv, uv, v32, 0, x12, 0, p['hy_bias'][0], row0, length, (F32, BF16), prev1)
    uv = dft_fwd(fwd, z1[1], pq, nrm, nyq, 1, row0, n_seq, length)
    z2 = hy_inverse(inv, uv, z1[0], 0, x12, HY_W, p['hy_bias'][1], row0, length, (BF16,), prev2)
    return z1, z2


def _merge_kernel(a_ref, h_ref, s_ref, wa_ref, wh_ref, ws_ref, ga_ref, gh_ref, gs_ref, o_ref):
    acc = ga_ref[...].astype(F32) * jnp.dot(a_ref[...], wa_ref[...], preferred_element_type=F32)
    acc += gh_ref[...].astype(F32) * jnp.dot(h_ref[...], wh_ref[...], preferred_element_type=F32)
    acc += gs_ref[...].astype(F32) * jnp.dot(s_ref[...], ws_ref[...], preferred_element_type=F32)
    o_ref[...] = acc.astype(o_ref.dtype)


def branch_merge(att, hy, ssm, wa, wh, ws, gate_logits, tm=512, tn=512):
    t, kd = att.shape
    d = wa.shape[1]
    nj = d // tn
    xs = pl.BlockSpec((tm, kd), lambda j, i: (i, 0))
    ws_ = pl.BlockSpec((kd, tn), lambda j, i: (0, j))
    gs = lambda b: pl.BlockSpec((tm, tn), lambda j, i: (i, b * nj + j))
    return pl.pallas_call(
        _merge_kernel,
        grid=(nj, t // tm),
        in_specs=[xs, xs, xs, ws_, ws_, ws_, gs(0), gs(1), gs(2)],
        out_specs=pl.BlockSpec((tm, tn), lambda j, i: (i, j)),
        out_shape=jax.ShapeDtypeStruct((t, d), BF16),
        compiler_params=_cparams("parallel", "parallel"),
        name="branch_merge",
    )(att, hy, ssm, wa, wh, ws, gate_logits, gate_logits, gate_logits)


def _swiglu_kernel(x_ref, wg_ref, wu_ref, o_ref):
    x = x_ref[0]
    g = jnp.dot(x, wg_ref[0].astype(BF16), preferred_element_type=F32)
    u = jnp.dot(x, wu_ref[0].astype(BF16), preferred_element_type=F32)
    o_ref[0] = (g * jax.nn.sigmoid(g) * u).astype(o_ref.dtype)


def expert_swiglu(xs, w_gate, w_up, layer, tn=256):
    e, m, d = xs.shape
    f = w_gate.shape[3]
    wspec = pl.BlockSpec((None, 1, d, tn), lambda e, j: (layer, e, 0, j))
    return pl.pallas_call(
        _swiglu_kernel,
        grid=(e, f // tn),
        in_specs=[pl.BlockSpec((1, m, d), lambda e, j: (e, 0, 0)), wspec, wspec],
        out_specs=pl.BlockSpec((1, m, tn), lambda e, j: (e, 0, j)),
        out_shape=jax.ShapeDtypeStruct((e, m, f), BF16),
        compiler_params=_cparams("parallel", "parallel"),
        name="expert_swiglu",
    )(xs, w_gate, w_up)


def _down_kernel(h_ref, w_ref, g_ref, o_ref):
    o_ref[0] = jnp.dot(h_ref[0], w_ref[0].astype(BF16), preferred_element_type=F32) * g_ref[0]


def expert_down(hid, w_down, gates, layer, tn=512):
    e, m, f = hid.shape
    d = w_down.shape[3]
    return pl.pallas_call(
        _down_kernel,
        grid=(e, d // tn),
        in_specs=[pl.BlockSpec((1, m, f), lambda e, j: (e, 0, 0)),
                  pl.BlockSpec((None, 1, f, tn), lambda e, j: (layer, e, 0, j)),
                  pl.BlockSpec((1, m, 1), lambda e, j: (e, 0, 0))],
        out_specs=pl.BlockSpec((1, m, tn), lambda e, j: (e, 0, j)),
        out_shape=jax.ShapeDtypeStruct((e, m, d), F32),
        compiler_params=_cparams("parallel", "parallel"),
        name="expert_down",
    )(hid, w_down, gates)


def ec_moe(xm, p, lay):
    t, d = xm.shape
    w_r = jnp.zeros((d, LANES), BF16).at[:, :N_EXPERTS].set(p['w_router'].astype(BF16))
    logits = mm(xm, w_r, tn=LANES)[:, :N_EXPERTS]
    aff = jax.nn.softmax(logits, axis=-1)
    gates, rows = [], []
    for row0, n_seq, length in ((0, lay.n_p, lay.l_p), (lay.tp, lay.n_s, lay.l_s)):
        cap = EC_CAPACITY * length // N_EXPERTS
        a = aff[row0:row0 + n_seq * length].reshape(n_seq, length, N_EXPERTS)
        g, idx = lax.top_k(jnp.swapaxes(a, 1, 2), cap)
        idx = idx + (row0 + jnp.arange(n_seq, dtype=idx.dtype) * length)[:, None, None]
        gates.append(jnp.swapaxes(g, 0, 1).reshape(N_EXPERTS, n_seq * cap))
        rows.append(jnp.swapaxes(idx, 0, 1).reshape(N_EXPERTS, n_seq * cap))
    gates = jnp.concatenate(gates, axis=1)
    rows = jnp.concatenate(rows, axis=1)
    xs = xm.at[rows.reshape(-1)].get(mode='promise_in_bounds').reshape(N_EXPERTS, -1, d)
    hid = expert_swiglu(xs, p['w_gate'], p['w_up'], p['layer'])
    y = expert_down(hid, p['w_down'], gates[..., None], p['layer'])
    return jnp.zeros((t, d), F32).at[rows.reshape(-1)].add(y.reshape(-1, d), mode='promise_in_bounds')


def rope_tables(lay):
    pos = jnp.arange(lay.l_s)
    row = (pos // GRID_W).astype(F32)
    col = (pos % GRID_W).astype(F32)
    inv = ROPE_THETA ** (-jnp.arange(ROT_FREQS, dtype=F32) / ROT_FREQS)
    ang = jnp.concatenate([row[:, None] * inv] * 2 + [col[:, None] * inv] * 2, axis=1)
    sign = jnp.where((jnp.arange(HEAD_DIM) % (2 * ROT_FREQS)) < ROT_FREQS, -1.0, 1.0).astype(F32)
    cos_s, sin_s = jnp.cos(ang), jnp.sin(ang) * sign
    cos_t = jnp.concatenate([jnp.ones((lay.tp, HEAD_DIM), F32)] + [cos_s] * lay.n_s, axis=0)
    sin_t = jnp.concatenate([jnp.zeros((lay.tp, HEAD_DIM), F32)] + [sin_s] * lay.n_s, axis=0)
    return cos_t, sin_t


def trunk_layer(x, xm, p, mod, mod_next, consts, lay, cache_k, cache_v, state_ssm):
    cos_t, sin_t, mats_p, mats_s = consts
    tp = lay.tp
    u = mm_rows_t(xm, p['w_in_t'], p['layer'], 0, DT_OFF, out_dtype=BF16, tm=1024, tn=512)
    dt_raw = mm_rows_t(xm, p['w_in_t'], p['layer'], DT_OFF, LANES, tn=LANES)
    gates = mm_rows_t(xm, p['w_in_gate_t'], None, 0, N_BRANCH * D_MODEL, out_dtype=BF16, sigmoid=True,
                      tm=1024, tn=512)

    q, kr, vb, kf, vf = qkv_prep(u, cos_t, sin_t, p['q_norm'], p['k_norm'])
    k_p = kr[:tp].reshape(lay.n_p, lay.l_p, KV_W)
    v_p = vb[:tp].reshape(lay.n_p, lay.l_p, KV_W)
    k_s = jnp.concatenate([cache_k.reshape(lay.n_s, -1, KV_W).astype(BF16),
                           kr[tp:].reshape(lay.n_s, lay.l_s, KV_W)], axis=1)
    v_s = jnp.concatenate([cache_v.reshape(lay.n_s, -1, KV_W).astype(BF16),
                           vb[tp:].reshape(lay.n_s, lay.l_s, KV_W)], axis=1)
    att = attention(q, k_p, v_p, 0, lay.n_p, lay.l_p, lay.l_p)
    att = attention(q, k_s, v_s, tp, lay.n_s, lay.l_s, ROWS, prev=att)

    v32, v16 = conv3(u, HY_OFF, HY_W, p['hy_conv_w'][:, :HY_W], p['hy_conv_b'][:HY_W], lay, silu=False,
                     out_dtypes=(F32, BF16))
    x12 = conv3(u, HY_OFF + HY_W, 2 * HY_W, p['hy_conv_w'][:, HY_W:], p['hy_conv_b'][HY_W:], lay, silu=False)
    filt_p = hyena_filters(lay.l_p, p, mats_p[0])
    filt_s = hyena_filters(lay.l_s, p, mats_s[0])
    z1, hy = hyena_group(v32, v16, x12, filt_p, mats_p, p, 0, lay.n_p, lay.l_p, (None, None))
    _, hy = hyena_group(v32, v16, x12, filt_s, mats_s, p, tp, lay.n_s, lay.l_s, (z1, hy))
    hy = hy[0]

    xbc = conv3(u, XBC_OFF, SSM_CONV_DIM, p['ssm_conv_w'], p['ssm_conv_b'], lay, silu=True)
    dtc = dt_prep(dt_raw, p['ssm_dt_bias'], p['ssm_a_log'])
    dtr = dtc.T
    zero_state = jnp.zeros((lay.n_p, 2, SSM_HEADS, SSM_HEADDIM, SSM_STATE), F32)
    xst = xbc[:, :SSM_W].T
    y2, states = ssd(xbc, xst, dtc, dtr, zero_state, 0, lay.n_p, lay.l_p)
    y2, _ = ssd(xbc, xst, dtc, dtr, state_ssm, tp, lay.n_s, lay.l_s, prev=y2)
    ssm = ssd_gate(y2, xbc, u, p['ssm_d'], p['ssm_norm'])

    merged = branch_merge(att, hy, ssm, p['w_br_att'], p['w_br_hy'], p['w_br_ssm'], gates)
    m = mm(merged, p['w_out'])
    x1, xm2 = ln_mod(x, m, mod, 2, p['ln1_g'], p['ln1_b'], lay, mod_next=mod, sec_sc=4, sec_sh=3)
    f = ec_moe(xm2, p, lay)
    x2, xm_next = ln_mod(x1, f, mod, 5, p['ln2_g'], p['ln2_b'], lay, mod_next=mod_next, sec_sc=1, sec_sh=0)
    new_k = kf[:tp].reshape(lay.n_p, lay.l_p, N_KV_HEADS, HEAD_DIM)
    new_v = vf[:tp].reshape(lay.n_p, lay.l_p, N_KV_HEADS, HEAD_DIM)
    return x2, xm_next, (new_k, new_v, states)


def kernel(x_prompt, x_sample, cache_k, cache_v, state_ssm, c, c_ctx, w_mod, b_mod, w_in, q_norm, k_norm, hy_conv_w, hy_conv_b, hy_w1, hy_b1, hy_freq, hy_w2, hy_b2, hy_w3, hy_b3, hy_bias, ssm_conv_w, ssm_conv_b, ssm_dt_bias, ssm_a_log, ssm_d, ssm_norm, w_br_att, w_br_hy, w_br_ssm, w_out, ln1_g, ln1_b, w_router, w_gate, w_up, w_down, ln2_g, ln2_b):
    n_p, l_p, d = x_prompt.shape
    n_s, l_s, _ = x_sample.shape
    depth = w_in.shape[0]
    lay = Layout(n_p, l_p, n_s, l_s)
    x = jnp.concatenate([x_prompt.reshape(lay.tp, d), x_sample.reshape(lay.ts, d)], axis=0)

    cond = jnp.zeros((N_COND_PAD, d), F32).at[0].set(c_ctx).at[1:1 + n_s].set(c)
    act = (cond * jax.nn.sigmoid(cond)).astype(BF16)
    mod_all = gmm(act[None], w_mod, tm=N_COND_PAD, tn=2048, tk=1024, share_x=True) + b_mod[:, None, :]
    mods = [mod_all[l].reshape(N_COND_PAD, 1, 6 * d) for l in range(depth)]

    consts = rope_tables(lay) + (dft_matrices(l_p), dft_matrices(l_s))
    xm = modulate(x, mods[0], 1, 0, lay)
    w_in_t = jnp.swapaxes(w_in, 1, 2).astype(BF16)
    new_k, new_v, new_s = [], [], []
    for l in range(depth):
        p = dict(w_in_t=w_in_t, w_in_gate_t=w_in_t[l, GATE_OFF:, :],
                 q_norm=q_norm[l], k_norm=k_norm[l],
                 hy_conv_w=hy_conv_w[l], hy_conv_b=hy_conv_b[l], hy_w1=hy_w1[l], hy_b1=hy_b1[l],
                 hy_freq=hy_freq[l], hy_w2=hy_w2[l], hy_b2=hy_b2[l], hy_w3=hy_w3[l], hy_b3=hy_b3[l],
                 hy_bias=hy_bias[l], ssm_conv_w=ssm_conv_w[l], ssm_conv_b=ssm_conv_b[l],
                 ssm_dt_bias=ssm_dt_bias[l], ssm_a_log=ssm_a_log[l], ssm_d=ssm_d[l], ssm_norm=ssm_norm[l],
                 w_br_att=w_br_att[l].astype(BF16), w_br_hy=w_br_hy[l].astype(BF16),
                 w_br_ssm=w_br_ssm[l].astype(BF16), w_out=w_out[l].astype(BF16),
                 ln1_g=ln1_g[l], ln1_b=ln1_b[l], w_router=w_router[l],
                 w_gate=w_gate, w_up=w_up, w_down=w_down, layer=l,
                 ln2_g=ln2_g[l], ln2_b=ln2_b[l])
        mod_next = mods[l + 1] if l + 1 < depth else None
        x, xm, (k_l, v_l, s_l) = trunk_layer(x, xm, p, mods[l], mod_next, consts, lay,
                                             cache_k[:, l], cache_v[:, l], state_ssm[:, l])
        new_k.append(k_l)
        new_v.append(v_l)
        new_s.append(s_l)
    y_prompt = x[:lay.tp].reshape(n_p, l_p, d)
    y_sample = x[lay.tp:].reshape(n_s, l_s, d)
    return (y_prompt, y_sample, jnp.stack(new_k, axis=1), jnp.stack(new_v, axis=1), jnp.stack(new_s, axis=1))
```

```python
import functools
import math

import jax
import jax.numpy as jnp
from jax import lax
from jax.experimental import pallas as pl
from jax.experimental.pallas import tpu as pltpu

F32 = jnp.float32
BF16 = jnp.bfloat16

D_MODEL = 4096
DEPTH = 2
GRID_W = 64
N_HEADS = 16
N_KV_HEADS = 4
KV_REP = N_HEADS // N_KV_HEADS
HEAD_DIM = 128
ATT_W = N_HEADS * HEAD_DIM
KV_W = N_KV_HEADS * HEAD_DIM
ROT_FREQS = HEAD_DIM // 4
ROPE_THETA = 10000.0
HY_W = 2048
HY_ORDER = 2
HY_BANDS = 16
HY_DECAY_TARGET = 1e-2
HY_FAST_DECAY = 0.3
HY_SLOW_DECAY = 1.5
SSM_W = 2048
SSM_HEADDIM = 64
SSM_HEADS = SSM_W // SSM_HEADDIM
SSM_GROUPS = 8
SSM_REP = SSM_HEADS // SSM_GROUPS
SSM_STATE = 128
SSM_CHUNK = 128
SSM_CONV_DIM = SSM_W + 2 * SSM_GROUPS * SSM_STATE
N_EXPERTS = 16
EC_CAPACITY = 2
MOE_FF = 2048
N_BRANCH = 3
Q_OFF = 0
K_OFF = Q_OFF + ATT_W
V_OFF = K_OFF + KV_W
HY_OFF = V_OFF + KV_W
Z_OFF = HY_OFF + 3 * HY_W
XBC_OFF = Z_OFF + SSM_W
DT_OFF = XBC_OFF + SSM_CONV_DIM
GATE_OFF = DT_OFF + 2 * SSM_HEADS
ALPHA = (2 * DEPTH) ** 0.25
LN_EPS = 1e-5
RMS_EPS = 1e-6
N_COND_PAD = 8
LANES = 128
SUBLANES = 8

VMEM_LIMIT_BYTES = 56 * 1024 * 1024

NT_DIMS = (((1,), (1,)), ((), ()))
TN_DIMS = (((0,), (0,)), ((), ()))


def _cparams(*sem):
    return pltpu.CompilerParams(dimension_semantics=sem, vmem_limit_bytes=VMEM_LIMIT_BYTES)


def _pick(dim, pref):
    t = min(dim, pref)
    while dim % t:
        t //= 2
    return t


def _mm_kernel(x_ref, w_ref, o_ref, acc_ref):
    k = pl.program_id(3)

    @pl.when(k == 0)
    def _():
        acc_ref[...] = jnp.zeros_like(acc_ref)

    acc_ref[...] += jnp.dot(x_ref[0].astype(BF16), w_ref[0].astype(BF16), preferred_element_type=F32)

    @pl.when(k == pl.num_programs(3) - 1)
    def _():
        o_ref[0] = acc_ref[...].astype(o_ref.dtype)


def _mm_fullk_kernel(x_ref, w_ref, o_ref):
    o_ref[0] = jnp.dot(x_ref[0].astype(BF16), w_ref[0].astype(BF16),
                       preferred_element_type=F32).astype(o_ref.dtype)


def gmm(x, w, out_dtype=F32, tm=1024, tn=512, tk=4096, share_x=False):
    g, kd, n = w.shape
    m = x.shape[1]
    tm, tn, tk = _pick(m, tm), _pick(n, tn), _pick(kd, tk)
    if tk == kd:
        xmap = (lambda e, i, j: (0, i, 0)) if share_x else (lambda e, i, j: (e, i, 0))
        return pl.pallas_call(
            _mm_fullk_kernel,
            grid=(g, m // tm, n // tn),
            in_specs=[pl.BlockSpec((1, tm, kd), xmap),
                      pl.BlockSpec((1, kd, tn), lambda e, i, j: (e, 0, j))],
            out_specs=pl.BlockSpec((1, tm, tn), lambda e, i, j: (e, i, j)),
            out_shape=jax.ShapeDtypeStruct((g, m, n), out_dtype),
            compiler_params=_cparams("parallel", "parallel", "parallel"),
            name="gmm",
        )(x, w)
    xmap = (lambda e, i, j, k: (0, i, k)) if share_x else (lambda e, i, j, k: (e, i, k))
    return pl.pallas_call(
        _mm_kernel,
        grid=(g, m // tm, n // tn, kd // tk),
        in_specs=[pl.BlockSpec((1, tm, tk), xmap),
                  pl.BlockSpec((1, tk, tn), lambda e, i, j, k: (e, k, j))],
        out_specs=pl.BlockSpec((1, tm, tn), lambda e, i, j, k: (e, i, j)),
        out_shape=jax.ShapeDtypeStruct((g, m, n), out_dtype),
        scratch_shapes=[pltpu.VMEM((tm, tn), F32)],
        compiler_params=_cparams("parallel", "parallel", "parallel", "arbitrary"),
        name="gmm",
    )(x, w)


def mm(x, w, out_dtype=F32, **kw):
    return gmm(x[None], w[None], out_dtype, **kw)[0]


def _mm_nt_kernel(x_ref, wt_ref, o_ref, *, sigmoid):
    y = lax.dot_general(x_ref[...], wt_ref[...].astype(BF16), NT_DIMS, preferred_element_type=F32)
    if sigmoid:
        y = jax.nn.sigmoid(y)
    o_ref[...] = y.astype(o_ref.dtype)


def mm_rows_t(x, wt, layer, row0, n_rows, out_dtype=F32, sigmoid=False, tm=2048, tn=256):
    m, kd = x.shape
    tm, tn = _pick(m, tm), _pick(n_rows, tn)
    r0 = row0 // tn
    if layer is None:
        wspec = pl.BlockSpec((tn, kd), lambda i, j: (r0 + j, 0))
    else:
        wspec = pl.BlockSpec((None, tn, kd), lambda i, j: (layer, r0 + j, 0))
    return pl.pallas_call(
        functools.partial(_mm_nt_kernel, sigmoid=sigmoid),
        grid=(m // tm, n_rows // tn),
        in_specs=[pl.BlockSpec((tm, kd), lambda i, j: (i, 0)), wspec],
        out_specs=pl.BlockSpec((tm, tn), lambda i, j: (i, j)),
        out_shape=jax.ShapeDtypeStruct((m, n_rows), out_dtype),
        compiler_params=_cparams("parallel", "parallel"),
        name="mm_rows_t",
    )(x, wt)


def _mm_act_kernel(x_ref, w_ref, o_ref, *, sigmoid):
    y = jnp.dot(x_ref[...], w_ref[...].astype(BF16), preferred_element_type=F32)
    if sigmoid:
        y = jax.nn.sigmoid(y)
    o_ref[...] = y.astype(o_ref.dtype)


def mm_cols(x, w, layer, col0, n_cols, out_dtype=F32, sigmoid=False, tm=2048, tn=256):
    m, kd = x.shape
    tm, tn = _pick(m, tm), _pick(n_cols, tn)
    c0 = col0 // tn
    if layer is None:
        wspec = pl.BlockSpec((kd, tn), lambda i, j: (0, c0 + j))
    else:
        wspec = pl.BlockSpec((None, kd, tn), lambda i, j: (layer, 0, c0 + j))
    return pl.pallas_call(
        functools.partial(_mm_act_kernel, sigmoid=sigmoid),
        grid=(m // tm, n_cols // tn),
        in_specs=[pl.BlockSpec((tm, kd), lambda i, j: (i, 0)), wspec],
        out_specs=pl.BlockSpec((tm, tn), lambda i, j: (i, j)),
        out_shape=jax.ShapeDtypeStruct((m, n_cols), out_dtype),
        compiler_params=_cparams("parallel", "parallel"),
        name="mm_cols",
    )(x, w)


class Layout:
    def __init__(self, n_p, l_p, n_s, l_s):
        self.n_p, self.l_p, self.n_s, self.l_s = n_p, l_p, n_s, l_s
        self.tp = n_p * l_p
        self.ts = n_s * l_s
        self.t = self.tp + self.ts

    def group_of_block(self, i, rows):
        bp = self.tp // rows
        return jnp.where(i < bp, 0, 1 + (i - bp) // (self.l_s // rows))

    def seq_edges(self, i, rows):
        bp = self.tp // rows
        per_p, per_s = self.l_p // rows, self.l_s // rows
        first = jnp.where(i < bp, i % per_p == 0, (i - bp) % per_s == 0)
        last = jnp.where(i < bp, i % per_p == per_p - 1, (i - bp) % per_s == per_s - 1)
        return first, last


ROWS = 256


def _modulate_kernel(x_ref, sc_ref, sh_ref, o_ref):
    o_ref[...] = (x_ref[...] * (1.0 + sc_ref[0]) + sh_ref[0]).astype(o_ref.dtype)


def modulate(x, mod, sec_sc, sec_sh, lay):
    t, d = x.shape
    grp = lambda i: lay.group_of_block(i, ROWS)
    return pl.pallas_call(
        _modulate_kernel,
        grid=(t // ROWS,),
        in_specs=[pl.BlockSpec((ROWS, d), lambda i: (i, 0)),
                  pl.BlockSpec((1, 1, d), lambda i: (grp(i), 0, sec_sc)),
                  pl.BlockSpec((1, 1, d), lambda i: (grp(i), 0, sec_sh))],
        out_specs=pl.BlockSpec((ROWS, d), lambda i: (i, 0)),
        out_shape=jax.ShapeDtypeStruct((t, d), BF16),
        compiler_params=_cparams("parallel"),
        name="modulate",
    )(x, mod, mod)


def _ln_mod_kernel(*refs, with_mod):
    if with_mod:
        x_ref, m_ref, gt_ref, lg_ref, lb_ref, sc_ref, sh_ref, o_ref, om_ref = refs
    else:
        x_ref, m_ref, gt_ref, lg_ref, lb_ref, o_ref = refs
    r = ALPHA * x_ref[...] + gt_ref[0] * m_ref[...]
    mu = jnp.mean(r, axis=-1, keepdims=True)
    dlt = r - mu
    var = jnp.mean(dlt * dlt, axis=-1, keepdims=True)
    y = dlt * lax.rsqrt(var + LN_EPS) * lg_ref[...] + lb_ref[...]
    o_ref[...] = y
    if with_mod:
        om_ref[...] = (y * (1.0 + sc_ref[0]) + sh_ref[0]).astype(om_ref.dtype)


def ln_mod(x, m, mod, sec_gate, ln_g, ln_b, lay, mod_next=None, sec_sc=0, sec_sh=0):
    t, d = x.shape
    rows = ROWS // 2
    grp = lambda i: lay.group_of_block(i, rows)
    with_mod = mod_next is not None
    row_spec = pl.BlockSpec((rows, d), lambda i: (i, 0))
    vec_spec = pl.BlockSpec((1, d), lambda i: (0, 0))
    in_specs = [row_spec, row_spec,
                pl.BlockSpec((1, 1, d), lambda i: (grp(i), 0, sec_gate)), vec_spec, vec_spec]
    args = [x, m, mod, ln_g.reshape(1, d), ln_b.reshape(1, d)]
    out_specs = [row_spec]
    out_shape = [jax.ShapeDtypeStruct((t, d), F32)]
    if with_mod:
        in_specs += [pl.BlockSpec((1, 1, d), lambda i: (grp(i), 0, sec_sc)),
                     pl.BlockSpec((1, 1, d), lambda i: (grp(i), 0, sec_sh))]
        args += [mod_next, mod_next]
        out_specs.append(row_spec)
        out_shape.append(jax.ShapeDtypeStruct((t, d), BF16))
    res = pl.pallas_call(
        functools.partial(_ln_mod_kernel, with_mod=with_mod),
        grid=(t // rows,),
        in_specs=in_specs, out_specs=out_specs, out_shape=out_shape,
        compiler_params=_cparams("parallel"),
        name="ln_mod",
    )(*args)
    return (res[0], res[1]) if with_mod else (res[0], None)


def _qkv_prep_kernel(u_ref, cos_ref, sin_ref, qn_ref, kn_ref, q_ref, kr_ref, vb_ref, kf_ref, vf_ref):
    cos = cos_ref[...]
    sin = sin_ref[...]
    lane = lax.broadcasted_iota(jnp.int32, cos.shape, 1)
    lane_lo = (lane % (2 * ROT_FREQS)) < ROT_FREQS

    def norm(x, g):
        return x * lax.rsqrt(jnp.mean(x * x, axis=-1, keepdims=True) + RMS_EPS) * g

    def rope(x):
        sw = jnp.where(lane_lo, pltpu.roll(x, LANES - ROT_FREQS, 1), pltpu.roll(x, ROT_FREQS, 1))
        return x * cos + sw * sin

    for h in range(N_HEADS):
        sl = slice(Q_OFF + h * HEAD_DIM, Q_OFF + (h + 1) * HEAD_DIM)
        q = rope(norm(u_ref[:, sl].astype(F32), qn_ref[...])) * (HEAD_DIM ** -0.5)
        q_ref[:, h * HEAD_DIM:(h + 1) * HEAD_DIM] = q.astype(q_ref.dtype)
    for h in range(N_KV_HEADS):
        o = slice(h * HEAD_DIM, (h + 1) * HEAD_DIM)
        kk = norm(u_ref[:, K_OFF + h * HEAD_DIM:K_OFF + (h + 1) * HEAD_DIM].astype(F32), kn_ref[...])
        kf_ref[:, o] = kk
        kr_ref[:, o] = rope(kk).astype(kr_ref.dtype)
        vv = u_ref[:, V_OFF + h * HEAD_DIM:V_OFF + (h + 1) * HEAD_DIM].astype(F32)
        vf_ref[:, o] = vv
        vb_ref[:, o] = vv.astype(vb_ref.dtype)


def qkv_prep(u, cos_t, sin_t, q_norm, k_norm):
    t = u.shape[0]
    row = lambda w: pl.BlockSpec((ROWS, w), lambda i: (i, 0))
    vec = pl.BlockSpec((1, HEAD_DIM), lambda i: (0, 0))
    return pl.pallas_call(
        _qkv_prep_kernel,
        grid=(t // ROWS,),
        in_specs=[row(HY_OFF), row(HEAD_DIM), row(HEAD_DIM), vec, vec],
        out_specs=[row(ATT_W), row(KV_W), row(KV_W), row(KV_W), row(KV_W)],
        out_shape=[jax.ShapeDtypeStruct((t, ATT_W), BF16), jax.ShapeDtypeStruct((t, KV_W), BF16),
                   jax.ShapeDtypeStruct((t, KV_W), BF16), jax.ShapeDtypeStruct((t, KV_W), F32),
                   jax.ShapeDtypeStruct((t, KV_W), F32)],
        compiler_params=_cparams("parallel"),
        name="qkv_prep",
    )(u, cos_t, sin_t, q_norm.reshape(1, HEAD_DIM), k_norm.reshape(1, HEAD_DIM))


def _attn_kernel(*refs, aliased):
    q_ref, k_ref, v_ref = refs[:3]
    o_ref = refs[-1]
    k = k_ref[0]
    v = v_ref[0]
    for r in range(KV_REP):
        sl = slice(r * HEAD_DIM, (r + 1) * HEAD_DIM)
        s = lax.dot_general(q_ref[:, sl], k, NT_DIMS, preferred_element_type=F32)
        m = jnp.max(s, axis=1, keepdims=True)
        p = jnp.exp(s - m)
        l = jnp.sum(p, axis=1, keepdims=True)
        o = jnp.dot(p.astype(BF16), v, preferred_element_type=F32)
        o_ref[:, sl] = (o / l).astype(o_ref.dtype)


def attention(q, k, v, row0, n_seq, l_q, tq, prev=None):
    t = q.shape[0]
    l_k = k.shape[1]
    gw = KV_REP * HEAD_DIM
    nq = l_q // tq
    b0 = row0 // tq
    qmap = lambda b, g, i: (b0 + b * nq + i, g)
    in_specs = [pl.BlockSpec((tq, gw), qmap),
                pl.BlockSpec((1, l_k, HEAD_DIM), lambda b, g, i: (b, 0, g)),
                pl.BlockSpec((1, l_k, HEAD_DIM), lambda b, g, i: (b, 0, g))]
    args = [q, k, v]
    aliases = {}
    if prev is not None:
        in_specs.append(pl.BlockSpec(memory_space=pl.ANY))
        args.append(prev)
        aliases = {3: 0}
    return pl.pallas_call(
        functools.partial(_attn_kernel, aliased=prev is not None),
        grid=(n_seq, N_KV_HEADS, nq),
        in_specs=in_specs,
        out_specs=pl.BlockSpec((tq, gw), qmap),
        out_shape=jax.ShapeDtypeStruct((t, ATT_W), BF16),
        input_output_aliases=aliases,
        compiler_params=_cparams("parallel", "parallel", "parallel"),
        name="attention",
    )(*args)


CONV_ROWS = 1024


def _conv3_kernel(x_ref, prev_ref, next_ref, w_ref, b_ref, *o_refs, lay, silu):
    i = pl.program_id(0)
    x = x_ref[...].astype(F32)
    halo = prev_ref.shape[0]
    before = prev_ref[...].astype(F32)[halo - 1:halo, :]
    after = next_ref[...].astype(F32)[0:1, :]
    rows = lax.broadcasted_iota(jnp.int32, (CONV_ROWS, 1), 0)
    grow = rows + i * CONV_ROWS
    in_p = grow < lay.tp
    pos = jnp.where(in_p, grow & (lay.l_p - 1), (grow - lay.tp) & (lay.l_s - 1))
    last = jnp.where(in_p, lay.l_p - 1, lay.l_s - 1)
    xm1 = jnp.where(rows == 0, before, pltpu.roll(x, 1, 0))
    xp1 = jnp.where(rows == CONV_ROWS - 1, after, pltpu.roll(x, CONV_ROWS - 1, 0))
    xm1 = jnp.where(pos == 0, 0.0, xm1)
    xp1 = jnp.where(pos == last, 0.0, xp1)
    y = xm1 * w_ref[0:1, :] + x * w_ref[1:2, :] + xp1 * w_ref[2:3, :] + b_ref[...]
    if silu:
        y = y * jax.nn.sigmoid(y)
    for o_ref in o_refs:
        o_ref[...] = y.astype(o_ref.dtype)


def conv3(u, col0, width, w, b, lay, silu, out_dtypes=(F32,), tc=1024):
    t = u.shape[0]
    assert lay.l_p & (lay.l_p - 1) == 0 and lay.l_s & (lay.l_s - 1) == 0
    c0 = col0 // tc
    halo = SUBLANES * (4 // u.dtype.itemsize)
    sub = CONV_ROWS // halo
    n_sub = t // halo
    res = pl.pallas_call(
        functools.partial(_conv3_kernel, lay=lay, silu=silu),
        grid=(t // CONV_ROWS, width // tc),
        in_specs=[pl.BlockSpec((CONV_ROWS, tc), lambda i, j: (i, c0 + j)),
                  pl.BlockSpec((halo, tc), lambda i, j: (jnp.maximum(i * sub - 1, 0), c0 + j)),
                  pl.BlockSpec((halo, tc), lambda i, j: (jnp.minimum((i + 1) * sub, n_sub - 1), c0 + j)),
                  pl.BlockSpec((3, tc), lambda i, j: (0, j)),
                  pl.BlockSpec((1, tc), lambda i, j: (0, j))],
        out_specs=[pl.BlockSpec((CONV_ROWS, tc), lambda i, j: (i, j)) for _ in out_dtypes],
        out_shape=[jax.ShapeDtypeStruct((t, width), dt) for dt in out_dtypes],
        compiler_params=_cparams("parallel", "parallel"),
        name="conv3",
    )(u, u, u, w, b.reshape(1, width))
    return res if len(out_dtypes) > 1 else res[0]


def _dt_prep_kernel(raw_ref, bias_ref, a_ref, o_ref):
    x = raw_ref[...] + bias_ref[...]
    dt = jnp.maximum(x, 0.0) + jnp.log1p(jnp.exp(-jnp.abs(x)))
    o_ref[...] = jnp.where(lax.broadcasted_iota(jnp.int32, x.shape, 1) < 2 * SSM_HEADS,
                           dt, pltpu.roll(dt, 2 * SSM_HEADS, 1) * a_ref[...])


def dt_prep(raw, dt_bias, a_log):
    t = raw.shape[0]
    nh2 = 2 * SSM_HEADS
    bias = jnp.zeros((1, LANES), F32).at[0, :nh2].set(dt_bias.reshape(-1))
    a = jnp.zeros((1, LANES), F32).at[0, nh2:2 * nh2].set(-jnp.exp(a_log.reshape(-1)))
    return pl.pallas_call(
        _dt_prep_kernel,
        grid=(t // ROWS,),
        in_specs=[pl.BlockSpec((ROWS, LANES), lambda i: (i, 0)),
                  pl.BlockSpec((1, LANES), lambda i: (0, 0)),
                  pl.BlockSpec((1, LANES), lambda i: (0, 0))],
        out_specs=pl.BlockSpec((ROWS, LANES), lambda i: (i, 0)),
        out_shape=jax.ShapeDtypeStruct((t, LANES), F32),
        compiler_params=_cparams("parallel"),
        name="dt_prep",
    )(raw, bias, a)


def _prefix_sum(x, axis):
    idx = lax.broadcasted_iota(jnp.int32, x.shape, axis)
    d = 1
    while d < SSM_CHUNK:
        x = x + jnp.where(idx >= d, pltpu.roll(x, d, axis), 0.0)
        d *= 2
    return x


def _ssd_kernel(*refs, n_chunks, aliased):
    xs_ref, xst_ref, b_ref, c_ref, dtc_ref, dtr_ref, init_ref = refs[:7]
    y_ref, fin_ref, st_ref = refs[-3:]
    d = pl.program_id(1)
    c = pl.program_id(2)
    tt = SSM_CHUNK
    nh = SSM_HEADS

    @pl.when(c == 0)
    def _():
        st_ref[...] = init_ref[0, 0]

    fwd = d == 0
    dtc = dtc_ref[...]
    dtr = dtr_ref[...]
    da_c = jnp.where(fwd, dtc[:, 2 * nh:3 * nh], dtc[:, 3 * nh:4 * nh])
    dtv_r = jnp.where(fwd, dtr[0:nh, :], dtr[nh:2 * nh, :])
    da_r = jnp.where(fwd, dtr[2 * nh:3 * nh, :], dtr[3 * nh:4 * nh, :])
    pc = _prefix_sum(da_c, 0)
    pr = _prefix_sum(da_r, 1)
    tot_c = pc[tt - 1:tt, :]
    tot_r = pr[:, tt - 1:tt]
    acs_c = jnp.where(fwd, pc, tot_c - pc + da_c)
    acs_r = jnp.where(fwd, pr, tot_r - pr + da_r)
    li = lax.broadcasted_iota(jnp.int32, (tt, tt), 0)
    si = lax.broadcasted_iota(jnp.int32, (tt, tt), 1)
    mask = jnp.where(fwd, li - si, si - li) >= 0
    w_r = dtv_r * jnp.exp(tot_r - acs_r)
    cdec_r = jnp.exp(tot_r)
    ns = SSM_STATE
    hp = SSM_HEADDIM
    for g in range(SSM_GROUPS):
        bg = b_ref[:, g * ns:(g + 1) * ns].astype(BF16)
        cg32 = c_ref[:, g * ns:(g + 1) * ns]
        cb = lax.dot_general(cg32.astype(BF16), bg, NT_DIMS, preferred_element_type=F32)
        for r in range(SSM_REP):
            h = g * SSM_REP + r
            acs_l = jnp.broadcast_to(acs_c[:, h:h + 1], (tt, tt))
            dec = jnp.exp(jnp.where(mask, acs_l - acs_r[h:h + 1, :], -jnp.inf))
            m = (cb * dec * dtv_r[h:h + 1, :]).astype(BF16)
            c_in = (cg32 * jnp.exp(acs_l)).astype(BF16)
            state = st_ref[h]
            y = jnp.dot(m, xs_ref[:, h * hp:(h + 1) * hp].astype(BF16), preferred_element_type=F32)
            y += lax.dot_general(c_in, state.astype(BF16), NT_DIMS, preferred_element_type=F32)
            y_ref[0, :, h * hp:(h + 1) * hp] = y
            xw = (xst_ref[h * hp:(h + 1) * hp, :] * w_r[h:h + 1, :]).astype(BF16)
            st_ref[h] = state * cdec_r[h:h + 1, :] + jnp.dot(xw, bg, preferred_element_type=F32)

    @pl.when(c == n_chunks - 1)
    def _():
        fin_ref[0, 0] = st_ref[...]


def ssd(xbc, xst, dtc, dtr, init, row0, n_seq, length, prev=None):
    t = xbc.shape[0]
    tt = SSM_CHUNK
    nc = length // tt
    b0 = row0 // tt
    gn = SSM_GROUPS * SSM_STATE

    def blk(b, d, c):
        return b0 + b * nc + jnp.where(d == 0, c, nc - 1 - c)

    st_shape = (SSM_HEADS, SSM_HEADDIM, SSM_STATE)
    st_spec = pl.BlockSpec((1, 1) + st_shape, lambda b, d, c: (b, d, 0, 0, 0))
    in_specs = [pl.BlockSpec((tt, SSM_W), lambda b, d, c: (blk(b, d, c), 0)),
                pl.BlockSpec((SSM_W, tt), lambda b, d, c: (0, blk(b, d, c))),
                pl.BlockSpec((tt, gn), lambda b, d, c: (blk(b, d, c), SSM_W // gn)),
                pl.BlockSpec((tt, gn), lambda b, d, c: (blk(b, d, c), SSM_W // gn + 1)),
                pl.BlockSpec((tt, LANES), lambda b, d, c: (blk(b, d, c), 0)),
                pl.BlockSpec((LANES, tt), lambda b, d, c: (0, blk(b, d, c))),
                st_spec]
    args = [xbc, xst, xbc, xbc, dtc, dtr, init]
    aliases = {}
    if prev is not None:
        in_specs.append(pl.BlockSpec(memory_space=pl.ANY))
        args.append(prev)
        aliases = {7: 0}
    return pl.pallas_call(
        functools.partial(_ssd_kernel, n_chunks=nc, aliased=prev is not None),
        grid=(n_seq, 2, nc),
        in_specs=in_specs,
        out_specs=[pl.BlockSpec((1, tt, SSM_W), lambda b, d, c: (d, blk(b, d, c), 0)), st_spec],
        out_shape=[jax.ShapeDtypeStruct((2, t, SSM_W), F32),
                   jax.ShapeDtypeStruct((n_seq, 2) + st_shape, F32)],
        scratch_shapes=[pltpu.VMEM(st_shape, F32)],
        input_output_aliases=aliases,
        compiler_params=_cparams("parallel", "parallel", "arbitrary"),
        name="ssd",
    )(*args)


def _ssd_gate_kernel(y_ref, xs_ref, zlo_ref, zhi_ref, d_ref, g_ref, o_ref):
    z = jnp.concatenate([zlo_ref[...], zhi_ref[...]], axis=1).astype(F32)
    y = (y_ref[0] + y_ref[1] + d_ref[...] * xs_ref[...]) * (z * jax.nn.sigmoid(z))
    y = y * lax.rsqrt(jnp.mean(y * y, axis=-1, keepdims=True) + RMS_EPS) * g_ref[...]
    o_ref[...] = y.astype(o_ref.dtype)


def ssd_gate(y2, xbc, u, ssm_d, ssm_norm):
    t = xbc.shape[0]
    half = SSM_W // 2
    zb = Z_OFF // half
    vec = pl.BlockSpec((1, SSM_W), lambda i: (0, 0))
    return pl.pallas_call(
        _ssd_gate_kernel,
        grid=(t // ROWS,),
        in_specs=[pl.BlockSpec((2, ROWS, SSM_W), lambda i: (0, i, 0)),
                  pl.BlockSpec((ROWS, SSM_W), lambda i: (i, 0)),
                  pl.BlockSpec((ROWS, half), lambda i: (i, zb)),
                  pl.BlockSpec((ROWS, half), lambda i: (i, zb + 1)),
                  vec, vec],
        out_specs=pl.BlockSpec((ROWS, SSM_W), lambda i: (i, 0)),
        out_shape=jax.ShapeDtypeStruct((t, SSM_W), BF16),
        compiler_params=_cparams("parallel"),
        name="ssd_gate",
    )(y2, xbc, u, u, jnp.repeat(ssm_d, SSM_HEADDIM).reshape(1, SSM_W), ssm_norm.reshape(1, SSM_W))


def dft_matrices(length):
    blk = 64
    two_l = 2 * length
    k = jnp.arange(length, dtype=jnp.int32)[:, None]
    a = jnp.arange(length // blk, dtype=jnp.int32)[None, :]
    b = jnp.arange(blk, dtype=jnp.int32)[None, :]
    xa = ((k * (a * blk)) % two_l).astype(F32) * (math.pi / length)
    xb = ((k * b) % two_l).astype(F32) * (math.pi / length)
    ca, sa, cb, sb = jnp.cos(xa)[:, :, None], jnp.sin(xa)[:, :, None], jnp.cos(xb)[:, None, :], jnp.sin(xb)[:, None, :]
    cos = (ca * cb - sa * sb).reshape(length, length)
    sin = (sa * cb + ca * sb).reshape(length, length)
    idx = jnp.arange(length, dtype=jnp.int32)
    alt = jnp.where(idx % 2 == 0, 1.0, -1.0).astype(F32)
    sin_fwd = jnp.where(idx[:, None] == 0, alt[None, :], sin)
    sin_inv = jnp.where(idx[None, :] == 0, alt[:, None], sin)
    return jnp.stack([cos, sin_fwd]).astype(BF16), jnp.stack([cos, sin_inv]).astype(BF16)


def _hymlp_kernel(bands_ref, w1_ref, b1_ref, fr_ref, w2_ref, b2_ref, o_ref, *, length, tr):
    i = pl.program_id(0)
    t = (lax.broadcasted_iota(jnp.int32, (tr, 1), 0) + i * tr).astype(F32) / length
    lane = lax.broadcasted_iota(jnp.int32, (tr, LANES), 1)
    ang = (2.0 * math.pi * t) * bands_ref[...]
    feats = jnp.where(lane < HY_BANDS, jnp.cos(ang),
                      jnp.where(lane < 2 * HY_BANDS, jnp.sin(ang), jnp.where(lane == 2 * HY_BANDS, t, 0.0)))
    fr = fr_ref[...]
    hid = jnp.sin(fr * (jnp.dot(feats.astype(BF16), w1_ref[...].astype(BF16),
                                preferred_element_type=F32) + b1_ref[...]))
    o_ref[...] = jnp.sin(fr * (jnp.dot(hid.astype(BF16), w2_ref[...].astype(BF16),
                                       preferred_element_type=F32) + b2_ref[...]))


def _hyfilt_kernel(hid_ref, wf_ref, wb_ref, bf_ref, bb_ref, dl_ref, o_ref, nrm_ref, nyq_ref, *, length, tr):
    i = pl.program_id(2)
    hid = hid_ref[...].astype(BF16)
    t_idx = lax.broadcasted_iota(jnp.int32, (tr, 1), 0) + i * tr
    t = t_idx.astype(F32) / length
    win = jnp.exp(-t * dl_ref[...])
    hf = (jnp.dot(hid, wf_ref[...].astype(BF16), preferred_element_type=F32) + bf_ref[...]) * win
    hb = (jnp.dot(hid, wb_ref[...].astype(BF16), preferred_element_type=F32) + bb_ref[...]) * win
    hb = jnp.where(t_idx == 0, 0.0, hb)
    o_ref[0] = (hf + hb).astype(o_ref.dtype)
    o_ref[1] = (hf - hb).astype(o_ref.dtype)
    sign = jnp.where(t_idx % 2 == 0, 1.0, -1.0)

    @pl.when(i == 0)
    def _():
        nrm_ref[...] = jnp.zeros_like(nrm_ref)
        nyq_ref[...] = jnp.zeros_like(nyq_ref)

    nrm_ref[...] += jnp.sum(jnp.abs(hf) + jnp.abs(hb), axis=0, keepdims=True)
    nyq_ref[...] += jnp.sum(sign * (hf + hb), axis=0, keepdims=True)


def hyena_filters(length, p, fwd):
    bands = jnp.linspace(1e-4, HY_BANDS - 1, HY_BANDS, dtype=F32)
    bands = jnp.zeros((1, LANES), F32).at[0, :2 * HY_BANDS].set(jnp.concatenate([bands, bands]))
    w1 = p['hy_w1']
    ffn = w1.shape[1]
    w1p = jnp.zeros((LANES, ffn), F32).at[:2 * HY_BANDS].set(w1[1:]).at[2 * HY_BANDS].set(w1[0])
    deltas = jnp.abs(jnp.linspace(math.log(HY_DECAY_TARGET) / HY_SLOW_DECAY,
                                  math.log(HY_DECAY_TARGET) / HY_FAST_DECAY, HY_W, dtype=F32)).reshape(1, HY_W)
    tr, tc = _pick(length, 256), 512
    nj = HY_W // tc
    w3, b3 = p['hy_w3'], p['hy_b3'].reshape(1, -1)
    ow = HY_ORDER * HY_W
    full = lambda shape: pl.BlockSpec(shape, lambda i: (0, 0))
    hid = pl.pallas_call(
        functools.partial(_hymlp_kernel, length=length, tr=tr),
        grid=(length // tr,),
        in_specs=[full((1, LANES)), full((LANES, ffn)), full((1, ffn)), full((1, ffn)), full((ffn, ffn)),
                  full((1, ffn))],
        out_specs=pl.BlockSpec((tr, ffn), lambda i: (i, 0)),
        out_shape=jax.ShapeDtypeStruct((length, ffn), F32),
        compiler_params=_cparams("parallel"),
        name="hyena_mlp",
    )(bands, w1p, p['hy_b1'].reshape(1, ffn), p['hy_freq'].reshape(1, ffn), p['hy_w2'],
      p['hy_b2'].reshape(1, ffn))
    hsd, nrm, nyq = pl.pallas_call(
        functools.partial(_hyfilt_kernel, length=length, tr=tr),
        grid=(HY_ORDER, nj, length // tr),
        in_specs=[pl.BlockSpec((tr, ffn), lambda n, j, i: (i, 0)),
                  pl.BlockSpec((ffn, tc), lambda n, j, i: (0, (2 * n) * nj + j)),
                  pl.BlockSpec((ffn, tc), lambda n, j, i: (0, (2 * n + 1) * nj + j)),
                  pl.BlockSpec((1, tc), lambda n, j, i: (0, (2 * n) * nj + j)),
                  pl.BlockSpec((1, tc), lambda n, j, i: (0, (2 * n + 1) * nj + j)),
                  pl.BlockSpec((1, tc), lambda n, j, i: (0, j))],
        out_specs=[pl.BlockSpec((2, tr, tc), lambda n, j, i: (0, i, n * nj + j)),
                   pl.BlockSpec((1, tc), lambda n, j, i: (0, n * nj + j)),
                   pl.BlockSpec((1, tc), lambda n, j, i: (0, n * nj + j))],
        out_shape=[jax.ShapeDtypeStruct((2, length, ow), BF16),
                   jax.ShapeDtypeStruct((1, ow), F32), jax.ShapeDtypeStruct((1, ow), F32)],
        compiler_params=_cparams("parallel", "parallel", "arbitrary"),
        name="hyena_filter",
    )(hid, w3, w3, b3, b3, deltas)
    pq = gmm(fwd, hsd)
    return pq, nrm, nyq


def _dft_fwd_kernel(f_ref, z_ref, pq_ref, nrm_ref, nyq_ref, uv_ref, *, length, tm):
    i = pl.program_id(0)
    z = z_ref[...]
    a = jnp.dot(f_ref[0], z, preferred_element_type=F32)
    b = jnp.dot(f_ref[1], z, preferred_element_type=F32)
    k_idx = lax.broadcasted_iota(jnp.int32, (tm, 1), 0) + i * tm
    is0 = k_idx == 0
    wk = jnp.where(is0, 1.0, 2.0) * (0.5 / length) / nrm_ref[...]
    pp, qq = pq_ref[0], pq_ref[1]
    uv_ref[0, 0] = (wk * (a * pp - jnp.where(is0, 0.0, b * qq))).astype(uv_ref.dtype)
    uv_ref[0, 1] = (wk * jnp.where(is0, b * nyq_ref[...], a * qq + b * pp)).astype(uv_ref.dtype)


def dft_fwd(fwd, z, pq, nrm, nyq, order, row0, n_seq, length):
    tm, tn = _pick(length, 512), 512
    nj = HY_W // tn
    r0 = row0 // length
    vec = pl.BlockSpec((1, tn), lambda i, b, j: (0, order * nj + j))
    return pl.pallas_call(
        functools.partial(_dft_fwd_kernel, length=length, tm=tm),
        grid=(length // tm, n_seq, nj),
        in_specs=[pl.BlockSpec((2, tm, length), lambda i, b, j: (0, i, 0)),
                  pl.BlockSpec((length, tn), lambda i, b, j: (r0 + b, j)),
                  pl.BlockSpec((2, tm, tn), lambda i, b, j: (0, i, order * nj + j)),
                  vec, vec],
        out_specs=pl.BlockSpec((1, 2, tm, tn), lambda i, b, j: (b, 0, i, j)),
        out_shape=jax.ShapeDtypeStruct((n_seq, 2, length, HY_W), BF16),
        compiler_params=_cparams("parallel", "parallel", "parallel"),
        name="hyena_dft_fwd",
    )(fwd, z, pq, nrm, nyq)


def _hy_inv_kernel(*refs, n_out):
    f_ref, uv_ref, z_ref, gate_ref, bias_ref = refs[:5]
    o_refs = refs[-n_out:]
    y = jnp.dot(f_ref[0], uv_ref[0, 0], preferred_element_type=F32)
    y += jnp.dot(f_ref[1], uv_ref[0, 1], preferred_element_type=F32)
    y = gate_ref[...] * (y + bias_ref[...] * z_ref[...])
    for o_ref in o_refs:
        o_ref[...] = y.astype(o_ref.dtype)


def hy_inverse(inv, uv, z, zcol0, gate, gcol0, bias, row0, length, out_dtypes, prev=None):
    t = z.shape[0]
    n_seq = uv.shape[0]
    tm, tn = _pick(length, 512), 512
    r0 = row0 // tm
    ni = length // tm
    rmap = lambda c0: (lambda i, b, j: (r0 + b * ni + i, c0 // tn + j))
    in_specs = [pl.BlockSpec((2, tm, length), lambda i, b, j: (0, i, 0)),
                pl.BlockSpec((1, 2, length, tn), lambda i, b, j: (b, 0, 0, j)),
                pl.BlockSpec((tm, tn), rmap(zcol0)),
                pl.BlockSpec((tm, tn), rmap(gcol0)),
                pl.BlockSpec((1, tn), lambda i, b, j: (0, j))]
    args = [inv, uv, z, gate, bias.reshape(1, HY_W)]
    aliases = {}
    if prev is not None:
        for n, pv in enumerate(prev):
            in_specs.append(pl.BlockSpec(memory_space=pl.ANY))
            args.append(pv)
            aliases[5 + n] = n
    return pl.pallas_call(
        functools.partial(_hy_inv_kernel, n_out=len(out_dtypes)),
        grid=(ni, n_seq, HY_W // tn),
        in_specs=in_specs,
        out_specs=[pl.BlockSpec((tm, tn), rmap(0)) for _ in out_dtypes],
        out_shape=[jax.ShapeDtypeStruct((t, HY_W), dt) for dt in out_dtypes],
        input_output_aliases=aliases,
        compiler_params=_cparams("parallel", "parallel", "parallel"),
        name="hyena_dft_inv",
    )(*args)


def hyena_group(v32, v16, x12, filt, mats, p, row0, n_seq, length, prev):
    fwd, inv = mats
    pq, nrm, nyq = filt
    prev1, prev2 = prev
    uv = dft_fwd(fwd, v16, pq, nrm, nyq, 0, row0, n_seq, length)
    z1 = hy_inverse(inv, uv, v32, 0, x12, 0, p['hy_bias'][0], row0, length, (F32, BF16), prev1)
    uv = dft_fwd(fwd, z1[1], pq, nrm, nyq, 1, row0, n_seq, length)
    z2 = hy_inverse(inv, uv, z1[0], 0, x12, HY_W, p['hy_bias'][1], row0, length, (BF16,), prev2)
    return z1, z2


def _merge_kernel(a_ref, h_ref, s_ref, wa_ref, wh_ref, ws_ref, ga_ref, gh_ref, gs_ref, o_ref):
    acc = ga_ref[...].astype(F32) * jnp.dot(a_ref[...], wa_ref[...], preferred_element_type=F32)
    acc += gh_ref[...].astype(F32) * jnp.dot(h_ref[...], wh_ref[...], preferred_element_type=F32)
    acc += gs_ref[...].astype(F32) * jnp.dot(s_ref[...], ws_ref[...], preferred_element_type=F32)
    o_ref[...] = acc.astype(o_ref.dtype)


def branch_merge(att, hy, ssm, wa, wh, ws, gate_logits, tm=512, tn=512):
    t, kd = att.shape
    d = wa.shape[1]
    nj = d // tn
    xs = pl.BlockSpec((tm, kd), lambda j, i: (i, 0))
    ws_ = pl.BlockSpec((kd, tn), lambda j, i: (0, j))
    gs = lambda b: pl.BlockSpec((tm, tn), lambda j, i: (i, b * nj + j))
    return pl.pallas_call(
        _merge_kernel,
        grid=(nj, t // tm),
        in_specs=[xs, xs, xs, ws_, ws_, ws_, gs(0), gs(1), gs(2)],
        out_specs=pl.BlockSpec((tm, tn), lambda j, i: (i, j)),
        out_shape=jax.ShapeDtypeStruct((t, d), BF16),
        compiler_params=_cparams("parallel", "parallel"),
        name="branch_merge",
    )(att, hy, ssm, wa, wh, ws, gate_logits, gate_logits, gate_logits)


def _swiglu_kernel(x_ref, wg_ref, wu_ref, o_ref):
    x = x_ref[0]
    g = jnp.dot(x, wg_ref[0].astype(BF16), preferred_element_type=F32)
    u = jnp.dot(x, wu_ref[0].astype(BF16), preferred_element_type=F32)
    o_ref[0] = (g * jax.nn.sigmoid(g) * u).astype(o_ref.dtype)


def expert_swiglu(xs, w_gate, w_up, layer, tn=256):
    e, m, d = xs.shape
    f = w_gate.shape[3]
    wspec = pl.BlockSpec((None, 1, d, tn), lambda e, j: (layer, e, 0, j))
    return pl.pallas_call(
        _swiglu_kernel,
        grid=(e, f // tn),
        in_specs=[pl.BlockSpec((1, m, d), lambda e, j: (e, 0, 0)), wspec, wspec],
        out_specs=pl.BlockSpec((1, m, tn), lambda e, j: (e, 0, j)),
        out_shape=jax.ShapeDtypeStruct((e, m, f), BF16),
        compiler_params=_cparams("parallel", "parallel"),
        name="expert_swiglu",
    )(xs, w_gate, w_up)


def _down_kernel(h_ref, w_ref, g_ref, o_ref):
    o_ref[0] = jnp.dot(h_ref[0], w_ref[0].astype(BF16), preferred_element_type=F32) * g_ref[0]


def expert_down(hid, w_down, gates, layer, tn=512):
    e, m, f = hid.shape
    d = w_down.shape[3]
    return pl.pallas_call(
        _down_kernel,
        grid=(e, d // tn),
        in_specs=[pl.BlockSpec((1, m, f), lambda e, j: (e, 0, 0)),
                  pl.BlockSpec((None, 1, f, tn), lambda e, j: (layer, e, 0, j)),
                  pl.BlockSpec((1, m, 1), lambda e, j: (e, 0, 0))],
        out_specs=pl.BlockSpec((1, m, tn), lambda e, j: (e, 0, j)),
        out_shape=jax.ShapeDtypeStruct((e, m, d), F32),
        compiler_params=_cparams("parallel", "parallel"),
        name="expert_down",
    )(hid, w_down, gates)


def ec_moe(xm, p, lay):
    t, d = xm.shape
    w_r = jnp.zeros((d, LANES), BF16).at[:, :N_EXPERTS].set(p['w_router'].astype(BF16))
    logits = mm(xm, w_r, tn=LANES)[:, :N_EXPERTS]
    aff = jax.nn.softmax(logits, axis=-1)
    gates, rows = [], []
    for row0, n_seq, length in ((0, lay.n_p, lay.l_p), (lay.tp, lay.n_s, lay.l_s)):
        cap = EC_CAPACITY * length // N_EXPERTS
        a = aff[row0:row0 + n_seq * length].reshape(n_seq, length, N_EXPERTS)
        g, idx = lax.top_k(jnp.swapaxes(a, 1, 2), cap)
        idx = idx + (row0 + jnp.arange(n_seq, dtype=idx.dtype) * length)[:, None, None]
        gates.append(jnp.swapaxes(g, 0, 1).reshape(N_EXPERTS, n_seq * cap))
        rows.append(jnp.swapaxes(idx, 0, 1).reshape(N_EXPERTS, n_seq * cap))
    gates = jnp.concatenate(gates, axis=1)
    rows = jnp.concatenate(rows, axis=1)
    xs = xm.at[rows.reshape(-1)].get(mode='promise_in_bounds').reshape(N_EXPERTS, -1, d)
    hid = expert_swiglu(xs, p['w_gate'], p['w_up'], p['layer'])
    y = expert_down(hid, p['w_down'], gates[..., None], p['layer'])
    f = jnp.zeros((t, d), F32)
    for e in range(N_EXPERTS):
        f = f.at[rows[e]].add(y[e], unique_indices=True, mode='promise_in_bounds')
    return f


def rope_tables(lay):
    pos = jnp.arange(lay.l_s)
    row = (pos // GRID_W).astype(F32)
    col = (pos % GRID_W).astype(F32)
    inv = ROPE_THETA ** (-jnp.arange(ROT_FREQS, dtype=F32) / ROT_FREQS)
    ang = jnp.concatenate([row[:, None] * inv] * 2 + [col[:, None] * inv] * 2, axis=1)
    sign = jnp.where((jnp.arange(HEAD_DIM) % (2 * ROT_FREQS)) < ROT_FREQS, -1.0, 1.0).astype(F32)
    cos_s, sin_s = jnp.cos(ang), jnp.sin(ang) * sign
    cos_t = jnp.concatenate([jnp.ones((lay.tp, HEAD_DIM), F32)] + [cos_s] * lay.n_s, axis=0)
    sin_t = jnp.concatenate([jnp.zeros((lay.tp, HEAD_DIM), F32)] + [sin_s] * lay.n_s, axis=0)
    return cos_t, sin_t


def trunk_layer(x, xm, p, mod, mod_next, consts, lay, cache_k, cache_v, state_ssm):
    cos_t, sin_t, mats_p, mats_s = consts
    tp = lay.tp
    u = mm_rows_t(xm, p['w_in_t'], p['layer'], 0, DT_OFF, tm=1024, tn=512)
    dt_raw = mm_rows_t(xm, p['w_in_t'], p['layer'], DT_OFF, LANES, tn=LANES)
    gates = mm_rows_t(xm, p['w_in_gate_t'], None, 0, N_BRANCH * D_MODEL, out_dtype=BF16, sigmoid=True,
                      tm=1024, tn=512)

    q, kr, vb, kf, vf = qkv_prep(u, cos_t, sin_t, p['q_norm'], p['k_norm'])
    k_p = kr[:tp].reshape(lay.n_p, lay.l_p, KV_W)
    v_p = vb[:tp].reshape(lay.n_p, lay.l_p, KV_W)
    k_s = jnp.concatenate([cache_k.reshape(lay.n_s, -1, KV_W).astype(BF16),
                           kr[tp:].reshape(lay.n_s, lay.l_s, KV_W)], axis=1)
    v_s = jnp.concatenate([cache_v.reshape(lay.n_s, -1, KV_W).astype(BF16),
                           vb[tp:].reshape(lay.n_s, lay.l_s, KV_W)], axis=1)
    att = attention(q, k_p, v_p, 0, lay.n_p, lay.l_p, lay.l_p)
    att = attention(q, k_s, v_s, tp, lay.n_s, lay.l_s, ROWS, prev=att)

    v32, v16 = conv3(u, HY_OFF, HY_W, p['hy_conv_w'][:, :HY_W], p['hy_conv_b'][:HY_W], lay, silu=False,
                     out_dtypes=(F32, BF16))
    x12 = conv3(u, HY_OFF + HY_W, 2 * HY_W, p['hy_conv_w'][:, HY_W:], p['hy_conv_b'][HY_W:], lay, silu=False)
    filt_p = hyena_filters(lay.l_p, p, mats_p[0])
    filt_s = hyena_filters(lay.l_s, p, mats_s[0])
    z1, hy = hyena_group(v32, v16, x12, filt_p, mats_p, p, 0, lay.n_p, lay.l_p, (None, None))
    _, hy = hyena_group(v32, v16, x12, filt_s, mats_s, p, tp, lay.n_s, lay.l_s, (z1, hy))
    hy = hy[0]

    xbc = conv3(u, XBC_OFF, SSM_CONV_DIM, p['ssm_conv_w'], p['ssm_conv_b'], lay, silu=True)
    dtc = dt_prep(dt_raw, p['ssm_dt_bias'], p['ssm_a_log'])
    dtr = dtc.T
    zero_state = jnp.zeros((lay.n_p, 2, SSM_HEADS, SSM_HEADDIM, SSM_STATE), F32)
    xst = xbc[:, :SSM_W].T
    y2, states = ssd(xbc, xst, dtc, dtr, zero_state, 0, lay.n_p, lay.l_p)
    y2, _ = ssd(xbc, xst, dtc, dtr, state_ssm, tp, lay.n_s, lay.l_s, prev=y2)
    ssm = ssd_gate(y2, xbc, u, p['ssm_d'], p['ssm_norm'])

    merged = branch_merge(att, hy, ssm, p['w_br_att'], p['w_br_hy'], p['w_br_ssm'], gates)
    m = mm(merged, p['w_out'])
    x1, xm2 = ln_mod(x, m, mod, 2, p['ln1_g'], p['ln1_b'], lay, mod_next=mod, sec_sc=4, sec_sh=3)
    f = ec_moe(xm2, p, lay)
    x2, xm_next = ln_mod(x1, f, mod, 5, p['ln2_g'], p['ln2_b'], lay, mod_next=mod_next, sec_sc=1, sec_sh=0)
    new_k = kf[:tp].reshape(lay.n_p, lay.l_p, N_KV_HEADS, HEAD_DIM)
    new_v = vf[:tp].reshape(lay.n_p, lay.l_p, N_KV_HEADS, HEAD_DIM)
    return x2, xm_next, (new_k, new_v, states)


def kernel(x_prompt, x_sample, cache_k, cache_v, state_ssm, c, c_ctx, w_mod, b_mod, w_in, q_norm, k_norm, hy_conv_w, hy_conv_b, hy_w1, hy_b1, hy_freq, hy_w2, hy_b2, hy_w3, hy_b3, hy_bias, ssm_conv_w, ssm_conv_b, ssm_dt_bias, ssm_a_log, ssm_d, ssm_norm, w_br_att, w_br_hy, w_br_ssm, w_out, ln1_g, ln1_b, w_router, w_gate, w_up, w_down, ln2_g, ln2_b):
    n_p, l_p, d = x_prompt.shape
    n_s, l_s, _ = x_sample.shape
    depth = w_in.shape[0]
    lay = Layout(n_p, l_p, n_s, l_s)
    x = jnp.concatenate([x_prompt.reshape(lay.tp, d), x_sample.reshape(lay.ts, d)], axis=0)

    cond = jnp.zeros((N_COND_PAD, d), F32).at[0].set(c_ctx).at[1:1 + n_s].set(c)
    act = (cond * jax.nn.sigmoid(cond)).astype(BF16)
    mod_all = gmm(act[None], w_mod, tm=N_COND_PAD, tn=2048, tk=1024, share_x=True) + b_mod[:, None, :]
    mods = [mod_all[l].reshape(N_COND_PAD, 1, 6 * d) for l in range(depth)]

    consts = rope_tables(lay) + (dft_matrices(l_p), dft_matrices(l_s))
    xm = modulate(x, mods[0], 1, 0, lay)
    w_in_t = jnp.swapaxes(w_in, 1, 2).astype(BF16)
    new_k, new_v, new_s = [], [], []
    for l in range(depth):
        p = dict(w_in_t=w_in_t, w_in_gate_t=w_in_t[l, GATE_OFF:, :],
                 q_norm=q_norm[l], k_norm=k_norm[l],
                 hy_conv_w=hy_conv_w[l], hy_conv_b=hy_conv_b[l], hy_w1=hy_w1[l], hy_b1=hy_b1[l],
                 hy_freq=hy_freq[l], hy_w2=hy_w2[l], hy_b2=hy_b2[l], hy_w3=hy_w3[l], hy_b3=hy_b3[l],
                 hy_bias=hy_bias[l], ssm_conv_w=ssm_conv_w[l], ssm_conv_b=ssm_conv_b[l],
                 ssm_dt_bias=ssm_dt_bias[l], ssm_a_log=ssm_a_log[l], ssm_d=ssm_d[l], ssm_norm=ssm_norm[l],
                 w_br_att=w_br_att[l].astype(BF16), w_br_hy=w_br_hy[l].astype(BF16),
                 w_br_ssm=w_br_ssm[l].astype(BF16), w_out=w_out[l].astype(BF16),
                 ln1_g=ln1_g[l], ln1_b=ln1_b[l], w_router=w_router[l],
                 w_gate=w_gate, w_up=w_up, w_down=w_down, layer=l,
                 ln2_g=ln2_g[l], ln2_b=ln2_b[l])
        mod_next = mods[l + 1] if l + 1 < depth else None
        x, xm, (k_l, v_l, s_l) = trunk_layer(x, xm, p, mods[l], mod_next, consts, lay,
                                             cache_k[:, l], cache_v[:, l], state_ssm[:, l])
        new_k.append(k_l)
        new_v.append(v_l)
        new_s.append(s_l)
    y_prompt = x[:lay.tp].reshape(n_p, l_p, d)
    y_sample = x[lay.tp:].reshape(n_s, l_s, d)
    return (y_prompt, y_sample, jnp.stack(new_k, axis=1), jnp.stack(new_v, axis=1), jnp.stack(new_s, axis=1))
```

```python
import functools
import math

import jax
import jax.numpy as jnp
from jax import lax
from jax.experimental import pallas as pl
from jax.experimental.pallas import tpu as pltpu

F32 = jnp.float32
BF16 = jnp.bfloat16

D_MODEL = 4096
DEPTH = 2
GRID_W = 64
N_HEADS = 16
N_KV_HEADS = 4
KV_REP = N_HEADS // N_KV_HEADS
HEAD_DIM = 128
ATT_W = N_HEADS * HEAD_DIM
KV_W = N_KV_HEADS * HEAD_DIM
ROT_FREQS = HEAD_DIM // 4
ROPE_THETA = 10000.0
HY_W = 2048
HY_ORDER = 2
HY_BANDS = 16
HY_DECAY_TARGET = 1e-2
HY_FAST_DECAY = 0.3
HY_SLOW_DECAY = 1.5
SSM_W = 2048
SSM_HEADDIM = 64
SSM_HEADS = SSM_W // SSM_HEADDIM
SSM_GROUPS = 8
SSM_REP = SSM_HEADS // SSM_GROUPS
SSM_STATE = 128
SSM_CHUNK = 128
SSM_CONV_DIM = SSM_W + 2 * SSM_GROUPS * SSM_STATE
N_EXPERTS = 16
EC_CAPACITY = 2
MOE_FF = 2048
N_BRANCH = 3
Q_OFF = 0
K_OFF = Q_OFF + ATT_W
V_OFF = K_OFF + KV_W
HY_OFF = V_OFF + KV_W
Z_OFF = HY_OFF + 3 * HY_W
XBC_OFF = Z_OFF + SSM_W
DT_OFF = XBC_OFF + SSM_CONV_DIM
GATE_OFF = DT_OFF + 2 * SSM_HEADS
ALPHA = (2 * DEPTH) ** 0.25
LN_EPS = 1e-5
RMS_EPS = 1e-6
N_COND_PAD = 8
LANES = 128
SUBLANES = 8

VMEM_LIMIT_BYTES = 56 * 1024 * 1024

NT_DIMS = (((1,), (1,)), ((), ()))
TN_DIMS = (((0,), (0,)), ((), ()))


def _cparams(*sem):
    return pltpu.CompilerParams(dimension_semantics=sem, vmem_limit_bytes=VMEM_LIMIT_BYTES)


def _pick(dim, pref):
    t = min(dim, pref)
    while dim % t:
        t //= 2
    return t


def _mm_kernel(x_ref, w_ref, o_ref, acc_ref):
    k = pl.program_id(3)

    @pl.when(k == 0)
    def _():
        acc_ref[...] = jnp.zeros_like(acc_ref)

    acc_ref[...] += jnp.dot(x_ref[0].astype(BF16), w_ref[0].astype(BF16), preferred_element_type=F32)

    @pl.when(k == pl.num_programs(3) - 1)
    def _():
        o_ref[0] = acc_ref[...].astype(o_ref.dtype)


def _mm_fullk_kernel(x_ref, w_ref, o_ref):
    o_ref[0] = jnp.dot(x_ref[0].astype(BF16), w_ref[0].astype(BF16),
                       preferred_element_type=F32).astype(o_ref.dtype)


def gmm(x, w, out_dtype=F32, tm=1024, tn=512, tk=4096, share_x=False):
    g, kd, n = w.shape
    m = x.shape[1]
    tm, tn, tk = _pick(m, tm), _pick(n, tn), _pick(kd, tk)
    if tk == kd:
        xmap = (lambda e, i, j: (0, i, 0)) if share_x else (lambda e, i, j: (e, i, 0))
        return pl.pallas_call(
            _mm_fullk_kernel,
            grid=(g, m // tm, n // tn),
            in_specs=[pl.BlockSpec((1, tm, kd), xmap),
                      pl.BlockSpec((1, kd, tn), lambda e, i, j: (e, 0, j))],
            out_specs=pl.BlockSpec((1, tm, tn), lambda e, i, j: (e, i, j)),
            out_shape=jax.ShapeDtypeStruct((g, m, n), out_dtype),
            compiler_params=_cparams("parallel", "parallel", "parallel"),
            name="gmm",
        )(x, w)
    xmap = (lambda e, i, j, k: (0, i, k)) if share_x else (lambda e, i, j, k: (e, i, k))
    return pl.pallas_call(
        _mm_kernel,
        grid=(g, m // tm, n // tn, kd // tk),
        in_specs=[pl.BlockSpec((1, tm, tk), xmap),
                  pl.BlockSpec((1, tk, tn), lambda e, i, j, k: (e, k, j))],
        out_specs=pl.BlockSpec((1, tm, tn), lambda e, i, j, k: (e, i, j)),
        out_shape=jax.ShapeDtypeStruct((g, m, n), out_dtype),
        scratch_shapes=[pltpu.VMEM((tm, tn), F32)],
        compiler_params=_cparams("parallel", "parallel", "parallel", "arbitrary"),
        name="gmm",
    )(x, w)


def mm(x, w, out_dtype=F32, **kw):
    return gmm(x[None], w[None], out_dtype, **kw)[0]


def _mm_nt_kernel(x_ref, wt_ref, o_ref, *, sigmoid):
    y = lax.dot_general(x_ref[...], wt_ref[...].astype(BF16), NT_DIMS, preferred_element_type=F32)
    if sigmoid:
        y = jax.nn.sigmoid(y)
    o_ref[...] = y.astype(o_ref.dtype)


def mm_rows_t(x, wt, layer, row0, n_rows, out_dtype=F32, sigmoid=False, tm=2048, tn=256):
    m, kd = x.shape
    tm, tn = _pick(m, tm), _pick(n_rows, tn)
    r0 = row0 // tn
    if layer is None:
        wspec = pl.BlockSpec((tn, kd), lambda i, j: (r0 + j, 0))
    else:
        wspec = pl.BlockSpec((None, tn, kd), lambda i, j: (layer, r0 + j, 0))
    return pl.pallas_call(
        functools.partial(_mm_nt_kernel, sigmoid=sigmoid),
        grid=(m // tm, n_rows // tn),
        in_specs=[pl.BlockSpec((tm, kd), lambda i, j: (i, 0)), wspec],
        out_specs=pl.BlockSpec((tm, tn), lambda i, j: (i, j)),
        out_shape=jax.ShapeDtypeStruct((m, n_rows), out_dtype),
        compiler_params=_cparams("parallel", "parallel"),
        name="mm_rows_t",
    )(x, wt)


def _mm_act_kernel(x_ref, w_ref, o_ref, *, sigmoid):
    y = jnp.dot(x_ref[...], w_ref[...].astype(BF16), preferred_element_type=F32)
    if sigmoid:
        y = jax.nn.sigmoid(y)
    o_ref[...] = y.astype(o_ref.dtype)


def mm_cols(x, w, layer, col0, n_cols, out_dtype=F32, sigmoid=False, tm=2048, tn=256):
    m, kd = x.shape
    tm, tn = _pick(m, tm), _pick(n_cols, tn)
    c0 = col0 // tn
    if layer is None:
        wspec = pl.BlockSpec((kd, tn), lambda i, j: (0, c0 + j))
    else:
        wspec = pl.BlockSpec((None, kd, tn), lambda i, j: (layer, 0, c0 + j))
    return pl.pallas_call(
        functools.partial(_mm_act_kernel, sigmoid=sigmoid),
        grid=(m // tm, n_cols // tn),
        in_specs=[pl.BlockSpec((tm, kd), lambda i, j: (i, 0)), wspec],
        out_specs=pl.BlockSpec((tm, tn), lambda i, j: (i, j)),
        out_shape=jax.ShapeDtypeStruct((m, n_cols), out_dtype),
        compiler_params=_cparams("parallel", "parallel"),
        name="mm_cols",
    )(x, w)


class Layout:
    def __init__(self, n_p, l_p, n_s, l_s):
        self.n_p, self.l_p, self.n_s, self.l_s = n_p, l_p, n_s, l_s
        self.tp = n_p * l_p
        self.ts = n_s * l_s
        self.t = self.tp + self.ts

    def group_of_block(self, i, rows):
        bp = self.tp // rows
        return jnp.where(i < bp, 0, 1 + (i - bp) // (self.l_s // rows))

    def seq_edges(self, i, rows):
        bp = self.tp // rows
        per_p, per_s = self.l_p // rows, self.l_s // rows
        first = jnp.where(i < bp, i % per_p == 0, (i - bp) % per_s == 0)
        last = jnp.where(i < bp, i % per_p == per_p - 1, (i - bp) % per_s == per_s - 1)
        return first, last


ROWS = 256


def _modulate_kernel(x_ref, sc_ref, sh_ref, o_ref):
    o_ref[...] = (x_ref[...] * (1.0 + sc_ref[0]) + sh_ref[0]).astype(o_ref.dtype)


def modulate(x, mod, sec_sc, sec_sh, lay):
    t, d = x.shape
    grp = lambda i: lay.group_of_block(i, ROWS)
    return pl.pallas_call(
        _modulate_kernel,
        grid=(t // ROWS,),
        in_specs=[pl.BlockSpec((ROWS, d), lambda i: (i, 0)),
                  pl.BlockSpec((1, 1, d), lambda i: (grp(i), 0, sec_sc)),
                  pl.BlockSpec((1, 1, d), lambda i: (grp(i), 0, sec_sh))],
        out_specs=pl.BlockSpec((ROWS, d), lambda i: (i, 0)),
        out_shape=jax.ShapeDtypeStruct((t, d), BF16),
        compiler_params=_cparams("parallel"),
        name="modulate",
    )(x, mod, mod)


def _ln_mod_kernel(*refs, with_mod):
    if with_mod:
        x_ref, m_ref, gt_ref, lg_ref, lb_ref, sc_ref, sh_ref, o_ref, om_ref = refs
    else:
        x_ref, m_ref, gt_ref, lg_ref, lb_ref, o_ref = refs
    r = ALPHA * x_ref[...] + gt_ref[0] * m_ref[...]
    mu = jnp.mean(r, axis=-1, keepdims=True)
    dlt = r - mu
    var = jnp.mean(dlt * dlt, axis=-1, keepdims=True)
    y = dlt * lax.rsqrt(var + LN_EPS) * lg_ref[...] + lb_ref[...]
    o_ref[...] = y
    if with_mod:
        om_ref[...] = (y * (1.0 + sc_ref[0]) + sh_ref[0]).astype(om_ref.dtype)


def ln_mod(x, m, mod, sec_gate, ln_g, ln_b, lay, mod_next=None, sec_sc=0, sec_sh=0):
    t, d = x.shape
    rows = ROWS // 2
    grp = lambda i: lay.group_of_block(i, rows)
    with_mod = mod_next is not None
    row_spec = pl.BlockSpec((rows, d), lambda i: (i, 0))
    vec_spec = pl.BlockSpec((1, d), lambda i: (0, 0))
    in_specs = [row_spec, row_spec,
                pl.BlockSpec((1, 1, d), lambda i: (grp(i), 0, sec_gate)), vec_spec, vec_spec]
    args = [x, m, mod, ln_g.reshape(1, d), ln_b.reshape(1, d)]
    out_specs = [row_spec]
    out_shape = [jax.ShapeDtypeStruct((t, d), F32)]
    if with_mod:
        in_specs += [pl.BlockSpec((1, 1, d), lambda i: (grp(i), 0, sec_sc)),
                     pl.BlockSpec((1, 1, d), lambda i: (grp(i), 0, sec_sh))]
        args += [mod_next, mod_next]
        out_specs.append(row_spec)
        out_shape.append(jax.ShapeDtypeStruct((t, d), BF16))
    res = pl.pallas_call(
        functools.partial(_ln_mod_kernel, with_mod=with_mod),
        grid=(t // rows,),
        in_specs=in_specs, out_specs=out_specs, out_shape=out_shape,
        compiler_params=_cparams("parallel"),
        name="ln_mod",
    )(*args)
    return (res[0], res[1]) if with_mod else (res[0], None)


def _qkv_prep_kernel(u_ref, cos_ref, sin_ref, qn_ref, kn_ref, q_ref, kr_ref, vb_ref, kf_ref, vf_ref):
    cos = cos_ref[...]
    sin = sin_ref[...]
    lane = lax.broadcasted_iota(jnp.int32, cos.shape, 1)
    lane_lo = (lane % (2 * ROT_FREQS)) < ROT_FREQS

    def norm(x, g):
        return x * lax.rsqrt(jnp.mean(x * x, axis=-1, keepdims=True) + RMS_EPS) * g

    def rope(x):
        sw = jnp.where(lane_lo, pltpu.roll(x, LANES - ROT_FREQS, 1), pltpu.roll(x, ROT_FREQS, 1))
        return x * cos + sw * sin

    for h in range(N_HEADS):
        sl = slice(Q_OFF + h * HEAD_DIM, Q_OFF + (h + 1) * HEAD_DIM)
        q = rope(norm(u_ref[:, sl], qn_ref[...])) * (HEAD_DIM ** -0.5)
        q_ref[:, h * HEAD_DIM:(h + 1) * HEAD_DIM] = q.astype(q_ref.dtype)
    for h in range(N_KV_HEADS):
        o = slice(h * HEAD_DIM, (h + 1) * HEAD_DIM)
        kk = norm(u_ref[:, K_OFF + h * HEAD_DIM:K_OFF + (h + 1) * HEAD_DIM], kn_ref[...])
        kf_ref[:, o] = kk
        kr_ref[:, o] = rope(kk).astype(kr_ref.dtype)
        vv = u_ref[:, V_OFF + h * HEAD_DIM:V_OFF + (h + 1) * HEAD_DIM]
        vf_ref[:, o] = vv
        vb_ref[:, o] = vv.astype(vb_ref.dtype)


def qkv_prep(u, cos_t, sin_t, q_norm, k_norm):
    t = u.shape[0]
    row = lambda w: pl.BlockSpec((ROWS, w), lambda i: (i, 0))
    vec = pl.BlockSpec((1, HEAD_DIM), lambda i: (0, 0))
    return pl.pallas_call(
        _qkv_prep_kernel,
        grid=(t // ROWS,),
        in_specs=[row(HY_OFF), row(HEAD_DIM), row(HEAD_DIM), vec, vec],
        out_specs=[row(ATT_W), row(KV_W), row(KV_W), row(KV_W), row(KV_W)],
        out_shape=[jax.ShapeDtypeStruct((t, ATT_W), BF16), jax.ShapeDtypeStruct((t, KV_W), BF16),
                   jax.ShapeDtypeStruct((t, KV_W), BF16), jax.ShapeDtypeStruct((t, KV_W), F32),
                   jax.ShapeDtypeStruct((t, KV_W), F32)],
        compiler_params=_cparams("parallel"),
        name="qkv_prep",
    )(u, cos_t, sin_t, q_norm.reshape(1, HEAD_DIM), k_norm.reshape(1, HEAD_DIM))


def _attn_kernel(*refs, aliased):
    q_ref, k_ref, v_ref = refs[:3]
    o_ref = refs[-1]
    k = k_ref[0]
    v = v_ref[0]
    for r in range(KV_REP):
        sl = slice(r * HEAD_DIM, (r + 1) * HEAD_DIM)
        s = lax.dot_general(q_ref[:, sl], k, NT_DIMS, preferred_element_type=F32)
        m = jnp.max(s, axis=1, keepdims=True)
        p = jnp.exp(s - m)
        l = jnp.sum(p, axis=1, keepdims=True)
        o = jnp.dot(p.astype(BF16), v, preferred_element_type=F32)
        o_ref[:, sl] = (o / l).astype(o_ref.dtype)


def attention(q, k, v, row0, n_seq, l_q, tq, prev=None):
    t = q.shape[0]
    l_k = k.shape[1]
    gw = KV_REP * HEAD_DIM
    nq = l_q // tq
    b0 = row0 // tq
    qmap = lambda b, g, i: (b0 + b * nq + i, g)
    in_specs = [pl.BlockSpec((tq, gw), qmap),
                pl.BlockSpec((1, l_k, HEAD_DIM), lambda b, g, i: (b, 0, g)),
                pl.BlockSpec((1, l_k, HEAD_DIM), lambda b, g, i: (b, 0, g))]
    args = [q, k, v]
    aliases = {}
    if prev is not None:
        in_specs.append(pl.BlockSpec(memory_space=pl.ANY))
        args.append(prev)
        aliases = {3: 0}
    return pl.pallas_call(
        functools.partial(_attn_kernel, aliased=prev is not None),
        grid=(n_seq, N_KV_HEADS, nq),
        in_specs=in_specs,
        out_specs=pl.BlockSpec((tq, gw), qmap),
        out_shape=jax.ShapeDtypeStruct((t, ATT_W), BF16),
        input_output_aliases=aliases,
        compiler_params=_cparams("parallel", "parallel", "parallel"),
        name="attention",
    )(*args)


CONV_ROWS = 1024


def _conv3_kernel(x_ref, prev_ref, next_ref, w_ref, b_ref, *o_refs, lay, silu):
    i = pl.program_id(0)
    x = x_ref[...]
    rows = lax.broadcasted_iota(jnp.int32, (CONV_ROWS, 1), 0)
    grow = rows + i * CONV_ROWS
    in_p = grow < lay.tp
    pos = jnp.where(in_p, grow & (lay.l_p - 1), (grow - lay.tp) & (lay.l_s - 1))
    last = jnp.where(in_p, lay.l_p - 1, lay.l_s - 1)
    xm1 = jnp.where(rows == 0, prev_ref[SUBLANES - 1:SUBLANES, :], pltpu.roll(x, 1, 0))
    xp1 = jnp.where(rows == CONV_ROWS - 1, next_ref[0:1, :], pltpu.roll(x, CONV_ROWS - 1, 0))
    xm1 = jnp.where(pos == 0, 0.0, xm1)
    xp1 = jnp.where(pos == last, 0.0, xp1)
    y = xm1 * w_ref[0:1, :] + x * w_ref[1:2, :] + xp1 * w_ref[2:3, :] + b_ref[...]
    if silu:
        y = y * jax.nn.sigmoid(y)
    for o_ref in o_refs:
        o_ref[...] = y.astype(o_ref.dtype)


def conv3(u, col0, width, w, b, lay, silu, out_dtypes=(F32,), tc=1024):
    t = u.shape[0]
    assert lay.l_p & (lay.l_p - 1) == 0 and lay.l_s & (lay.l_s - 1) == 0
    c0 = col0 // tc
    sub = CONV_ROWS // SUBLANES
    n_sub = t // SUBLANES
    res = pl.pallas_call(
        functools.partial(_conv3_kernel, lay=lay, silu=silu),
        grid=(t // CONV_ROWS, width // tc),
        in_specs=[pl.BlockSpec((CONV_ROWS, tc), lambda i, j: (i, c0 + j)),
                  pl.BlockSpec((SUBLANES, tc), lambda i, j: (jnp.maximum(i * sub - 1, 0), c0 + j)),
                  pl.BlockSpec((SUBLANES, tc), lambda i, j: (jnp.minimum((i + 1) * sub, n_sub - 1), c0 + j)),
                  pl.BlockSpec((3, tc), lambda i, j: (0, j)),
                  pl.BlockSpec((1, tc), lambda i, j: (0, j))],
        out_specs=[pl.BlockSpec((CONV_ROWS, tc), lambda i, j: (i, j)) for _ in out_dtypes],
        out_shape=[jax.ShapeDtypeStruct((t, width), dt) for dt in out_dtypes],
        compiler_params=_cparams("parallel", "parallel"),
        name="conv3",
    )(u, u, u, w, b.reshape(1, width))
    return res if len(out_dtypes) > 1 else res[0]


def _dt_prep_kernel(raw_ref, bias_ref, a_ref, o_ref):
    x = raw_ref[...] + bias_ref[...]
    dt = jnp.maximum(x, 0.0) + jnp.log1p(jnp.exp(-jnp.abs(x)))
    o_ref[...] = jnp.where(lax.broadcasted_iota(jnp.int32, x.shape, 1) < 2 * SSM_HEADS,
                           dt, pltpu.roll(dt, 2 * SSM_HEADS, 1) * a_ref[...])


def dt_prep(raw, dt_bias, a_log):
    t = raw.shape[0]
    nh2 = 2 * SSM_HEADS
    bias = jnp.zeros((1, LANES), F32).at[0, :nh2].set(dt_bias.reshape(-1))
    a = jnp.zeros((1, LANES), F32).at[0, nh2:2 * nh2].set(-jnp.exp(a_log.reshape(-1)))
    return pl.pallas_call(
        _dt_prep_kernel,
        grid=(t // ROWS,),
        in_specs=[pl.BlockSpec((ROWS, LANES), lambda i: (i, 0)),
                  pl.BlockSpec((1, LANES), lambda i: (0, 0)),
                  pl.BlockSpec((1, LANES), lambda i: (0, 0))],
        out_specs=pl.BlockSpec((ROWS, LANES), lambda i: (i, 0)),
        out_shape=jax.ShapeDtypeStruct((t, LANES), F32),
        compiler_params=_cparams("parallel"),
        name="dt_prep",
    )(raw, bias, a)


def _prefix_sum(x, axis):
    idx = lax.broadcasted_iota(jnp.int32, x.shape, axis)
    d = 1
    while d < SSM_CHUNK:
        x = x + jnp.where(idx >= d, pltpu.roll(x, d, axis), 0.0)
        d *= 2
    return x


def _ssd_kernel(*refs, n_chunks, aliased):
    xs_ref, xst_ref, b_ref, c_ref, dtc_ref, dtr_ref, init_ref = refs[:7]
    y_ref, fin_ref, st_ref = refs[-3:]
    d = pl.program_id(1)
    c = pl.program_id(2)
    tt = SSM_CHUNK
    nh = SSM_HEADS

    @pl.when(c == 0)
    def _():
        st_ref[...] = init_ref[0, 0]

    fwd = d == 0
    dtc = dtc_ref[...]
    dtr = dtr_ref[...]
    da_c = jnp.where(fwd, dtc[:, 2 * nh:3 * nh], dtc[:, 3 * nh:4 * nh])
    dtv_r = jnp.where(fwd, dtr[0:nh, :], dtr[nh:2 * nh, :])
    da_r = jnp.where(fwd, dtr[2 * nh:3 * nh, :], dtr[3 * nh:4 * nh, :])
    pc = _prefix_sum(da_c, 0)
    pr = _prefix_sum(da_r, 1)
    tot_c = pc[tt - 1:tt, :]
    tot_r = pr[:, tt - 1:tt]
    acs_c = jnp.where(fwd, pc, tot_c - pc + da_c)
    acs_r = jnp.where(fwd, pr, tot_r - pr + da_r)
    li = lax.broadcasted_iota(jnp.int32, (tt, tt), 0)
    si = lax.broadcasted_iota(jnp.int32, (tt, tt), 1)
    mask = jnp.where(fwd, li - si, si - li) >= 0
    w_r = dtv_r * jnp.exp(tot_r - acs_r)
    cdec_r = jnp.exp(tot_r)
    ns = SSM_STATE
    hp = SSM_HEADDIM
    for g in range(SSM_GROUPS):
        bg = b_ref[:, g * ns:(g + 1) * ns].astype(BF16)
        cg32 = c_ref[:, g * ns:(g + 1) * ns]
        cb = lax.dot_general(cg32.astype(BF16), bg, NT_DIMS, preferred_element_type=F32)
        for r in range(SSM_REP):
            h = g * SSM_REP + r
            acs_l = jnp.broadcast_to(acs_c[:, h:h + 1], (tt, tt))
            dec = jnp.exp(jnp.where(mask, acs_l - acs_r[h:h + 1, :], -jnp.inf))
            m = (cb * dec * dtv_r[h:h + 1, :]).astype(BF16)
            c_in = (cg32 * jnp.exp(acs_l)).astype(BF16)
            state = st_ref[h]
            y = jnp.dot(m, xs_ref[:, h * hp:(h + 1) * hp].astype(BF16), preferred_element_type=F32)
            y += lax.dot_general(c_in, state.astype(BF16), NT_DIMS, preferred_element_type=F32)
            y_ref[0, :, h * hp:(h + 1) * hp] = y
            xw = (xst_ref[h * hp:(h + 1) * hp, :] * w_r[h:h + 1, :]).astype(BF16)
            st_ref[h] = state * cdec_r[h:h + 1, :] + jnp.dot(xw, bg, preferred_element_type=F32)

    @pl.when(c == n_chunks - 1)
    def _():
        fin_ref[0, 0] = st_ref[...]


def ssd(xbc, xst, dtc, dtr, init, row0, n_seq, length, prev=None):
    t = xbc.shape[0]
    tt = SSM_CHUNK
    nc = length // tt
    b0 = row0 // tt
    gn = SSM_GROUPS * SSM_STATE

    def blk(b, d, c):
        return b0 + b * nc + jnp.where(d == 0, c, nc - 1 - c)

    st_shape = (SSM_HEADS, SSM_HEADDIM, SSM_STATE)
    st_spec = pl.BlockSpec((1, 1) + st_shape, lambda b, d, c: (b, d, 0, 0, 0))
    in_specs = [pl.BlockSpec((tt, SSM_W), lambda b, d, c: (blk(b, d, c), 0)),
                pl.BlockSpec((SSM_W, tt), lambda b, d, c: (0, blk(b, d, c))),
                pl.BlockSpec((tt, gn), lambda b, d, c: (blk(b, d, c), SSM_W // gn)),
                pl.BlockSpec((tt, gn), lambda b, d, c: (blk(b, d, c), SSM_W // gn + 1)),
                pl.BlockSpec((tt, LANES), lambda b, d, c: (blk(b, d, c), 0)),
                pl.BlockSpec((LANES, tt), lambda b, d, c: (0, blk(b, d, c))),
                st_spec]
    args = [xbc, xst, xbc, xbc, dtc, dtr, init]
    aliases = {}
    if prev is not None:
        in_specs.append(pl.BlockSpec(memory_space=pl.ANY))
        args.append(prev)
        aliases = {7: 0}
    return pl.pallas_call(
        functools.partial(_ssd_kernel, n_chunks=nc, aliased=prev is not None),
        grid=(n_seq, 2, nc),
        in_specs=in_specs,
        out_specs=[pl.BlockSpec((1, tt, SSM_W), lambda b, d, c: (d, blk(b, d, c), 0)), st_spec],
        out_shape=[jax.ShapeDtypeStruct((2, t, SSM_W), F32),
                   jax.ShapeDtypeStruct((n_seq, 2) + st_shape, F32)],
        scratch_shapes=[pltpu.VMEM(st_shape, F32)],
        input_output_aliases=aliases,
        compiler_params=_cparams("parallel", "parallel", "arbitrary"),
        name="ssd",
    )(*args)


def _ssd_gate_kernel(y_ref, xs_ref, zlo_ref, zhi_ref, d_ref, g_ref, o_ref):
    z = jnp.concatenate([zlo_ref[...], zhi_ref[...]], axis=1)
    y = (y_ref[0] + y_ref[1] + d_ref[...] * xs_ref[...]) * (z * jax.nn.sigmoid(z))
    y = y * lax.rsqrt(jnp.mean(y * y, axis=-1, keepdims=True) + RMS_EPS) * g_ref[...]
    o_ref[...] = y.astype(o_ref.dtype)


def ssd_gate(y2, xbc, u, ssm_d, ssm_norm):
    t = xbc.shape[0]
    half = SSM_W // 2
    zb = Z_OFF // half
    vec = pl.BlockSpec((1, SSM_W), lambda i: (0, 0))
    return pl.pallas_call(
        _ssd_gate_kernel,
        grid=(t // ROWS,),
        in_specs=[pl.BlockSpec((2, ROWS, SSM_W), lambda i: (0, i, 0)),
                  pl.BlockSpec((ROWS, SSM_W), lambda i: (i, 0)),
                  pl.BlockSpec((ROWS, half), lambda i: (i, zb)),
                  pl.BlockSpec((ROWS, half), lambda i: (i, zb + 1)),
                  vec, vec],
        out_specs=pl.BlockSpec((ROWS, SSM_W), lambda i: (i, 0)),
        out_shape=jax.ShapeDtypeStruct((t, SSM_W), BF16),
        compiler_params=_cparams("parallel"),
        name="ssd_gate",
    )(y2, xbc, u, u, jnp.repeat(ssm_d, SSM_HEADDIM).reshape(1, SSM_W), ssm_norm.reshape(1, SSM_W))


def dft_matrices(length):
    blk = 64
    two_l = 2 * length
    k = jnp.arange(length, dtype=jnp.int32)[:, None]
    a = jnp.arange(length // blk, dtype=jnp.int32)[None, :]
    b = jnp.arange(blk, dtype=jnp.int32)[None, :]
    xa = ((k * (a * blk)) % two_l).astype(F32) * (math.pi / length)
    xb = ((k * b) % two_l).astype(F32) * (math.pi / length)
    ca, sa, cb, sb = jnp.cos(xa)[:, :, None], jnp.sin(xa)[:, :, None], jnp.cos(xb)[:, None, :], jnp.sin(xb)[:, None, :]
    cos = (ca * cb - sa * sb).reshape(length, length)
    sin = (sa * cb + ca * sb).reshape(length, length)
    idx = jnp.arange(length, dtype=jnp.int32)
    alt = jnp.where(idx % 2 == 0, 1.0, -1.0).astype(F32)
    sin_fwd = jnp.where(idx[:, None] == 0, alt[None, :], sin)
    sin_inv = jnp.where(idx[None, :] == 0, alt[:, None], sin)
    return jnp.stack([cos, sin_fwd]).astype(BF16), jnp.stack([cos, sin_inv]).astype(BF16)


def _hymlp_kernel(bands_ref, w1_ref, b1_ref, fr_ref, w2_ref, b2_ref, o_ref, *, length, tr):
    i = pl.program_id(0)
    t = (lax.broadcasted_iota(jnp.int32, (tr, 1), 0) + i * tr).astype(F32) / length
    lane = lax.broadcasted_iota(jnp.int32, (tr, LANES), 1)
    ang = (2.0 * math.pi * t) * bands_ref[...]
    feats = jnp.where(lane < HY_BANDS, jnp.cos(ang),
                      jnp.where(lane < 2 * HY_BANDS, jnp.sin(ang), jnp.where(lane == 2 * HY_BANDS, t, 0.0)))
    fr = fr_ref[...]
    hid = jnp.sin(fr * (jnp.dot(feats.astype(BF16), w1_ref[...].astype(BF16),
                                preferred_element_type=F32) + b1_ref[...]))
    o_ref[...] = jnp.sin(fr * (jnp.dot(hid.astype(BF16), w2_ref[...].astype(BF16),
                                       preferred_element_type=F32) + b2_ref[...]))


def _hyfilt_kernel(hid_ref, wf_ref, wb_ref, bf_ref, bb_ref, dl_ref, o_ref, nrm_ref, nyq_ref, *, length, tr):
    i = pl.program_id(2)
    hid = hid_ref[...].astype(BF16)
    t_idx = lax.broadcasted_iota(jnp.int32, (tr, 1), 0) + i * tr
    t = t_idx.astype(F32) / length
    win = jnp.exp(-t * dl_ref[...])
    hf = (jnp.dot(hid, wf_ref[...].astype(BF16), preferred_element_type=F32) + bf_ref[...]) * win
    hb = (jnp.dot(hid, wb_ref[...].astype(BF16), preferred_element_type=F32) + bb_ref[...]) * win
    hb = jnp.where(t_idx == 0, 0.0, hb)
    o_ref[0] = (hf + hb).astype(o_ref.dtype)
    o_ref[1] = (hf - hb).astype(o_ref.dtype)
    sign = jnp.where(t_idx % 2 == 0, 1.0, -1.0)

    @pl.when(i == 0)
    def _():
        nrm_ref[...] = jnp.zeros_like(nrm_ref)
        nyq_ref[...] = jnp.zeros_like(nyq_ref)

    nrm_ref[...] += jnp.sum(jnp.abs(hf) + jnp.abs(hb), axis=0, keepdims=True)
    nyq_ref[...] += jnp.sum(sign * (hf + hb), axis=0, keepdims=True)


def hyena_filters(length, p, fwd):
    bands = jnp.linspace(1e-4, HY_BANDS - 1, HY_BANDS, dtype=F32)
    bands = jnp.zeros((1, LANES), F32).at[0, :2 * HY_BANDS].set(jnp.concatenate([bands, bands]))
    w1 = p['hy_w1']
    ffn = w1.shape[1]
    w1p = jnp.zeros((LANES, ffn), F32).at[:2 * HY_BANDS].set(w1[1:]).at[2 * HY_BANDS].set(w1[0])
    deltas = jnp.abs(jnp.linspace(math.log(HY_DECAY_TARGET) / HY_SLOW_DECAY,
                                  math.log(HY_DECAY_TARGET) / HY_FAST_DECAY, HY_W, dtype=F32)).reshape(1, HY_W)
    tr, tc = _pick(length, 256), 512
    nj = HY_W // tc
    w3, b3 = p['hy_w3'], p['hy_b3'].reshape(1, -1)
    ow = HY_ORDER * HY_W
    full = lambda shape: pl.BlockSpec(shape, lambda i: (0, 0))
    hid = pl.pallas_call(
        functools.partial(_hymlp_kernel, length=length, tr=tr),
        grid=(length // tr,),
        in_specs=[full((1, LANES)), full((LANES, ffn)), full((1, ffn)), full((1, ffn)), full((ffn, ffn)),
                  full((1, ffn))],
        out_specs=pl.BlockSpec((tr, ffn), lambda i: (i, 0)),
        out_shape=jax.ShapeDtypeStruct((length, ffn), F32),
        compiler_params=_cparams("parallel"),
        name="hyena_mlp",
    )(bands, w1p, p['hy_b1'].reshape(1, ffn), p['hy_freq'].reshape(1, ffn), p['hy_w2'],
      p['hy_b2'].reshape(1, ffn))
    hsd, nrm, nyq = pl.pallas_call(
        functools.partial(_hyfilt_kernel, length=length, tr=tr),
        grid=(HY_ORDER, nj, length // tr),
        in_specs=[pl.BlockSpec((tr, ffn), lambda n, j, i: (i, 0)),
                  pl.BlockSpec((ffn, tc), lambda n, j, i: (0, (2 * n) * nj + j)),
                  pl.BlockSpec((ffn, tc), lambda n, j, i: (0, (2 * n + 1) * nj + j)),
                  pl.BlockSpec((1, tc), lambda n, j, i: (0, (2 * n) * nj + j)),
                  pl.BlockSpec((1, tc), lambda n, j, i: (0, (2 * n + 1) * nj + j)),
                  pl.BlockSpec((1, tc), lambda n, j, i: (0, j))],
        out_specs=[pl.BlockSpec((2, tr, tc), lambda n, j, i: (0, i, n * nj + j)),
                   pl.BlockSpec((1, tc), lambda n, j, i: (0, n * nj + j)),
                   pl.BlockSpec((1, tc), lambda n, j, i: (0, n * nj + j))],
        out_shape=[jax.ShapeDtypeStruct((2, length, ow), BF16),
                   jax.ShapeDtypeStruct((1, ow), F32), jax.ShapeDtypeStruct((1, ow), F32)],
        compiler_params=_cparams("parallel", "parallel", "arbitrary"),
        name="hyena_filter",
    )(hid, w3, w3, b3, b3, deltas)
    pq = gmm(fwd, hsd)
    return pq, nrm, nyq


def _dft_fwd_kernel(f_ref, z_ref, pq_ref, nrm_ref, nyq_ref, uv_ref, *, length, tm):
    i = pl.program_id(0)
    z = z_ref[...]
    a = jnp.dot(f_ref[0], z, preferred_element_type=F32)
    b = jnp.dot(f_ref[1], z, preferred_element_type=F32)
    k_idx = lax.broadcasted_iota(jnp.int32, (tm, 1), 0) + i * tm
    is0 = k_idx == 0
    wk = jnp.where(is0, 1.0, 2.0) * (0.5 / length) / nrm_ref[...]
    pp, qq = pq_ref[0], pq_ref[1]
    uv_ref[0, 0] = (wk * (a * pp - jnp.where(is0, 0.0, b * qq))).astype(uv_ref.dtype)
    uv_ref[0, 1] = (wk * jnp.where(is0, b * nyq_ref[...], a * qq + b * pp)).astype(uv_ref.dtype)


def dft_fwd(fwd, z, pq, nrm, nyq, order, row0, n_seq, length):
    tm, tn = _pick(length, 512), (512 if length >= 1024 else HY_W)
    nj = HY_W // tn
    r0 = row0 // length
    vec = pl.BlockSpec((1, tn), lambda i, b, j: (0, order * nj + j))
    return pl.pallas_call(
        functools.partial(_dft_fwd_kernel, length=length, tm=tm),
        grid=(length // tm, n_seq, nj),
        in_specs=[pl.BlockSpec((2, tm, length), lambda i, b, j: (0, i, 0)),
                  pl.BlockSpec((length, tn), lambda i, b, j: (r0 + b, j)),
                  pl.BlockSpec((2, tm, tn), lambda i, b, j: (0, i, order * nj + j)),
                  vec, vec],
        out_specs=pl.BlockSpec((1, 2, tm, tn), lambda i, b, j: (b, 0, i, j)),
        out_shape=jax.ShapeDtypeStruct((n_seq, 2, length, HY_W), BF16),
        compiler_params=_cparams("parallel", "parallel", "parallel"),
        name="hyena_dft_fwd",
    )(fwd, z, pq, nrm, nyq)


def _hy_inv_kernel(*refs, n_out):
    f_ref, uv_ref, z_ref, gate_ref, bias_ref = refs[:5]
    o_refs = refs[-n_out:]
    y = jnp.dot(f_ref[0], uv_ref[0, 0], preferred_element_type=F32)
    y += jnp.dot(f_ref[1], uv_ref[0, 1], preferred_element_type=F32)
    y = gate_ref[...] * (y + bias_ref[...] * z_ref[...])
    for o_ref in o_refs:
        o_ref[...] = y.astype(o_ref.dtype)


def hy_inverse(inv, uv, z, zcol0, gate, gcol0, bias, row0, length, out_dtypes, prev=None):
    t = z.shape[0]
    n_seq = uv.shape[0]
    tm, tn = _pick(length, 512), (512 if length >= 1024 else HY_W)
    r0 = row0 // tm
    ni = length // tm
    rmap = lambda c0: (lambda i, b, j: (r0 + b * ni + i, c0 // tn + j))
    in_specs = [pl.BlockSpec((2, tm, length), lambda i, b, j: (0, i, 0)),
                pl.BlockSpec((1, 2, length, tn), lambda i, b, j: (b, 0, 0, j)),
                pl.BlockSpec((tm, tn), rmap(zcol0)),
                pl.BlockSpec((tm, tn), rmap(gcol0)),
                pl.BlockSpec((1, tn), lambda i, b, j: (0, j))]
    args = [inv, uv, z, gate, bias.reshape(1, HY_W)]
    aliases = {}
    if prev is not None:
        for n, pv in enumerate(prev):
            in_specs.append(pl.BlockSpec(memory_space=pl.ANY))
            args.append(pv)
            aliases[5 + n] = n
    return pl.pallas_call(
        functools.partial(_hy_inv_kernel, n_out=len(out_dtypes)),
        grid=(ni, n_seq, HY_W // tn),
        in_specs=in_specs,
        out_specs=[pl.BlockSpec((tm, tn), rmap(0)) for _ in out_dtypes],
        out_shape=[jax.ShapeDtypeStruct((t, HY_W), dt) for dt in out_dtypes],
        input_output_aliases=aliases,
        compiler_params=_cparams("parallel", "parallel", "parallel"),
        name="hyena_dft_inv",
    )(*args)


def hyena_group(v32, v16, x12, filt, mats, p, row0, n_seq, length, prev):
    fwd, inv = mats
    pq, nrm, nyq = filt
    prev1, prev2 = prev
    uv = dft_fwd(fwd, v16, pq, nrm, nyq, 0, row0, n_seq, length)
    z1 = hy_inverse(inv, uv, v32, 0, x12, 0, p['hy_bias'][0], row0, length, (F32, BF16), prev1)
    uv = dft_fwd(fwd, z1[1], pq, nrm, nyq, 1, row0, n_seq, length)
    z2 = hy_inverse(inv, uv, z1[0], 0, x12, HY_W, p['hy_bias'][1], row0, length, (BF16,), prev2)
    return z1, z2


def _merge_kernel(a_ref, h_ref, s_ref, wa_ref, wh_ref, ws_ref, ga_ref, gh_ref, gs_ref, o_ref):
    acc = ga_ref[...].astype(F32) * jnp.dot(a_ref[...], wa_ref[...], preferred_element_type=F32)
    acc += gh_ref[...].astype(F32) * jnp.dot(h_ref[...], wh_ref[...], preferred_element_type=F32)
    acc += gs_ref[...].astype(F32) * jnp.dot(s_ref[...], ws_ref[...], preferred_element_type=F32)
    o_ref[...] = acc.astype(o_ref.dtype)


def branch_merge(att, hy, ssm, wa, wh, ws, gate_logits, tm=512, tn=512):
    t, kd = att.shape
    d = wa.shape[1]
    nj = d // tn
    xs = pl.BlockSpec((tm, kd), lambda j, i: (i, 0))
    ws_ = pl.BlockSpec((kd, tn), lambda j, i: (0, j))
    gs = lambda b: pl.BlockSpec((tm, tn), lambda j, i: (i, b * nj + j))
    return pl.pallas_call(
        _merge_kernel,
        grid=(nj, t // tm),
        in_specs=[xs, xs, xs, ws_, ws_, ws_, gs(0), gs(1), gs(2)],
        out_specs=pl.BlockSpec((tm, tn), lambda j, i: (i, j)),
        out_shape=jax.ShapeDtypeStruct((t, d), BF16),
        compiler_params=_cparams("parallel", "parallel"),
        name="branch_merge",
    )(att, hy, ssm, wa, wh, ws, gate_logits, gate_logits, gate_logits)


def _swiglu_kernel(x_ref, wg_ref, wu_ref, o_ref):
    x = x_ref[0]
    g = jnp.dot(x, wg_ref[0].astype(BF16), preferred_element_type=F32)
    u = jnp.dot(x, wu_ref[0].astype(BF16), preferred_element_type=F32)
    o_ref[0] = (g * jax.nn.sigmoid(g) * u).astype(o_ref.dtype)


def expert_swiglu(xs, w_gate, w_up, layer, tn=256):
    e, m, d = xs.shape
    f = w_gate.shape[3]
    wspec = pl.BlockSpec((None, 1, d, tn), lambda e, j: (layer, e, 0, j))
    return pl.pallas_call(
        _swiglu_kernel,
        grid=(e, f // tn),
        in_specs=[pl.BlockSpec((1, m, d), lambda e, j: (e, 0, 0)), wspec, wspec],
        out_specs=pl.BlockSpec((1, m, tn), lambda e, j: (e, 0, j)),
        out_shape=jax.ShapeDtypeStruct((e, m, f), BF16),
        compiler_params=_cparams("parallel", "parallel"),
        name="expert_swiglu",
    )(xs, w_gate, w_up)


def _down_kernel(h_ref, w_ref, g_ref, o_ref):
    o_ref[0] = jnp.dot(h_ref[0], w_ref[0].astype(BF16), preferred_element_type=F32) * g_ref[0]


def expert_down(hid, w_down, gates, layer, tn=512):
    e, m, f = hid.shape
    d = w_down.shape[3]
    return pl.pallas_call(
        _down_kernel,
        grid=(e, d // tn),
        in_specs=[pl.BlockSpec((1, m, f), lambda e, j: (e, 0, 0)),
                  pl.BlockSpec((None, 1, f, tn), lambda e, j: (layer, e, 0, j)),
                  pl.BlockSpec((1, m, 1), lambda e, j: (e, 0, 0))],
        out_specs=pl.BlockSpec((1, m, tn), lambda e, j: (e, 0, j)),
        out_shape=jax.ShapeDtypeStruct((e, m, d), F32),
        compiler_params=_cparams("parallel", "parallel"),
        name="expert_down",
    )(hid, w_down, gates)


def ec_moe(xm, p, lay):
    t, d = xm.shape
    w_r = jnp.zeros((d, LANES), BF16).at[:, :N_EXPERTS].set(p['w_router'].astype(BF16))
    logits = mm(xm, w_r, tn=LANES)[:, :N_EXPERTS]
    aff = jax.nn.softmax(logits, axis=-1)
    gates, rows = [], []
    for row0, n_seq, length in ((0, lay.n_p, lay.l_p), (lay.tp, lay.n_s, lay.l_s)):
        cap = EC_CAPACITY * length // N_EXPERTS
        a = aff[row0:row0 + n_seq * length].reshape(n_seq, length, N_EXPERTS)
        g, idx = lax.top_k(jnp.swapaxes(a, 1, 2), cap)
        idx = idx + (row0 + jnp.arange(n_seq, dtype=idx.dtype) * length)[:, None, None]
        gates.append(jnp.swapaxes(g, 0, 1).reshape(N_EXPERTS, n_seq * cap))
        rows.append(jnp.swapaxes(idx, 0, 1).reshape(N_EXPERTS, n_seq * cap))
    gates = jnp.concatenate(gates, axis=1)
    rows = jnp.concatenate(rows, axis=1)
    xs = xm.at[rows.reshape(-1)].get(mode='promise_in_bounds').reshape(N_EXPERTS, -1, d)
    hid = expert_swiglu(xs, p['w_gate'], p['w_up'], p['layer'])
    y = expert_down(hid, p['w_down'], gates[..., None], p['layer'])
    return jnp.zeros((t, d), F32).at[rows.reshape(-1)].add(y.reshape(-1, d), mode='promise_in_bounds')


def rope_tables(lay):
    pos = jnp.arange(lay.l_s)
    row = (pos // GRID_W).astype(F32)
    col = (pos % GRID_W).astype(F32)
    inv = ROPE_THETA ** (-jnp.arange(ROT_FREQS, dtype=F32) / ROT_FREQS)
    ang = jnp.concatenate([row[:, None] * inv] * 2 + [col[:, None] * inv] * 2, axis=1)
    sign = jnp.where((jnp.arange(HEAD_DIM) % (2 * ROT_FREQS)) < ROT_FREQS, -1.0, 1.0).astype(F32)
    cos_s, sin_s = jnp.cos(ang), jnp.sin(ang) * sign
    cos_t = jnp.concatenate([jnp.ones((lay.tp, HEAD_DIM), F32)] + [cos_s] * lay.n_s, axis=0)
    sin_t = jnp.concatenate([jnp.zeros((lay.tp, HEAD_DIM), F32)] + [sin_s] * lay.n_s, axis=0)
    return cos_t, sin_t


def trunk_layer(x, xm, p, mod, mod_next, consts, lay, cache_k, cache_v, state_ssm):
    cos_t, sin_t, mats_p, mats_s = consts
    tp = lay.tp
    u = mm_rows_t(xm, p['w_in_t'], p['layer'], 0, DT_OFF, tm=1024, tn=512)
    dt_raw = mm_rows_t(xm, p['w_in_t'], p['layer'], DT_OFF, LANES, tn=LANES)
    gates = mm_rows_t(xm, p['w_in_gate_t'], None, 0, N_BRANCH * D_MODEL, out_dtype=BF16, sigmoid=True,
                      tm=1024, tn=512)

    q, kr, vb, kf, vf = qkv_prep(u, cos_t, sin_t, p['q_norm'], p['k_norm'])
    k_p = kr[:tp].reshape(lay.n_p, lay.l_p, KV_W)
    v_p = vb[:tp].reshape(lay.n_p, lay.l_p, KV_W)
    k_s = jnp.concatenate([cache_k.reshape(lay.n_s, -1, KV_W).astype(BF16),
                           kr[tp:].reshape(lay.n_s, lay.l_s, KV_W)], axis=1)
    v_s = jnp.concatenate([cache_v.reshape(lay.n_s, -1, KV_W).astype(BF16),
                           vb[tp:].reshape(lay.n_s, lay.l_s, KV_W)], axis=1)
    att = attention(q, k_p, v_p, 0, lay.n_p, lay.l_p, lay.l_p)
    att = attention(q, k_s, v_s, tp, lay.n_s, lay.l_s, ROWS, prev=att)

    v32, v16 = conv3(u, HY_OFF, HY_W, p['hy_conv_w'][:, :HY_W], p['hy_conv_b'][:HY_W], lay, silu=False,
                     out_dtypes=(F32, BF16))
    x12 = conv3(u, HY_OFF + HY_W, 2 * HY_W, p['hy_conv_w'][:, HY_W:], p['hy_conv_b'][HY_W:], lay, silu=False)
    filt_p = hyena_filters(lay.l_p, p, mats_p[0])
    filt_s = hyena_filters(lay.l_s, p, mats_s[0])
    z1, hy = hyena_group(v32, v16, x12, filt_p, mats_p, p, 0, lay.n_p, lay.l_p, (None, None))
    _, hy = hyena_group(v32, v16, x12, filt_s, mats_s, p, tp, lay.n_s, lay.l_s, (z1, hy))
    hy = hy[0]

    xbc = conv3(u, XBC_OFF, SSM_CONV_DIM, p['ssm_conv_w'], p['ssm_conv_b'], lay, silu=True)
    dtc = dt_prep(dt_raw, p['ssm_dt_bias'], p['ssm_a_log'])
    dtr = dtc.T
    zero_state = jnp.zeros((lay.n_p, 2, SSM_HEADS, SSM_HEADDIM, SSM_STATE), F32)
    xst = xbc[:, :SSM_W].T
    y2, states = ssd(xbc, xst, dtc, dtr, zero_state, 0, lay.n_p, lay.l_p)
    y2, _ = ssd(xbc, xst, dtc, dtr, state_ssm, tp, lay.n_s, lay.l_s, prev=y2)
    ssm = ssd_gate(y2, xbc, u, p['ssm_d'], p['ssm_norm'])

    merged = branch_merge(att, hy, ssm, p['w_br_att'], p['w_br_hy'], p['w_br_ssm'], gates)
    m = mm(merged, p['w_out'])
    x1, xm2 = ln_mod(x, m, mod, 2, p['ln1_g'], p['ln1_b'], lay, mod_next=mod, sec_sc=4, sec_sh=3)
    f = ec_moe(xm2, p, lay)
    x2, xm_next = ln_mod(x1, f, mod, 5, p['ln2_g'], p['ln2_b'], lay, mod_next=mod_next, sec_sc=1, sec_sh=0)
    new_k = kf[:tp].reshape(lay.n_p, lay.l_p, N_KV_HEADS, HEAD_DIM)
    new_v = vf[:tp].reshape(lay.n_p, lay.l_p, N_KV_HEADS, HEAD_DIM)
    return x2, xm_next, (new_k, new_v, states)


def kernel(x_prompt, x_sample, cache_k, cache_v, state_ssm, c, c_ctx, w_mod, b_mod, w_in, q_norm, k_norm, hy_conv_w, hy_conv_b, hy_w1, hy_b1, hy_freq, hy_w2, hy_b2, hy_w3, hy_b3, hy_bias, ssm_conv_w, ssm_conv_b, ssm_dt_bias, ssm_a_log, ssm_d, ssm_norm, w_br_att, w_br_hy, w_br_ssm, w_out, ln1_g, ln1_b, w_router, w_gate, w_up, w_down, ln2_g, ln2_b):
    n_p, l_p, d = x_prompt.shape
    n_s, l_s, _ = x_sample.shape
    depth = w_in.shape[0]
    lay = Layout(n_p, l_p, n_s, l_s)
    x = jnp.concatenate([x_prompt.reshape(lay.tp, d), x_sample.reshape(lay.ts, d)], axis=0)

    cond = jnp.zeros((N_COND_PAD, d), F32).at[0].set(c_ctx).at[1:1 + n_s].set(c)
    act = (cond * jax.nn.sigmoid(cond)).astype(BF16)
    mod_all = gmm(act[None], w_mod, tm=N_COND_PAD, tn=2048, tk=1024, share_x=True) + b_mod[:, None, :]
    mods = [mod_all[l].reshape(N_COND_PAD, 1, 6 * d) for l in range(depth)]

    consts = rope_tables(lay) + (dft_matrices(l_p), dft_matrices(l_s))
    xm = modulate(x, mods[0], 1, 0, lay)
    w_in_t = jnp.swapaxes(w_in, 1, 2).astype(BF16)
    new_k, new_v, new_s = [], [], []
    for l in range(depth):
        p = dict(w_in_t=w_in_t, w_in_gate_t=w_in_t[l, GATE_OFF:, :],
                 q_norm=q_norm[l], k_norm=k_norm[l],
                 hy_conv_w=hy_conv_w[l], hy_conv_b=hy_conv_b[l], hy_w1=hy_w1[l], hy_b1=hy_b1[l],
                 hy_freq=hy_freq[l], hy_w2=hy_w2[l], hy_b2=hy_b2[l], hy_w3=hy_w3[l], hy_b3=hy_b3[l],
                 hy_bias=hy_bias[l], ssm_conv_w=ssm_conv_w[l], ssm_conv_b=ssm_conv_b[l],
                 ssm_dt_bias=ssm_dt_bias[l], ssm_a_log=ssm_a_log[l], ssm_d=ssm_d[l], ssm_norm=ssm_norm[l],
                 w_br_att=w_br_att[l].astype(BF16), w_br_hy=w_br_hy[l].astype(BF16),
                 w_br_ssm=w_br_ssm[l].astype(BF16), w_out=w_out[l].astype(BF16),
                 ln1_g=ln1_g[l], ln1_b=ln1_b[l], w_router=w_router[l],
                 w_gate=w_gate, w_up=w_up, w_down=w_down, layer=l,
                 ln2_g=ln2_g[l], ln2_b=ln2_b[l])
        mod_next = mods[l + 1] if l + 1 < depth else None
        x, xm, (k_l, v_l, s_l) = trunk_layer(x, xm, p, mods[l], mod_next, consts, lay,
                                             cache_k[:, l], cache_v[:, l], state_ssm[:, l])
        new_k.append(k_l)
        new_v.append(v_l)
        new_s.append(s_l)
    y_prompt = x[:lay.tp].reshape(n_p, l_p, d)
    y_sample = x[lay.tp:].reshape(n_s, l_s, d)
    return (y_prompt, y_sample, jnp.stack(new_k, axis=1), jnp.stack(new_v, axis=1), jnp.stack(new_s, axis=1))
```
